```python
import jax, jax.numpy as jnp
from jax import lax
import numpy as np

D_MODEL = 2048
BATCH = 2
SEQ = 4096
DEPTH = 1

CHUNK = 64
HEAD_DIM = 128
N_HEADS_SB = 8
N_HEADS_FOX = 8
W_SB = N_HEADS_SB * HEAD_DIM
W_FOX = N_HEADS_FOX * HEAD_DIM
MIX_WIDTH = W_SB + W_FOX
IN_COLS = 3 * W_SB + 3 * W_FOX + N_HEADS_FOX
D_FF = 4 * D_MODEL
Q_BLOCK = 128
EPS = 1e-6

kernel_name = "hybrid_stickbreaking_fox_block"


def rmsnorm(x, g):
    xf = x.astype(jnp.float32)
    var = jnp.mean(xf * xf, axis=-1, keepdims=True)
    return (xf * lax.rsqrt(var + EPS) * g.astype(jnp.float32)).astype(x.dtype)


def head_rmsnorm(o, g):
    of = o.astype(jnp.float32)
    of = of * lax.rsqrt(jnp.mean(of * of, axis=-1, keepdims=True) + EPS)
    b, h, s, d = o.shape
    of = jnp.transpose(of, (0, 2, 1, 3)).reshape(b, s, h * d)
    return (of * g.astype(jnp.float32)).astype(o.dtype)


def split_heads(t, n_heads):
    b, s, _ = t.shape
    return jnp.transpose(t.reshape(b, s, n_heads, HEAD_DIM), (0, 2, 1, 3))


def stick_breaking_attention(q, k, v):
    seq = q.shape[2]
    scale = HEAD_DIM ** -0.5
    outs = []
    for i in range(seq // Q_BLOCK):
        end = (i + 1) * Q_BLOCK
        qb = q[:, :, i * Q_BLOCK:end]
        kb = k[:, :, :end]
        vb = v[:, :, :end]
        z = jnp.einsum('bhqd,bhkd->bhqk', qb, kb).astype(jnp.float32) * scale
        t_pos = i * Q_BLOCK + jnp.arange(Q_BLOCK)
        s_pos = jnp.arange(end)
        causal = s_pos[None, :] < t_pos[:, None]
        log_beta = jax.nn.log_sigmoid(z)
        log_keep = jnp.where(causal, jax.nn.log_sigmoid(-z), 0.0)
        rest = lax.cumsum(log_keep, axis=3, reverse=True) - log_keep
        a = jnp.where(causal, jnp.exp(log_beta + rest), 0.0)
        outs.append(jnp.einsum('bhqk,bhkd->bhqd', a.astype(v.dtype), vb))
    return jnp.concatenate(outs, axis=2)


def forgetting_attention(q, k, v, log_f):
    seq = q.shape[2]
    scale = HEAD_DIM ** -0.5
    c = lax.cumsum(log_f.astype(jnp.float32), axis=2)
    outs = []
    for i in range(seq // Q_BLOCK):
        end = (i + 1) * Q_BLOCK
        qb = q[:, :, i * Q_BLOCK:end]
        kb = k[:, :, :end]
        vb = v[:, :, :end]
        z = jnp.einsum('bhqd,bhkd->bhqk', qb, kb).astype(jnp.float32) * scale
        z = z + c[:, :, i * Q_BLOCK:end, None] - c[:, :, None, :end]
        t_pos = i * Q_BLOCK + jnp.arange(Q_BLOCK)
        s_pos = jnp.arange(end)
        causal = s_pos[None, :] <= t_pos[:, None]
        p = jax.nn.softmax(jnp.where(causal, z, -jnp.inf), axis=-1)
        outs.append(jnp.einsum('bhqk,bhkd->bhqd', p.astype(v.dtype), vb))
    return jnp.concatenate(outs, axis=2)


def setup_inputs(seed: int = 0) -> dict:
    key = jax.random.key(seed)
    ks = jax.random.split(key, 12)
    f32 = jnp.float32
    x = jax.random.normal(ks[0], (BATCH, SEQ, D_MODEL), f32)
    g_attn = 1.0 + 0.05 * jax.random.normal(ks[1], (DEPTH, D_MODEL), f32)
    w_in = jax.random.normal(ks[2], (DEPTH, D_MODEL, IN_COLS), f32) * D_MODEL ** -0.5
    b_f = jax.random.uniform(ks[3], (DEPTH, N_HEADS_FOX), f32, 1.0, 4.0)
    g_out_sb = 1.0 + 0.05 * jax.random.normal(ks[4], (DEPTH, W_SB), f32)
    g_out_fox = 1.0 + 0.05 * jax.random.normal(ks[5], (DEPTH, W_FOX), f32)
    w_out = jax.random.normal(ks[6], (DEPTH, MIX_WIDTH, D_MODEL), f32) * MIX_WIDTH ** -0.5
    g_mlp = 1.0 + 0.05 * jax.random.normal(ks[7], (DEPTH, D_MODEL), f32)
    w_up = jax.random.normal(ks[8], (DEPTH, D_MODEL, D_FF), f32) * D_MODEL ** -0.5
    w_down = jax.random.normal(ks[9], (DEPTH, D_FF, D_MODEL), f32) * D_FF ** -0.5
    g_final = 1.0 + 0.05 * jax.random.normal(ks[10], (D_MODEL,), f32)
    return {"x": x, "g_attn": g_attn, "w_in": w_in, "b_f": b_f,
            "g_out_sb": g_out_sb, "g_out_fox": g_out_fox, "w_out": w_out,
            "g_mlp": g_mlp, "w_up": w_up, "w_down": w_down, "g_final": g_final}


def reference(x, g_attn, w_in, b_f, g_out_sb, g_out_fox, w_out, g_mlp, w_up, w_down, g_final):
    o0 = 3 * W_SB
    o1 = o0 + 3 * W_FOX
    for l in range(DEPTH):
        h = rmsnorm(x, g_attn[l])
        proj = jnp.einsum('bsd,de->bse', h, w_in[l])
        qa = split_heads(proj[..., 0:W_SB], N_HEADS_SB)
        ka = split_heads(proj[..., W_SB:2 * W_SB], N_HEADS_SB)
        va = split_heads(proj[..., 2 * W_SB:o0], N_HEADS_SB)
        qf = split_heads(proj[..., o0:o0 + W_FOX], N_HEADS_FOX)
        kf = split_heads(proj[..., o0 + W_FOX:o0 + 2 * W_FOX], N_HEADS_FOX)
        vf = split_heads(proj[..., o0 + 2 * W_FOX:o1], N_HEADS_FOX)
        f_logit = proj[..., o1:].astype(jnp.float32) + b_f[l].astype(jnp.float32)
        log_f = jnp.transpose(jax.nn.log_sigmoid(f_logit), (0, 2, 1))

        o_sb = stick_breaking_attention(qa, ka, va)
        o_fox = forgetting_attention(qf, kf, vf, log_f)
        mixed = jnp.concatenate([head_rmsnorm(o_sb, g_out_sb[l]),
                                 head_rmsnorm(o_fox, g_out_fox[l])], axis=-1)
        x = x + jnp.einsum('bse,ed->bsd', mixed, w_out[l])

        h2 = rmsnorm(x, g_mlp[l])
        u = jax.nn.relu(jnp.einsum('bsd,df->bsf', h2, w_up[l]))
        x = x + jnp.einsum('bsf,fd->bsd', u * u, w_down[l])
    return rmsnorm(x, g_final)
```

```python
import functools

import jax
import jax.numpy as jnp
from jax import lax
from jax.experimental import pallas as pl
from jax.experimental.pallas import tpu as pltpu

F32 = jnp.float32
BF16 = jnp.bfloat16

HEAD_DIM = 128
N_HEADS = 8
GROUP_W = N_HEADS * HEAD_DIM
EPS = 1e-6
SCALE = HEAD_DIM ** -0.5
LANES = 128

VMEM_LIMIT = 56 * 1024 * 1024

TM_IN = 512
TN_IN = 1024
TQ = 256
TK = 256
CS_BLK = 256
TM_OUT = 512
TM_MLP = 512
TF_MLP = 1024


def _params(sem):
    return pltpu.CompilerParams(dimension_semantics=sem, vmem_limit_bytes=VMEM_LIMIT)


def _dot(a, b):
    return jnp.dot(a, b, preferred_element_type=F32)


def _dot_nt(a, b):
    return lax.dot_general(a, b, (((1,), (1,)), ((), ())), preferred_element_type=F32)


def _log_sigmoid(x):
    return jnp.minimum(x, 0.0) - jnp.log(1.0 + jnp.exp(-jnp.abs(x)))


def _in_proj_kernel(x_ref, g_ref, w_ref, wf_ref, o_ref, f_ref, h_ref):
    j = pl.program_id(1)

    @pl.when(j == 0)
    def _():
        x = x_ref[...]
        var = jnp.mean(x * x, axis=-1, keepdims=True)
        h = (x * lax.rsqrt(var + EPS) * g_ref[...]).astype(BF16)
        h_ref[...] = h
        f_ref[...] = _dot(h, wf_ref[...])

    acc = _dot(h_ref[...], w_ref[...])
    scale = jnp.where((j == 0) | (j == 3), SCALE, 1.0).astype(F32)
    o_ref[...] = (acc * scale).astype(BF16)


def _in_proj(x2, g, w_qkv, w_f):
    m, d = x2.shape
    n = w_qkv.shape[1]
    return pl.pallas_call(
        _in_proj_kernel,
        grid=(m // TM_IN, n // TN_IN),
        in_specs=[
            pl.BlockSpec((TM_IN, d), lambda i, j: (i, 0)),
            pl.BlockSpec((1, d), lambda i, j: (0, 0)),
            pl.BlockSpec((d, TN_IN), lambda i, j: (0, j)),
            pl.BlockSpec((d, LANES), lambda i, j: (0, 0)),
        ],
        out_specs=[
            pl.BlockSpec((TM_IN, TN_IN), lambda i, j: (i, j)),
            pl.BlockSpec((TM_IN, LANES), lambda i, j: (i, 0)),
        ],
        out_shape=[
            jax.ShapeDtypeStruct((m, n), BF16),
            jax.ShapeDtypeStruct((m, LANES), F32),
        ],
        scratch_shapes=[pltpu.VMEM((TM_IN, d), BF16)],
        compiler_params=_params(("arbitrary", "arbitrary")),
        name="in_proj",
    )(x2, g, w_qkv, w_f)


def _forget_cs_kernel(f_ref, b_ref, tri_ref, c_ref):
    n_blk = f_ref.shape[0] // CS_BLK
    tri = tri_ref[...]

    def body(i, carry):
        r0 = pl.multiple_of(i * CS_BLK, CS_BLK)
        lf = _log_sigmoid(f_ref[pl.ds(r0, CS_BLK), :] + b_ref[...])
        p1 = lf.astype(BF16)
        r1 = lf - p1.astype(F32)
        p2 = r1.astype(BF16)
        p3 = (r1 - p2.astype(F32)).astype(BF16)
        c = _dot(tri, p1) + _dot(tri, p2) + _dot(tri, p3) + carry
        c_ref[pl.ds(r0, CS_BLK), :] = c
        return c[CS_BLK - 1:CS_BLK, :]

    lax.fori_loop(0, n_blk, body, jnp.zeros((1, LANES), F32))


def _forget_cs(f_logit, b_pad, tri_incl):
    b, s, _ = f_logit.shape
    return pl.pallas_call(
        _forget_cs_kernel,
        grid=(b,),
        in_specs=[
            pl.BlockSpec((None, s, LANES), lambda i: (i, 0, 0)),
            pl.BlockSpec((1, LANES), lambda i: (0, 0)),
            pl.BlockSpec((CS_BLK, CS_BLK), lambda i: (0, 0)),
        ],
        out_specs=pl.BlockSpec((None, s, LANES), lambda i: (i, 0, 0)),
        out_shape=jax.ShapeDtypeStruct((b, s, LANES), F32),
        compiler_params=_params(("arbitrary",)),
        name="forget_cs",
    )(f_logit, b_pad, tri_incl)


def _head_rmsnorm(o, g):
    return o * lax.rsqrt(jnp.mean(o * o, axis=-1, keepdims=True) + EPS) * g


def _kv_tile(k_ref, v_ref, kb):
    ks = pl.multiple_of(kb * TK, TK)
    return k_ref[pl.ds(ks, TK), :], v_ref[pl.ds(ks, TK), :]


def _sb_kernel(q_ref, k_ref, v_ref, g_ref, tri2_ref, o_ref):
    qi = pl.program_id(2)
    q = q_ref[...]
    tri2 = tri2_ref[...]

    def tile(kb, carry, acc, masked):
        k, v = _kv_tile(k_ref, v_ref, kb)
        z = _dot_nt(q, k)
        sp = jnp.log(1.0 + jnp.exp(-jnp.abs(z)))
        lk = -(jnp.maximum(z, 0.0) + sp)
        lb = lk + z
        if masked:
            row = lax.broadcasted_iota(jnp.int32, (TQ, TK), 0)
            col = lax.broadcasted_iota(jnp.int32, (TQ, TK), 1)
            causal = col < row
            lk = jnp.where(causal, lk, 0.0)
        hi = lk.astype(BF16)
        lo = (lk - hi.astype(F32)).astype(BF16)
        rest = _dot(jnp.concatenate([hi, lo], axis=1), tri2) + carry
        a = jnp.exp(lb + rest)
        if masked:
            a = jnp.where(causal, a, 0.0)
        acc = acc + _dot(a.astype(BF16), v)
        carry = carry + jnp.sum(lk, axis=-1, keepdims=True)
        return carry, acc

    carry, acc = tile(qi, jnp.zeros((TQ, 1), F32), jnp.zeros((TQ, HEAD_DIM), F32), True)

    def body(i, state):
        return tile(qi - 1 - i, state[0], state[1], False)

    carry, acc = lax.fori_loop(0, qi, body, (carry, acc))
    o_ref[...] = _head_rmsnorm(acc, g_ref[...]).astype(o_ref.dtype)


def _sb_attn(proj, g_out, tri2):
    b, s, _ = proj.shape
    kv_spec = lambda off: pl.BlockSpec((None, s, HEAD_DIM), lambda bi, h, qi: (bi, 0, off + h))
    return pl.pallas_call(
        _sb_kernel,
        grid=(b, N_HEADS, s // TQ),
        in_specs=[
            pl.BlockSpec((None, TQ, HEAD_DIM), lambda bi, h, qi: (bi, qi, h)),
            kv_spec(N_HEADS),
            kv_spec(2 * N_HEADS),
            pl.BlockSpec((1, HEAD_DIM), lambda bi, h, qi: (0, h)),
            pl.BlockSpec((2 * TK, TK), lambda bi, h, qi: (0, 0)),
        ],
        out_specs=pl.BlockSpec((None, TQ, HEAD_DIM), lambda bi, h, qi: (bi, qi, h)),
        out_shape=jax.ShapeDtypeStruct((b, s, GROUP_W), BF16),
        compiler_params=_params(("arbitrary", "arbitrary", "arbitrary")),
        name="sb_attn",
    )(proj, proj, proj, g_out, tri2)


def _fox_kernel(q_ref, k_ref, v_ref, ccol_ref, crow_ref, g_ref, o_ref):
    h = pl.program_id(1)
    qi = pl.program_id(2)
    q = q_ref[...]
    lane = lax.broadcasted_iota(jnp.int32, (TQ, LANES), 1)
    c_t = jnp.sum(jnp.where(lane == h, ccol_ref[...], 0.0), axis=-1, keepdims=True)

    def logits(kb, masked):
        k, v = _kv_tile(k_ref, v_ref, kb)
        sc = _dot_nt(q, k) + (c_t - crow_ref[kb])
        if masked:
            row = lax.broadcasted_iota(jnp.int32, (TQ, TK), 0)
            col = lax.broadcasted_iota(jnp.int32, (TQ, TK), 1)
            sc = jnp.where(col <= row, sc, -jnp.inf)
        return sc, v

    sc, v = logits(qi, True)
    m = jnp.max(sc, axis=-1, keepdims=True)
    p = jnp.exp(sc - m)
    l = jnp.sum(p, axis=-1, keepdims=True)
    acc = _dot(p.astype(BF16), v)

    def body(i, state):
        m, l, acc = state
        sc, v = logits(qi - 1 - i, False)
        m_new = jnp.maximum(m, jnp.max(sc, axis=-1, keepdims=True))
        alpha = jnp.exp(m - m_new)
        p = jnp.exp(sc - m_new)
        l = alpha * l + jnp.sum(p, axis=-1, keepdims=True)
        acc = alpha * acc + _dot(p.astype(BF16), v)
        return m_new, l, acc

    m, l, acc = lax.fori_loop(0, qi, body, (m, l, acc))
    o_ref[...] = _head_rmsnorm(acc / l, g_ref[...]).astype(o_ref.dtype)


def _fox_attn(proj, c_col, c_row, g_out):
    b, s, _ = proj.shape
    base = 3 * N_HEADS
    kv_spec = lambda off: pl.BlockSpec((None, s, HEAD_DIM), lambda bi, h, qi: (bi, 0, off + h))
    return pl.pallas_call(
        _fox_kernel,
        grid=(b, N_HEADS, s // TQ),
        in_specs=[
            pl.BlockSpec((None, TQ, HEAD_DIM), lambda bi, h, qi: (bi, qi, base + h)),
            kv_spec(base + N_HEADS),
            kv_spec(base + 2 * N_HEADS),
            pl.BlockSpec((None, TQ, LANES), lambda bi, h, qi: (bi, qi, 0)),
            pl.BlockSpec((None, None, s // TK, 1, TK), lambda bi, h, qi: (bi, h, 0, 0, 0)),
            pl.BlockSpec((1, HEAD_DIM), lambda bi, h, qi: (0, h)),
        ],
        out_specs=pl.BlockSpec((None, TQ, HEAD_DIM), lambda bi, h, qi: (bi, qi, h)),
        out_shape=jax.ShapeDtypeStruct((b, s, GROUP_W), BF16),
        compiler_params=_params(("arbitrary", "arbitrary", "arbitrary")),
        name="fox_attn",
    )(proj, proj, proj, c_col, c_row, g_out)


def _out_proj_kernel(ms_ref, mf_ref, w_ref, x_ref, g_ref, x1_ref, h2_ref):
    acc = _dot(ms_ref[...], w_ref[:GROUP_W, :]) + _dot(mf_ref[...], w_ref[GROUP_W:, :])
    x1 = x_ref[...] + acc
    x1_ref[...] = x1
    var = jnp.mean(x1 * x1, axis=-1, keepdims=True)
    h2_ref[...] = (x1 * lax.rsqrt(var + EPS) * g_ref[...]).astype(BF16)


def _out_proj(mixed_sb, mixed_fox, w_out, x2, g_mlp):
    m, d = x2.shape
    return pl.pallas_call(
        _out_proj_kernel,
        grid=(m // TM_OUT,),
        in_specs=[
            pl.BlockSpec((TM_OUT, GROUP_W), lambda i: (i, 0)),
            pl.BlockSpec((TM_OUT, GROUP_W), lambda i: (i, 0)),
            pl.BlockSpec((2 * GROUP_W, d), lambda i: (0, 0)),
            pl.BlockSpec((TM_OUT, d), lambda i: (i, 0)),
            pl.BlockSpec((1, d), lambda i: (0, 0)),
        ],
        out_specs=[
            pl.BlockSpec((TM_OUT, d), lambda i: (i, 0)),
            pl.BlockSpec((TM_OUT, d), lambda i: (i, 0)),
        ],
        out_shape=[
            jax.ShapeDtypeStruct((m, d), F32),
            jax.ShapeDtypeStruct((m, d), BF16),
        ],
        compiler_params=_params(("arbitrary",)),
        name="out_proj",
    )(mixed_sb, mixed_fox, w_out, x2, g_mlp)


def _mlp_kernel(h_ref, x1_ref, wu_ref, wd_ref, g_ref, o_ref):
    f = pl.program_id(1)
    u = jnp.maximum(_dot(h_ref[...], wu_ref[...]), 0.0)
    part = _dot((u * u).astype(BF16), wd_ref[...])

    @pl.when(f == 0)
    def _():
        o_ref[...] = x1_ref[...] + part

    @pl.when(f > 0)
    def _():
        o_ref[...] += part

    @pl.when(f == pl.num_programs(1) - 1)
    def _():
        x2 = o_ref[...]
        var = jnp.mean(x2 * x2, axis=-1, keepdims=True)
        o_ref[...] = x2 * lax.rsqrt(var + EPS) * g_ref[...]


def _mlp(h2, x1, w_up, w_down, g_final):
    m, d = x1.shape
    dff = w_up.shape[1]
    return pl.pallas_call(
        _mlp_kernel,
        grid=(m // TM_MLP, dff // TF_MLP),
        in_specs=[
            pl.BlockSpec((TM_MLP, d), lambda i, f: (i, 0)),
            pl.BlockSpec((TM_MLP, d), lambda i, f: (i, 0)),
            pl.BlockSpec((d, TF_MLP), lambda i, f: (0, f)),
            pl.BlockSpec((TF_MLP, d), lambda i, f: (f, 0)),
            pl.BlockSpec((1, d), lambda i, f: (0, 0)),
        ],
        out_specs=pl.BlockSpec((TM_MLP, d), lambda i, f: (i, 0)),
        out_shape=jax.ShapeDtypeStruct((m, d), F32),
        compiler_params=_params(("arbitrary", "arbitrary")),
        name="mlp",
    )(h2, x1, w_up, w_down, g_final)


def kernel(x, g_attn, w_in, b_f, g_out_sb, g_out_fox, w_out, g_mlp, w_up, w_down, g_final):
    b, s, d = x.shape
    n_qkv = 6 * GROUP_W
    assert s % TQ == 0 and TQ == TK and (b * s) % TM_IN == 0
    x2 = x.reshape(b * s, d)

    idx = jnp.arange(TK)
    tri_excl = (idx[:, None] > idx[None, :]).astype(BF16)
    tri2 = jnp.concatenate([tri_excl, tri_excl], axis=0)
    cidx = jnp.arange(CS_BLK)
    tri_incl = (cidx[None, :] <= cidx[:, None]).astype(BF16)

    for l in range(g_attn.shape[0]):
        w_qkv = w_in[l, :, :n_qkv].astype(BF16)
        w_f = jnp.pad(w_in[l, :, n_qkv:], ((0, 0), (0, LANES - N_HEADS))).astype(BF16)
        b_pad = jnp.pad(b_f[l], (0, LANES - N_HEADS)).reshape(1, LANES)

        proj, f_logit = _in_proj(x2, g_attn[l].reshape(1, d), w_qkv, w_f)
        proj = proj.reshape(b, s, n_qkv)
        c = _forget_cs(f_logit.reshape(b, s, LANES), b_pad, tri_incl)
        c_row = jnp.transpose(c[:, :, :N_HEADS], (0, 2, 1)).reshape(b, N_HEADS, s // TK, 1, TK)

        mixed_sb = _sb_attn(proj, g_out_sb[l].reshape(1, GROUP_W), tri2)
        mixed_fox = _fox_attn(proj, c, c_row, g_out_fox[l].reshape(1, GROUP_W))

        x1, h2 = _out_proj(mixed_sb.reshape(b * s, GROUP_W), mixed_fox.reshape(b * s, GROUP_W),
                           w_out[l].astype(BF16), x2, g_mlp[l].reshape(1, d))
        assert g_attn.shape[0] == 1
        x2 = _mlp(h2, x1, w_up[l].astype(BF16), w_down[l].astype(BF16), g_final.reshape(1, d))
    return x2.reshape(b, s, d)
```

```python
import functools

import jax
import jax.numpy as jnp
from jax import lax
from jax.experimental import pallas as pl
from jax.experimental.pallas import tpu as pltpu

F32 = jnp.float32
BF16 = jnp.bfloat16

HEAD_DIM = 128
N_HEADS = 8
GROUP_W = N_HEADS * HEAD_DIM
EPS = 1e-6
SCALE = HEAD_DIM ** -0.5
LOG2E = 1.4426950408889634
SB_DEAD_LOG2 = -152.0
FOX_DEAD_LOG2 = -152.0
NORM_SLACK = 1.01
KN_BLK = 512
LANES = 128

VMEM_LIMIT = 56 * 1024 * 1024

TM_IN = 512
TN_IN = 1024
TQ = 256
TK = 256
TQ_F = 256
TK_F = 1024
CS_BLK = 256
TM_OUT = 512
TM_MLP = 512
TF_MLP = 1024


def _params(sem):
    return pltpu.CompilerParams(dimension_semantics=sem, vmem_limit_bytes=VMEM_LIMIT)


def _dot(a, b):
    return jnp.dot(a, b, preferred_element_type=F32)


def _dot_nt(a, b):
    return lax.dot_general(a, b, (((1,), (1,)), ((), ())), preferred_element_type=F32)


def _log_sigmoid(x):
    return jnp.minimum(x, 0.0) - jnp.log(1.0 + jnp.exp(-jnp.abs(x)))


def _in_proj_kernel(x_ref, g_ref, w_ref, wf_ref, o_ref, f_ref, h_ref):
    j = pl.program_id(1)

    @pl.when(j == 0)
    def _():
        x = x_ref[...]
        var = jnp.mean(x * x, axis=-1, keepdims=True)
        h = (x * lax.rsqrt(var + EPS) * g_ref[...]).astype(BF16)
        h_ref[...] = h
        f_ref[...] = _dot(h, wf_ref[...])

    acc = _dot(h_ref[...], w_ref[...])
    scale = jnp.where((j == 0) | (j == 3), SCALE * LOG2E, 1.0).astype(F32)
    o_ref[...] = (acc * scale).astype(BF16)


def _in_proj(x2, g, w_qkv, w_f):
    m, d = x2.shape
    n = w_qkv.shape[1]
    return pl.pallas_call(
        _in_proj_kernel,
        grid=(m // TM_IN, n // TN_IN),
        in_specs=[
            pl.BlockSpec((TM_IN, d), lambda i, j: (i, 0)),
            pl.BlockSpec((1, d), lambda i, j: (0, 0)),
            pl.BlockSpec((d, TN_IN), lambda i, j: (0, j)),
            pl.BlockSpec((d, LANES), lambda i, j: (0, 0)),
        ],
        out_specs=[
            pl.BlockSpec((TM_IN, TN_IN), lambda i, j: (i, j)),
            pl.BlockSpec((TM_IN, LANES), lambda i, j: (i, 0)),
        ],
        out_shape=[
            jax.ShapeDtypeStruct((m, n), BF16),
            jax.ShapeDtypeStruct((m, LANES), F32),
        ],
        scratch_shapes=[pltpu.VMEM((TM_IN, d), BF16)],
        compiler_params=_params(("arbitrary", "arbitrary")),
        name="in_proj",
    )(x2, g, w_qkv, w_f)


def _forget_cs_kernel(f_ref, b_ref, tri_ref, c_ref):
    n_blk = f_ref.shape[0] // CS_BLK
    tri = tri_ref[...]

    def body(i, carry):
        r0 = pl.multiple_of(i * CS_BLK, CS_BLK)
        lf = _log_sigmoid(f_ref[pl.ds(r0, CS_BLK), :] + b_ref[...])
        p1 = lf.astype(BF16)
        r1 = lf - p1.astype(F32)
        p2 = r1.astype(BF16)
        p3 = (r1 - p2.astype(F32)).astype(BF16)
        c = _dot(tri, p1) + _dot(tri, p2) + _dot(tri, p3) + carry
        c_ref[pl.ds(r0, CS_BLK), :] = c * LOG2E
        return c[CS_BLK - 1:CS_BLK, :]

    lax.fori_loop(0, n_blk, body, jnp.zeros((1, LANES), F32))


def _forget_cs(f_logit, b_pad, tri_incl):
    b, s, _ = f_logit.shape
    return pl.pallas_call(
        _forget_cs_kernel,
        grid=(b,),
        in_specs=[
            pl.BlockSpec((None, s, LANES), lambda i: (i, 0, 0)),
            pl.BlockSpec((1, LANES), lambda i: (0, 0)),
            pl.BlockSpec((CS_BLK, CS_BLK), lambda i: (0, 0)),
        ],
        out_specs=pl.BlockSpec((None, s, LANES), lambda i: (i, 0, 0)),
        out_shape=jax.ShapeDtypeStruct((b, s, LANES), F32),
        compiler_params=_params(("arbitrary",)),
        name="forget_cs",
    )(f_logit, b_pad, tri_incl)


def _head_rmsnorm(o, g):
    return o * lax.rsqrt(jnp.mean(o * o, axis=-1, keepdims=True) + EPS) * g


def _kv_tile(k_ref, v_ref, kb):
    ks = pl.multiple_of(kb * TK, TK)
    return k_ref[pl.ds(ks, TK), :], v_ref[pl.ds(ks, TK), :]


def _sb_kernel(q_ref, k_ref, v_ref, g_ref, tri2_ref, o_ref):
    qi = pl.program_id(2)
    q = q_ref[...]
    tri2 = tri2_ref[...]

    def tile(kb, carry, acc, masked, v_gate=None):
        k, v = _kv_tile(k_ref, v_ref, kb)
        if v_gate is not None:
            v = v * v_gate
        z = _dot_nt(q, k)
        sp = jnp.log2(1.0 + jnp.exp2(-jnp.abs(z)))
        lk = jnp.minimum(-z, 0.0) - sp
        lb = lk + z
        if masked:
            row = lax.broadcasted_iota(jnp.int32, (TQ, TK), 0)
            col = lax.broadcasted_iota(jnp.int32, (TQ, TK), 1)
            causal = col < row
            lk = jnp.where(causal, lk, 0.0)
        hi = lk.astype(BF16)
        lo = (lk - hi.astype(F32)).astype(BF16)
        rest = _dot(jnp.concatenate([hi, lo], axis=1), tri2) + carry
        a = jnp.exp2(lb + rest)
        if masked:
            a = jnp.where(causal, a, 0.0)
        acc = acc + _dot(a.astype(BF16), v)
        carry = carry + jnp.sum(lk, axis=-1, keepdims=True)
        return carry, acc

    carry, acc = tile(qi, jnp.zeros((TQ, 1), F32), jnp.zeros((TQ, HEAD_DIM), F32), True)
    has_prev = qi > 0
    gate = jnp.where(has_prev, 1.0, 0.0).astype(BF16)
    carry, acc = tile(jnp.maximum(qi - 1, 0), carry, acc, False, v_gate=gate)

    def cond(state):
        i, carry, _ = state
        return jnp.logical_and(i >= 0, jnp.max(carry) > SB_DEAD_LOG2)

    def body(state):
        i, carry, acc = state
        carry, acc = tile(i, carry, acc, False)
        return i - 1, carry, acc

    _, _, acc = lax.while_loop(cond, body, (qi - 2, carry, acc))
    o_ref[...] = _head_rmsnorm(acc, g_ref[...]).astype(o_ref.dtype)


def _sb_attn(proj, g_out, tri2):
    b, s, _ = proj.shape
    kv_spec = lambda off: pl.BlockSpec((None, s, HEAD_DIM), lambda bi, h, qi: (bi, 0, off + h))
    return pl.pallas_call(
        _sb_kernel,
        grid=(b, N_HEADS, s // TQ),
        in_specs=[
            pl.BlockSpec((None, TQ, HEAD_DIM), lambda bi, h, qi: (bi, qi, h)),
            kv_spec(N_HEADS),
            kv_spec(2 * N_HEADS),
            pl.BlockSpec((1, HEAD_DIM), lambda bi, h, qi: (0, h)),
            pl.BlockSpec((2 * TK, TK), lambda bi, h, qi: (0, 0)),
        ],
        out_specs=pl.BlockSpec((None, TQ, HEAD_DIM), lambda bi, h, qi: (bi, qi, h)),
        out_shape=jax.ShapeDtypeStruct((b, s, GROUP_W), BF16),
        compiler_params=_params(("arbitrary", "arbitrary", "arbitrary")),
        name="sb_attn",
    )(proj, proj, proj, g_out, tri2)


def _fox_kernel(q_ref, k_ref, v_ref, ccol_ref, crow_ref, clast_ref, g_ref, o_ref, kn_ref):
    h = pl.program_id(1)
    qi = pl.program_id(2)
    q = q_ref[...]
    lane = lax.broadcasted_iota(jnp.int32, (TQ_F, LANES), 1)
    c_t = jnp.sum(jnp.where(lane == h, ccol_ref[...], 0.0), axis=-1, keepdims=True)
    n_diag = max(1, TQ_F // TK_F)
    n_full = (qi * TQ_F) // TK_F

    @pl.when(qi == 0)
    def _():
        def chunk(i, best):
            r0 = pl.multiple_of(i * KN_BLK, KN_BLK)
            kf = k_ref[pl.ds(r0, KN_BLK), :].astype(F32)
            n2 = jnp.sum(kf * kf, axis=-1, keepdims=True)
            return jnp.maximum(best, jnp.max(n2, axis=0, keepdims=True))

        best = lax.fori_loop(0, k_ref.shape[0] // KN_BLK, chunk, jnp.zeros((1, 1), F32))
        kn_ref[...] = jnp.broadcast_to(jnp.sqrt(best), kn_ref.shape)

    def logits(kb, masked):
        ks = pl.multiple_of(kb * TK_F, TK_F)
        k = k_ref[pl.ds(ks, TK_F), :]
        v = v_ref[pl.ds(ks, TK_F), :]
        sc = _dot_nt(q, k) + (c_t - crow_ref[kb])
        if masked:
            row = qi * TQ_F + lax.broadcasted_iota(jnp.int32, (TQ_F, TK_F), 0)
            col = kb * TK_F + lax.broadcasted_iota(jnp.int32, (TQ_F, TK_F), 1)
            sc = jnp.where(col <= row, sc, -jnp.inf)
        return sc, v

    def update(state, sc, v):
        m, l, acc = state
        m_new = jnp.maximum(m, jnp.max(sc, axis=-1, keepdims=True))
        alpha = jnp.exp2(m - m_new)
        p = jnp.exp2(sc - m_new)
        l = alpha * l + jnp.sum(p, axis=-1, keepdims=True)
        acc = alpha * acc + _dot(p.astype(BF16), v)
        return m_new, l, acc

    sc, v = logits(n_full, True)
    m = jnp.max(sc, axis=-1, keepdims=True)
    p = jnp.exp2(sc - m)
    state = (m, jnp.sum(p, axis=-1, keepdims=True), _dot(p.astype(BF16), v))
    for d in range(1, n_diag):
        state = update(state, *logits(n_full + d, True))

    qf = q.astype(F32)
    qn = jnp.sqrt(jnp.sum(qf * qf, axis=-1, keepdims=True))
    ub = qn * kn_ref[:, :1] * NORM_SLACK + c_t
    gap = jnp.max(ub - state[0], axis=0, keepdims=True)
    tile_idx = lax.broadcasted_iota(jnp.int32, (1, LANES), 1)
    live = jnp.logical_and(tile_idx < n_full, gap - clast_ref[...] > FOX_DEAD_LOG2)
    n_live = jnp.sum(live.astype(jnp.int32))

    def body(i, state):
        return update(state, *logits(n_full - 1 - i, False))

    m, l, acc = lax.fori_loop(0, n_live, body, state)
    o_ref[...] = _head_rmsnorm(acc / l, g_ref[...]).astype(o_ref.dtype)


def _fox_attn(proj, c, g_out):
    b, s, _ = proj.shape
    base = 3 * N_HEADS
    n_kt = s // TK_F
    assert n_kt <= LANES
    c_hs = jnp.transpose(c[:, :, :N_HEADS], (0, 2, 1))
    c_row = c_hs.reshape(b, N_HEADS, n_kt, 1, TK_F)
    c_last = jnp.pad(c_row[:, :, :, 0, TK_F - 1], ((0, 0), (0, 0), (0, LANES - n_kt)))
    c_last = c_last.reshape(b, N_HEADS, 1, LANES)
    kv_spec = lambda off: pl.BlockSpec((None, s, HEAD_DIM), lambda bi, h, qi: (bi, 0, off + h))
    return pl.pallas_call(
        _fox_kernel,
        grid=(b, N_HEADS, s // TQ_F),
        in_specs=[
            pl.BlockSpec((None, TQ_F, HEAD_DIM), lambda bi, h, qi: (bi, qi, base + h)),
            kv_spec(base + N_HEADS),
            kv_spec(base + 2 * N_HEADS),
            pl.BlockSpec((None, TQ_F, LANES), lambda bi, h, qi: (bi, qi, 0)),
            pl.BlockSpec((None, None, n_kt, 1, TK_F), lambda bi, h, qi: (bi, h, 0, 0, 0)),
            pl.BlockSpec((None, None, 1, LANES), lambda bi, h, qi: (bi, h, 0, 0)),
            pl.BlockSpec((1, HEAD_DIM), lambda bi, h, qi: (0, h)),
        ],
        out_specs=pl.BlockSpec((None, TQ_F, HEAD_DIM), lambda bi, h, qi: (bi, qi, h)),
        scratch_shapes=[pltpu.VMEM((1, LANES), F32)],
        out_shape=jax.ShapeDtypeStruct((b, s, GROUP_W), BF16),
        compiler_params=_params(("arbitrary", "arbitrary", "arbitrary")),
        name="fox_attn",
    )(proj, proj, proj, c, c_row, c_last, g_out)


def _out_proj_kernel(ms_ref, mf_ref, w_ref, x_ref, g_ref, x1_ref, h2_ref):
    acc = _dot(ms_ref[...], w_ref[:GROUP_W, :]) + _dot(mf_ref[...], w_ref[GROUP_W:, :])
    x1 = x_ref[...] + acc
    x1_ref[...] = x1
    var = jnp.mean(x1 * x1, axis=-1, keepdims=True)
    h2_ref[...] = (x1 * lax.rsqrt(var + EPS) * g_ref[...]).astype(BF16)


def _out_proj(mixed_sb, mixed_fox, w_out, x2, g_mlp):
    m, d = x2.shape
    return pl.pallas_call(
        _out_proj_kernel,
        grid=(m // TM_OUT,),
        in_specs=[
            pl.BlockSpec((TM_OUT, GROUP_W), lambda i: (i, 0)),
            pl.BlockSpec((TM_OUT, GROUP_W), lambda i: (i, 0)),
            pl.BlockSpec((2 * GROUP_W, d), lambda i: (0, 0)),
            pl.BlockSpec((TM_OUT, d), lambda i: (i, 0)),
            pl.BlockSpec((1, d), lambda i: (0, 0)),
        ],
        out_specs=[
            pl.BlockSpec((TM_OUT, d), lambda i: (i, 0)),
            pl.BlockSpec((TM_OUT, d), lambda i: (i, 0)),
        ],
        out_shape=[
            jax.ShapeDtypeStruct((m, d), F32),
            jax.ShapeDtypeStruct((m, d), BF16),
        ],
        compiler_params=_params(("arbitrary",)),
        name="out_proj",
    )(mixed_sb, mixed_fox, w_out, x2, g_mlp)


def _mlp_kernel(h_ref, x1_ref, wu_ref, wd_ref, g_ref, o_ref):
    f = pl.program_id(1)
    u = jnp.maximum(_dot(h_ref[...], wu_ref[...]), 0.0)
    part = _dot((u * u).astype(BF16), wd_ref[...])

    @pl.when(f == 0)
    def _():
        o_ref[...] = x1_ref[...] + part

    @pl.when(f > 0)
    def _():
        o_ref[...] += part

    @pl.when(f == pl.num_programs(1) - 1)
    def _():
        x2 = o_ref[...]
        var = jnp.mean(x2 * x2, axis=-1, keepdims=True)
        o_ref[...] = x2 * lax.rsqrt(var + EPS) * g_ref[...]


def _mlp(h2, x1, w_up, w_down, g_final):
    m, d = x1.shape
    dff = w_up.shape[1]
    return pl.pallas_call(
        _mlp_kernel,
        grid=(m // TM_MLP, dff // TF_MLP),
        in_specs=[
            pl.BlockSpec((TM_MLP, d), lambda i, f: (i, 0)),
            pl.BlockSpec((TM_MLP, d), lambda i, f: (i, 0)),
            pl.BlockSpec((d, TF_MLP), lambda i, f: (0, f)),
            pl.BlockSpec((TF_MLP, d), lambda i, f: (f, 0)),
            pl.BlockSpec((1, d), lambda i, f: (0, 0)),
        ],
        out_specs=pl.BlockSpec((TM_MLP, d), lambda i, f: (i, 0)),
        out_shape=jax.ShapeDtypeStruct((m, d), F32),
        compiler_params=_params(("arbitrary", "arbitrary")),
        name="mlp",
    )(h2, x1, w_up, w_down, g_final)


def kernel(x, g_attn, w_in, b_f, g_out_sb, g_out_fox, w_out, g_mlp, w_up, w_down, g_final):
    b, s, d = x.shape
    n_qkv = 6 * GROUP_W
    assert s % TQ == 0 and TQ == TK and (b * s) % TM_IN == 0
    x2 = x.reshape(b * s, d)

    idx = jnp.arange(TK)
    tri_excl = (idx[:, None] > idx[None, :]).astype(BF16)
    tri2 = jnp.concatenate([tri_excl, tri_excl], axis=0)
    cidx = jnp.arange(CS_BLK)
    tri_incl = (cidx[None, :] <= cidx[:, None]).astype(BF16)

    for l in range(g_attn.shape[0]):
        w_qkv = w_in[l, :, :n_qkv].astype(BF16)
        w_f = jnp.pad(w_in[l, :, n_qkv:], ((0, 0), (0, LANES - N_HEADS))).astype(BF16)
        b_pad = jnp.pad(b_f[l], (0, LANES - N_HEADS)).reshape(1, LANES)

        proj, f_logit = _in_proj(x2, g_attn[l].reshape(1, d), w_qkv, w_f)
        proj = proj.reshape(b, s, n_qkv)
        c = _forget_cs(f_logit.reshape(b, s, LANES), b_pad, tri_incl)

        mixed_sb = _sb_attn(proj, g_out_sb[l].reshape(1, GROUP_W), tri2)
        mixed_fox = _fox_attn(proj, c, g_out_fox[l].reshape(1, GROUP_W))

        x1, h2 = _out_proj(mixed_sb.reshape(b * s, GROUP_W), mixed_fox.reshape(b * s, GROUP_W),
                           w_out[l].astype(BF16), x2, g_mlp[l].reshape(1, d))
        assert g_attn.shape[0] == 1
        x2 = _mlp(h2, x1, w_up[l].astype(BF16), w_down[l].astype(BF16), g_final.reshape(1, d))
    return x2.reshape(b, s, d)
```

```python
import functools

import jax
import jax.numpy as jnp
from jax import lax
from jax.experimental import pallas as pl
from jax.experimental.pallas import tpu as pltpu

F32 = jnp.float32
BF16 = jnp.bfloat16

HEAD_DIM = 128
N_HEADS = 8
GROUP_W = N_HEADS * HEAD_DIM
EPS = 1e-6
SCALE = HEAD_DIM ** -0.5
LOG2E = 1.4426950408889634
SB_DEAD_LOG2 = -152.0
FOX_DEAD_LOG2 = -152.0
NORM_SLACK = 1.01
M_INIT = -1e30
KN_BLK = 512
LANES = 128

VMEM_LIMIT = 56 * 1024 * 1024

TM_IN = 512
TN_IN = 1024
TQ = 256
TK = 256
TQ_F = 256
TK_F = 512
CS_BLK = 256
TM_OUT = 512
TM_MLP = 512
TF_MLP = 1024


def _params(sem):
    return pltpu.CompilerParams(dimension_semantics=sem, vmem_limit_bytes=VMEM_LIMIT)


def _dot(a, b):
    return jnp.dot(a, b, preferred_element_type=F32)


def _dot_nt(a, b):
    return lax.dot_general(a, b, (((1,), (1,)), ((), ())), preferred_element_type=F32)


def _log_sigmoid(x):
    return jnp.minimum(x, 0.0) - jnp.log(1.0 + jnp.exp(-jnp.abs(x)))


def _in_proj_kernel(x_ref, g_ref, w_ref, wf_ref, o_ref, f_ref, h_ref):
    j = pl.program_id(1)

    @pl.when(j == 0)
    def _():
        x = x_ref[...]
        var = jnp.mean(x * x, axis=-1, keepdims=True)
        h = (x * lax.rsqrt(var + EPS) * g_ref[...]).astype(BF16)
        h_ref[...] = h
        f_ref[...] = _dot(h, wf_ref[...])

    acc = _dot(h_ref[...], w_ref[...])
    scale = jnp.where((j == 0) | (j == 3), SCALE * LOG2E, 1.0).astype(F32)
    o_ref[...] = (acc * scale).astype(BF16)


def _in_proj(x2, g, w_qkv, w_f):
    m, d = x2.shape
    n = w_qkv.shape[1]
    return pl.pallas_call(
        _in_proj_kernel,
        grid=(m // TM_IN, n // TN_IN),
        in_specs=[
            pl.BlockSpec((TM_IN, d), lambda i, j: (i, 0)),
            pl.BlockSpec((1, d), lambda i, j: (0, 0)),
            pl.BlockSpec((d, TN_IN), lambda i, j: (0, j)),
            pl.BlockSpec((d, LANES), lambda i, j: (0, 0)),
        ],
        out_specs=[
            pl.BlockSpec((TM_IN, TN_IN), lambda i, j: (i, j)),
            pl.BlockSpec((TM_IN, LANES), lambda i, j: (i, 0)),
        ],
        out_shape=[
            jax.ShapeDtypeStruct((m, n), BF16),
            jax.ShapeDtypeStruct((m, LANES), F32),
        ],
        scratch_shapes=[pltpu.VMEM((TM_IN, d), BF16)],
        compiler_params=_params(("arbitrary", "arbitrary")),
        name="in_proj",
    )(x2, g, w_qkv, w_f)


def _forget_cs_kernel(f_ref, b_ref, tri_ref, c_ref):
    n_blk = f_ref.shape[0] // CS_BLK
    tri = tri_ref[...]

    def body(i, carry):
        r0 = pl.multiple_of(i * CS_BLK, CS_BLK)
        lf = _log_sigmoid(f_ref[pl.ds(r0, CS_BLK), :] + b_ref[...])
        p1 = lf.astype(BF16)
        r1 = lf - p1.astype(F32)
        p2 = r1.astype(BF16)
        p3 = (r1 - p2.astype(F32)).astype(BF16)
        c = _dot(tri, p1) + _dot(tri, p2) + _dot(tri, p3) + carry
        c_ref[pl.ds(r0, CS_BLK), :] = c * LOG2E
        return c[CS_BLK - 1:CS_BLK, :]

    lax.fori_loop(0, n_blk, body, jnp.zeros((1, LANES), F32))


def _forget_cs(f_logit, b_pad, tri_incl):
    b, s, _ = f_logit.shape
    return pl.pallas_call(
        _forget_cs_kernel,
        grid=(b,),
        in_specs=[
            pl.BlockSpec((None, s, LANES), lambda i: (i, 0, 0)),
            pl.BlockSpec((1, LANES), lambda i: (0, 0)),
            pl.BlockSpec((CS_BLK, CS_BLK), lambda i: (0, 0)),
        ],
        out_specs=pl.BlockSpec((None, s, LANES), lambda i: (i, 0, 0)),
        out_shape=jax.ShapeDtypeStruct((b, s, LANES), F32),
        compiler_params=_params(("arbitrary",)),
        name="forget_cs",
    )(f_logit, b_pad, tri_incl)


def _head_rmsnorm(o, g):
    return o * lax.rsqrt(jnp.mean(o * o, axis=-1, keepdims=True) + EPS) * g


def _kv_tile(k_ref, v_ref, kb):
    ks = pl.multiple_of(kb * TK, TK)
    return k_ref[pl.ds(ks, TK), :], v_ref[pl.ds(ks, TK), :]


def _sb_kernel(q_ref, k_ref, v_ref, g_ref, tri2_ref, o_ref):
    qi = pl.program_id(2)
    q = q_ref[...]
    tri2 = tri2_ref[...]

    def tile(kb, carry, acc, masked, v_gate=None):
        k, v = _kv_tile(k_ref, v_ref, kb)
        if v_gate is not None:
            v = v * v_gate
        z = _dot_nt(q, k)
        sp = jnp.log2(1.0 + jnp.exp2(-jnp.abs(z)))
        lk = jnp.minimum(-z, 0.0) - sp
        lb = lk + z
        if masked:
            row = lax.broadcasted_iota(jnp.int32, (TQ, TK), 0)
            col = lax.broadcasted_iota(jnp.int32, (TQ, TK), 1)
            causal = col < row
            lk = jnp.where(causal, lk, 0.0)
        hi = lk.astype(BF16)
        lo = (lk - hi.astype(F32)).astype(BF16)
        rest = _dot(jnp.concatenate([hi, lo], axis=1), tri2) + carry
        a = jnp.exp2(lb + rest)
        if masked:
            a = jnp.where(causal, a, 0.0)
        acc = acc + _dot(a.astype(BF16), v)
        carry = carry + jnp.sum(lk, axis=-1, keepdims=True)
        return carry, acc

    carry, acc = tile(qi, jnp.zeros((TQ, 1), F32), jnp.zeros((TQ, HEAD_DIM), F32), True)
    has_prev = qi > 0
    gate = jnp.where(has_prev, 1.0, 0.0).astype(BF16)
    carry, acc = tile(jnp.maximum(qi - 1, 0), carry, acc, False, v_gate=gate)

    def cond(state):
        i, carry, _ = state
        return jnp.logical_and(i >= 0, jnp.max(carry) > SB_DEAD_LOG2)

    def body(state):
        i, carry, acc = state
        carry, acc = tile(i, carry, acc, False)
        return i - 1, carry, acc

    _, _, acc = lax.while_loop(cond, body, (qi - 2, carry, acc))
    o_ref[...] = _head_rmsnorm(acc, g_ref[...]).astype(o_ref.dtype)


def _sb_attn(proj, g_out, tri2):
    b, s, _ = proj.shape
    kv_spec = lambda off: pl.BlockSpec((None, s, HEAD_DIM), lambda bi, h, qi: (bi, 0, off + h))
    return pl.pallas_call(
        _sb_kernel,
        grid=(b, N_HEADS, s // TQ),
        in_specs=[
            pl.BlockSpec((None, TQ, HEAD_DIM), lambda bi, h, qi: (bi, qi, h)),
            kv_spec(N_HEADS),
            kv_spec(2 * N_HEADS),
            pl.BlockSpec((1, HEAD_DIM), lambda bi, h, qi: (0, h)),
            pl.BlockSpec((2 * TK, TK), lambda bi, h, qi: (0, 0)),
        ],
        out_specs=pl.BlockSpec((None, TQ, HEAD_DIM), lambda bi, h, qi: (bi, qi, h)),
        out_shape=jax.ShapeDtypeStruct((b, s, GROUP_W), BF16),
        compiler_params=_params(("arbitrary", "arbitrary", "arbitrary")),
        name="sb_attn",
    )(proj, proj, proj, g_out, tri2)


def _col_reduce(x, pair, full):
    while x.shape[0] > 8:
        half = x.shape[0] // 2
        x = pair(x[:half], x[half:])
    return full(x, axis=0, keepdims=True)


def _split3(x):
    p1 = x.astype(BF16).astype(F32)
    r = x - p1
    p2 = r.astype(BF16).astype(F32)
    p3 = (r - p2).astype(BF16).astype(F32)
    return p1, p2, p3


def _aug_lanes(x, first):
    p1, p2, p3 = _split3(x)
    lane = lax.broadcasted_iota(jnp.int32, (x.shape[0], LANES), 1)
    a = 0 if first else 3
    b = 3 - a
    ones = jnp.where((lane >= b) & (lane < b + 3), 1.0, 0.0)
    ext = jnp.where(lane == a, p1, jnp.where(lane == a + 1, p2, jnp.where(lane == a + 2, p3, ones)))
    return ext.astype(BF16)


def _fox_kernel(q_ref, k_ref, vt_ref, c_ref, cq_ref, clast_ref, g_ref, o_ref,
                kaug_ref, kn_ref, z_ref, p_ref, m_ref, l_ref, a_ref, acc_ref):
    h = pl.program_id(1)
    qi = pl.program_id(2)
    n_full = (qi * TQ_F) // TK_F

    def head_col(c_blk):
        lane = lax.broadcasted_iota(jnp.int32, c_blk.shape, 1)
        return jnp.sum(jnp.where(lane == h, c_blk, 0.0), axis=-1, keepdims=True)

    @pl.when(qi == 0)
    def _():
        def chunk(i, best):
            r0 = pl.multiple_of(i * KN_BLK, KN_BLK)
            k = k_ref[pl.ds(r0, KN_BLK), :]
            kaug_ref[pl.ds(r0, KN_BLK), :HEAD_DIM] = k
            kaug_ref[pl.ds(r0, KN_BLK), HEAD_DIM:] = _aug_lanes(-head_col(c_ref[pl.ds(r0, KN_BLK), :]), True)
            kf = k.astype(F32)
            n2 = jnp.sum(kf * kf, axis=-1, keepdims=True)
            return jnp.maximum(best, jnp.max(n2, axis=0, keepdims=True))

        best = lax.fori_loop(0, k_ref.shape[0] // KN_BLK, chunk, jnp.zeros((1, 1), F32))
        kn_ref[...] = jnp.broadcast_to(jnp.sqrt(best), kn_ref.shape)

    q = q_ref[...]
    q0 = pl.multiple_of(qi * TQ_F, TQ_F)
    q_aug = jnp.concatenate([q, _aug_lanes(head_col(c_ref[pl.ds(q0, TQ_F), :]), False)], axis=1)

    def row_sums(x):
        hi = x.astype(BF16)
        lo = (x - hi.astype(F32)).astype(BF16)
        ones = jnp.ones((8, x.shape[1]), BF16)
        return (_dot_nt(ones, hi) + _dot_nt(ones, lo))[:1]

    qf = q.astype(F32)
    diag = row_sums(q_aug.astype(F32) * kaug_ref[pl.ds(q0, TQ_F), :].astype(F32))
    ub = jnp.sqrt(row_sums(qf * qf)) * kn_ref[:, :1] * NORM_SLACK + cq_ref[...]
    gap = jnp.max(ub - diag, axis=1, keepdims=True)
    tile_idx = lax.broadcasted_iota(jnp.int32, (1, LANES), 1)
    live = jnp.logical_and(tile_idx < n_full, gap - clast_ref[...] > FOX_DEAD_LOG2)
    n_tiles = 1 + jnp.sum(live.astype(jnp.int32))

    last_tile = vt_ref.shape[0] - 1

    def kb_of(i):
        return jnp.clip(n_full - i, 0, last_tile)

    def qk_t(i):
        ks = pl.multiple_of(kb_of(i) * TK_F, TK_F)
        return _dot_nt(kaug_ref[pl.ds(ks, TK_F), :], q_aug)

    st = qk_t(0)
    z_ref[1] = qk_t(1)
    key = n_full * TK_F + lax.broadcasted_iota(jnp.int32, (TK_F, TQ_F), 0)
    qry = qi * TQ_F + lax.broadcasted_iota(jnp.int32, (TK_F, TQ_F), 1)
    st = jnp.where(key <= qry, st, -jnp.inf)
    m0 = _col_reduce(st, jnp.maximum, jnp.max)
    p0 = jnp.exp2(st - m0)
    p_ref[0] = p0.astype(BF16)
    m_ref[...] = m0
    l_ref[...] = _col_reduce(p0, jnp.add, jnp.sum)
    a_ref[...] = jnp.ones_like(m0)
    acc_ref[...] = jnp.zeros_like(acc_ref)

    def step(i, slot, active=None):
        z_ref[1 - slot] = qk_t(i + 1)
        acc_ref[...] = a_ref[...] * acc_ref[...] + _dot(vt_ref[kb_of(i - 1)], p_ref[1 - slot])
        z = z_ref[slot]
        if active is not None:
            z = jnp.where(active, z, -jnp.inf)
        m = m_ref[...]
        m_new = jnp.maximum(m, _col_reduce(z, jnp.maximum, jnp.max))
        alpha = jnp.exp2(m - m_new)
        p = jnp.exp2(z_ref[slot] - m_new) if active is None else jnp.exp2(z - m_new)
        p_ref[slot] = p.astype(BF16)
        l_ref[...] = alpha * l_ref[...] + _col_reduce(p, jnp.add, jnp.sum)
        m_ref[...] = m_new
        a_ref[...] = alpha

    def body(j, _):
        i = 1 + 2 * j
        step(i, 1)
        step(i + 1, 0, active=i + 1 < n_tiles)
        return 0

    n_pairs = n_tiles // 2
    lax.fori_loop(0, n_pairs, body, 0)
    acc = a_ref[...] * acc_ref[...] + _dot(vt_ref[kb_of(2 * n_pairs)], p_ref[0])
    l = l_ref[...]
    o = acc / l
    o = o * lax.rsqrt(jnp.mean(o * o, axis=0, keepdims=True) + EPS) * g_ref[...]
    o_ref[...] = jnp.transpose(o).astype(o_ref.dtype)


def _fox_attn(proj, c, g_out):
    b, s, _ = proj.shape
    base = 3 * N_HEADS
    n_kt = s // TK_F
    assert n_kt <= LANES
    c_hs = jnp.transpose(c[:, :, :N_HEADS], (0, 2, 1))
    c_q = c_hs.reshape(b, N_HEADS, s // TQ_F, 1, TQ_F)
    c_last = jnp.pad(c_hs.reshape(b, N_HEADS, n_kt, TK_F)[..., TK_F - 1], ((0, 0), (0, 0), (0, LANES - n_kt)))
    c_last = c_last.reshape(b, N_HEADS, 1, LANES)
    v_cols = proj[:, :, (base + 2 * N_HEADS) * HEAD_DIM:].reshape(b, n_kt, TK_F, N_HEADS, HEAD_DIM)
    v_t = jnp.transpose(v_cols, (0, 3, 1, 4, 2))
    return pl.pallas_call(
        _fox_kernel,
        grid=(b, N_HEADS, s // TQ_F),
        in_specs=[
            pl.BlockSpec((None, TQ_F, HEAD_DIM), lambda bi, h, qi: (bi, qi, base + h)),
            pl.BlockSpec((None, s, HEAD_DIM), lambda bi, h, qi: (bi, 0, base + N_HEADS + h)),
            pl.BlockSpec((None, None, n_kt, HEAD_DIM, TK_F), lambda bi, h, qi: (bi, h, 0, 0, 0)),
            pl.BlockSpec((None, s, LANES), lambda bi, h, qi: (bi, 0, 0)),
            pl.BlockSpec((None, None, None, 1, TQ_F), lambda bi, h, qi: (bi, h, qi, 0, 0)),
            pl.BlockSpec((None, None, 1, LANES), lambda bi, h, qi: (bi, h, 0, 0)),
            pl.BlockSpec((HEAD_DIM, 1), lambda bi, h, qi: (h, 0)),
        ],
        out_specs=pl.BlockSpec((None, TQ_F, HEAD_DIM), lambda bi, h, qi: (bi, qi, h)),
        scratch_shapes=[
            pltpu.VMEM((s, 2 * HEAD_DIM), BF16),
            pltpu.VMEM((1, LANES), F32),
            pltpu.VMEM((2, TK_F, TQ_F), F32),
            pltpu.VMEM((2, TK_F, TQ_F), BF16),
            pltpu.VMEM((1, TQ_F), F32),
            pltpu.VMEM((1, TQ_F), F32),
            pltpu.VMEM((1, TQ_F), F32),
            pltpu.VMEM((HEAD_DIM, TQ_F), F32),
        ],
        out_shape=jax.ShapeDtypeStruct((b, s, GROUP_W), BF16),
        compiler_params=_params(("arbitrary", "arbitrary", "arbitrary")),
        name="fox_attn",
    )(proj, proj, v_t, c, c_q, c_last, g_out)


def _out_proj_kernel(ms_ref, mf_ref, w_ref, x_ref, g_ref, x1_ref, h2_ref):
    acc = _dot(ms_ref[...], w_ref[:GROUP_W, :]) + _dot(mf_ref[...], w_ref[GROUP_W:, :])
    x1 = x_ref[...] + acc
    x1_ref[...] = x1
    var = jnp.mean(x1 * x1, axis=-1, keepdims=True)
    h2_ref[...] = (x1 * lax.rsqrt(var + EPS) * g_ref[...]).astype(BF16)


def _out_proj(mixed_sb, mixed_fox, w_out, x2, g_mlp):
    m, d = x2.shape
    return pl.pallas_call(
        _out_proj_kernel,
        grid=(m // TM_OUT,),
        in_specs=[
            pl.BlockSpec((TM_OUT, GROUP_W), lambda i: (i, 0)),
            pl.BlockSpec((TM_OUT, GROUP_W), lambda i: (i, 0)),
            pl.BlockSpec((2 * GROUP_W, d), lambda i: (0, 0)),
            pl.BlockSpec((TM_OUT, d), lambda i: (i, 0)),
            pl.BlockSpec((1, d), lambda i: (0, 0)),
        ],
        out_specs=[
            pl.BlockSpec((TM_OUT, d), lambda i: (i, 0)),
            pl.BlockSpec((TM_OUT, d), lambda i: (i, 0)),
        ],
        out_shape=[
            jax.ShapeDtypeStruct((m, d), F32),
            jax.ShapeDtypeStruct((m, d), BF16),
        ],
        compiler_params=_params(("arbitrary",)),
        name="out_proj",
    )(mixed_sb, mixed_fox, w_out, x2, g_mlp)


def _mlp_kernel(h_ref, x1_ref, wu_ref, wd_ref, g_ref, o_ref):
    f = pl.program_id(1)
    u = jnp.maximum(_dot(h_ref[...], wu_ref[...]), 0.0)
    part = _dot((u * u).astype(BF16), wd_ref[...])

    @pl.when(f == 0)
    def _():
        o_ref[...] = x1_ref[...] + part

    @pl.when(f > 0)
    def _():
        o_ref[...] += part

    @pl.when(f == pl.num_programs(1) - 1)
    def _():
        x2 = o_ref[...]
        var = jnp.mean(x2 * x2, axis=-1, keepdims=True)
        o_ref[...] = x2 * lax.rsqrt(var + EPS) * g_ref[...]


def _mlp(h2, x1, w_up, w_down, g_final):
    m, d = x1.shape
    dff = w_up.shape[1]
    return pl.pallas_call(
        _mlp_kernel,
        grid=(m // TM_MLP, dff // TF_MLP),
        in_specs=[
            pl.BlockSpec((TM_MLP, d), lambda i, f: (i, 0)),
            pl.BlockSpec((TM_MLP, d), lambda i, f: (i, 0)),
            pl.BlockSpec((d, TF_MLP), lambda i, f: (0, f)),
            pl.BlockSpec((TF_MLP, d), lambda i, f: (f, 0)),
            pl.BlockSpec((1, d), lambda i, f: (0, 0)),
        ],
        out_specs=pl.BlockSpec((TM_MLP, d), lambda i, f: (i, 0)),
        out_shape=jax.ShapeDtypeStruct((m, d), F32),
        compiler_params=_params(("arbitrary", "arbitrary")),
        name="mlp",
    )(h2, x1, w_up, w_down, g_final)


def kernel(x, g_attn, w_in, b_f, g_out_sb, g_out_fox, w_out, g_mlp, w_up, w_down, g_final):
    b, s, d = x.shape
    n_qkv = 6 * GROUP_W
    assert s % TQ == 0 and TQ == TK and (b * s) % TM_IN == 0
    x2 = x.reshape(b * s, d)

    idx = jnp.arange(TK)
    tri_excl = (idx[:, None] > idx[None, :]).astype(BF16)
    tri2 = jnp.concatenate([tri_excl, tri_excl], axis=0)
    cidx = jnp.arange(CS_BLK)
    tri_incl = (cidx[None, :] <= cidx[:, None]).astype(BF16)

    for l in range(g_attn.shape[0]):
        w_qkv = w_in[l, :, :n_qkv].astype(BF16)
        w_f = jnp.pad(w_in[l, :, n_qkv:], ((0, 0), (0, LANES - N_HEADS))).astype(BF16)
        b_pad = jnp.pad(b_f[l], (0, LANES - N_HEADS)).reshape(1, LANES)

        proj, f_logit = _in_proj(x2, g_attn[l].reshape(1, d), w_qkv, w_f)
        proj = proj.reshape(b, s, n_qkv)
        c = _forget_cs(f_logit.reshape(b, s, LANES), b_pad, tri_incl)

        mixed_sb = _sb_attn(proj, g_out_sb[l].reshape(1, GROUP_W), tri2)
        mixed_fox = _fox_attn(proj, c, g_out_fox[l].reshape(GROUP_W, 1))

        x1, h2 = _out_proj(mixed_sb.reshape(b * s, GROUP_W), mixed_fox.reshape(b * s, GROUP_W),
                           w_out[l].astype(BF16), x2, g_mlp[l].reshape(1, d))
        assert g_attn.shape[0] == 1
        x2 = _mlp(h2, x1, w_up[l].astype(BF16), w_down[l].astype(BF16), g_final.reshape(1, d))
    return x2.reshape(b, s, d)
```

```python
import functools

import jax
import jax.numpy as jnp
from jax import lax
from jax.experimental import pallas as pl
from jax.experimental.pallas import tpu as pltpu

F32 = jnp.float32
BF16 = jnp.bfloat16

HEAD_DIM = 128
N_HEADS = 8
GROUP_W = N_HEADS * HEAD_DIM
EPS = 1e-6
SCALE = HEAD_DIM ** -0.5
LOG2E = 1.4426950408889634
SB_DEAD_LOG2 = -152.0
FOX_DEAD_LOG2 = -152.0
NORM_SLACK = 1.01
M_INIT = -1e30
KN_BLK = 512
LANES = 128

VMEM_LIMIT = 56 * 1024 * 1024

TM_IN = 512
TN_IN = 1024
TQ_S = 1024
TK_S = 256
TQ_F = 512
TK_F = 512
VT_TK = 256
CS_BLK = 256
TM_OUT = 512
TM_MLP = 512
TF_MLP = 1024


def _params(sem):
    return pltpu.CompilerParams(dimension_semantics=sem, vmem_limit_bytes=VMEM_LIMIT)


def _dot(a, b):
    return jnp.dot(a, b, preferred_element_type=F32)


def _dot_nt(a, b):
    return lax.dot_general(a, b, (((1,), (1,)), ((), ())), preferred_element_type=F32)


def _log_sigmoid(x):
    return jnp.minimum(x, 0.0) - jnp.log(1.0 + jnp.exp(-jnp.abs(x)))


N_QK_GROUPS = 4


def _in_proj_kernel(x_ref, g_ref, w_ref, wf_ref, qk_ref, v_ref, f_ref, h_ref):
    j = pl.program_id(1)

    @pl.when(j == 0)
    def _():
        x = x_ref[...]
        var = jnp.mean(x * x, axis=-1, keepdims=True)
        h = (x * lax.rsqrt(var + EPS) * g_ref[...]).astype(BF16)
        h_ref[...] = h
        f_ref[...] = _dot(h, wf_ref[...])

    acc = _dot(h_ref[...], w_ref[...])

    @pl.when(j < N_QK_GROUPS)
    def _():
        scale = jnp.where((j == 0) | (j == 2), SCALE * LOG2E, 1.0).astype(F32)
        qk_ref[...] = (acc * scale).astype(BF16)

    @pl.when(j >= N_QK_GROUPS)
    def _():
        v_ref[...] = acc.astype(BF16)


def _in_proj(x2, g, w_qkv, w_f):
    m, d = x2.shape
    n = w_qkv.shape[1]
    assert TN_IN == GROUP_W and n == 6 * GROUP_W
    return pl.pallas_call(
        _in_proj_kernel,
        grid=(m // TM_IN, n // TN_IN),
        in_specs=[
            pl.BlockSpec((TM_IN, d), lambda i, j: (i, 0)),
            pl.BlockSpec((1, d), lambda i, j: (0, 0)),
            pl.BlockSpec((d, TN_IN), lambda i, j: (0, j)),
            pl.BlockSpec((d, LANES), lambda i, j: (0, 0)),
        ],
        out_specs=[
            pl.BlockSpec((TM_IN, TN_IN), lambda i, j: (i, jnp.minimum(j, N_QK_GROUPS - 1))),
            pl.BlockSpec((TM_IN, TN_IN), lambda i, j: (i, jnp.maximum(j - N_QK_GROUPS, 0))),
            pl.BlockSpec((TM_IN, LANES), lambda i, j: (i, 0)),
        ],
        out_shape=[
            jax.ShapeDtypeStruct((m, N_QK_GROUPS * GROUP_W), BF16),
            jax.ShapeDtypeStruct((m, 2 * GROUP_W), BF16),
            jax.ShapeDtypeStruct((m, LANES), F32),
        ],
        scratch_shapes=[pltpu.VMEM((TM_IN, d), BF16)],
        compiler_params=_params(("arbitrary", "arbitrary")),
        name="in_proj",
    )(x2, g, w_qkv, w_f)


def _forget_cs_kernel(f_ref, b_ref, tri_ref, c_ref):
    n_blk = f_ref.shape[0] // CS_BLK
    tri = tri_ref[...]

    def body(i, carry):
        r0 = pl.multiple_of(i * CS_BLK, CS_BLK)
        lf = _log_sigmoid(f_ref[pl.ds(r0, CS_BLK), :] + b_ref[...])
        p1 = lf.astype(BF16)
        r1 = lf - p1.astype(F32)
        p2 = r1.astype(BF16)
        p3 = (r1 - p2.astype(F32)).astype(BF16)
        c = _dot(tri, p1) + _dot(tri, p2) + _dot(tri, p3) + carry
        c_ref[pl.ds(r0, CS_BLK), :] = c * LOG2E
        return c[CS_BLK - 1:CS_BLK, :]

    lax.fori_loop(0, n_blk, body, jnp.zeros((1, LANES), F32))


def _forget_cs(f_logit, b_pad, tri_incl):
    b, s, _ = f_logit.shape
    return pl.pallas_call(
        _forget_cs_kernel,
        grid=(b,),
        in_specs=[
            pl.BlockSpec((None, s, LANES), lambda i: (i, 0, 0)),
            pl.BlockSpec((1, LANES), lambda i: (0, 0)),
            pl.BlockSpec((CS_BLK, CS_BLK), lambda i: (0, 0)),
        ],
        out_specs=pl.BlockSpec((None, s, LANES), lambda i: (i, 0, 0)),
        out_shape=jax.ShapeDtypeStruct((b, s, LANES), F32),
        compiler_params=_params(("arbitrary",)),
        name="forget_cs",
    )(f_logit, b_pad, tri_incl)


def _head_rmsnorm(o, g):
    return o * lax.rsqrt(jnp.mean(o * o, axis=-1, keepdims=True) + EPS) * g


def _col_reduce(x, pair, full):
    while x.shape[0] > 8:
        half = x.shape[0] // 2
        x = pair(x[:half], x[half:])
    return full(x, axis=0, keepdims=True)


def _store_head_out(o_ref, acc_t, g_col):
    o = acc_t * lax.rsqrt(jnp.mean(acc_t * acc_t, axis=0, keepdims=True) + EPS) * g_col
    o_ref[...] = jnp.transpose(o).astype(o_ref.dtype)


def _sb_kernel(q_ref, k_ref, vt_ref, g_ref, u2_ref, o_ref):
    qi = pl.program_id(2)
    u2 = u2_ref[...]
    n_sub = TQ_S // TK_S
    key = lax.broadcasted_iota(jnp.int32, (TK_S, TK_S), 0)
    qry = lax.broadcasted_iota(jnp.int32, (TK_S, TK_S), 1)
    causal = key < qry

    def stage_q(q, kb):
        ks = pl.multiple_of(kb * TK_S, TK_S)
        return _dot_nt(k_ref[pl.ds(ks, TK_S), :], q)

    def stage_e(z, diagonal):
        zn = jnp.minimum(z, 0.0)
        zp = zn - z
        sp = jnp.log2(1.0 + jnp.exp2(zn + zp))
        lb = zn - sp
        lk = zp - sp
        if diagonal:
            lk = jnp.where(causal, lk, 0.0)
        hi = lk.astype(BF16)
        lo = (lk - hi.astype(F32)).astype(BF16)
        return lb, jnp.concatenate([hi, lo], axis=0), _col_reduce(lk, jnp.add, jnp.sum)

    def stage_c(hilo):
        return _dot(u2, hilo)

    def stage_x(lb, rest, carry, diagonal):
        if diagonal:
            a = jnp.where(causal, jnp.exp2(lb + rest), 0.0)
        else:
            a = jnp.exp2(lb + (rest + carry))
        return a.astype(BF16)

    def stage_v(kb, a, gate=None):
        vt = vt_ref[kb]
        if gate is not None:
            vt = vt * gate
        return _dot(vt, a)

    items = []
    for sub in range(n_sub):
        kd = qi * n_sub + sub
        q = q_ref[sub * TK_S:(sub + 1) * TK_S, :]
        items.append(dict(q=q, kb=kd, diagonal=True, gate=None))
        items.append(dict(q=q, kb=jnp.maximum(kd - 1, 0), diagonal=False,
                          gate=jnp.where(kd > 0, 1.0, 0.0).astype(BF16)))
    n_items = len(items)
    for t in range(n_items + 4):
        if 4 <= t:
            it = items[t - 4]
            it["pv"] = stage_v(it["kb"], it["a"], it["gate"])
        if 2 <= t < n_items + 2:
            it = items[t - 2]
            it["rest"] = stage_c(it["hilo"])
        if t < n_items:
            it = items[t]
            it["z"] = stage_q(it["q"], it["kb"])
        if 3 <= t < n_items + 3:
            it = items[t - 3]
            carry_in = None if it["diagonal"] else items[t - 4]["colsum"]
            it["a"] = stage_x(it["lb"], it["rest"], carry_in, it["diagonal"])
        if 1 <= t < n_items + 1:
            it = items[t - 1]
            it["lb"], it["hilo"], it["colsum"] = stage_e(it["z"], it["diagonal"])

    qs = [items[2 * r]["q"] for r in range(n_sub)]
    state = []
    for r in range(n_sub):
        d, p = items[2 * r], items[2 * r + 1]
        state += [d["colsum"] + p["colsum"], d["pv"] + p["pv"]]

    kd_last = qi * n_sub + n_sub - 1

    def cond(st):
        worst = st[1]
        for r in range(1, n_sub):
            worst = jnp.maximum(worst, st[1 + 2 * r])
        return jnp.logical_and(kd_last - 2 - st[0] >= 0, jnp.max(worst) > SB_DEAD_LOG2)

    def body(st):
        j, out = st[0], []
        for r in range(n_sub):
            kb = qi * n_sub + r - 2 - j
            gate = jnp.where(kb >= 0, 1.0, 0.0).astype(BF16)
            kb = jnp.maximum(kb, 0)
            lb, hilo, colsum = stage_e(stage_q(qs[r], kb), False)
            a = stage_x(lb, stage_c(hilo), st[1 + 2 * r], False)
            out += [st[1 + 2 * r] + colsum, st[2 + 2 * r] + stage_v(kb, a, gate)]
        return (j + 1, *out)

    st = lax.while_loop(cond, body, (0, *state))
    _store_head_out(o_ref, jnp.concatenate([st[2 + 2 * r] for r in range(n_sub)], axis=1), g_ref[...])


def _sb_attn(qk, v_t, g_col, u2):
    b, s, _ = qk.shape
    assert VT_TK == TK_S and TQ_S % TK_S == 0
    return pl.pallas_call(
        _sb_kernel,
        grid=(b, N_HEADS, s // TQ_S),
        in_specs=[
            pl.BlockSpec((None, TQ_S, HEAD_DIM), lambda bi, h, qi: (bi, qi, h)),
            pl.BlockSpec((None, s, HEAD_DIM), lambda bi, h, qi: (bi, 0, N_HEADS + h)),
            pl.BlockSpec((None, None, s // VT_TK, HEAD_DIM, VT_TK), lambda bi, h, qi: (bi, h, 0, 0, 0)),
            pl.BlockSpec((HEAD_DIM, 1), lambda bi, h, qi: (h, 0)),
            pl.BlockSpec((TK_S, 2 * TK_S), lambda bi, h, qi: (0, 0)),
        ],
        out_specs=pl.BlockSpec((None, TQ_S, HEAD_DIM), lambda bi, h, qi: (bi, qi, h)),
        out_shape=jax.ShapeDtypeStruct((b, s, GROUP_W), BF16),
        compiler_params=_params(("arbitrary", "arbitrary", "arbitrary")),
        name="sb_attn",
    )(qk, qk, v_t, g_col, u2)


def _split3(x):
    p1 = x.astype(BF16).astype(F32)
    r = x - p1
    p2 = r.astype(BF16).astype(F32)
    p3 = (r - p2).astype(BF16).astype(F32)
    return p1, p2, p3


def _aug_lanes(x, first):
    p1, p2, p3 = _split3(x)
    lane = lax.broadcasted_iota(jnp.int32, (x.shape[0], LANES), 1)
    a = 0 if first else 3
    b = 3 - a
    ones = jnp.where((lane >= b) & (lane < b + 3), 1.0, 0.0)
    ext = jnp.where(lane == a, p1, jnp.where(lane == a + 1, p2, jnp.where(lane == a + 2, p3, ones)))
    return ext.astype(BF16)


def _fox_kernel(q_ref, k_ref, vt_ref, c_ref, cq_ref, clast_ref, g_ref, o_ref,
                kaug_ref, kn_ref, z_ref, p_ref, m_ref, l_ref, a_ref, acc_ref):
    h = pl.program_id(1)
    qi = pl.program_id(2)
    n_full = (qi * TQ_F) // TK_F

    def head_col(c_blk):
        lane = lax.broadcasted_iota(jnp.int32, c_blk.shape, 1)
        return jnp.sum(jnp.where(lane == h, c_blk, 0.0), axis=-1, keepdims=True)

    @pl.when(qi == 0)
    def _():
        def chunk(i, best):
            r0 = pl.multiple_of(i * KN_BLK, KN_BLK)
            k = k_ref[pl.ds(r0, KN_BLK), :]
            kaug_ref[pl.ds(r0, KN_BLK), :HEAD_DIM] = k
            kaug_ref[pl.ds(r0, KN_BLK), HEAD_DIM:] = _aug_lanes(-head_col(c_ref[pl.ds(r0, KN_BLK), :]), True)
            kf = k.astype(F32)
            n2 = jnp.sum(kf * kf, axis=-1, keepdims=True)
            return jnp.maximum(best, jnp.max(n2, axis=0, keepdims=True))

        best = lax.fori_loop(0, k_ref.shape[0] // KN_BLK, chunk, jnp.zeros((1, 1), F32))
        kn_ref[...] = jnp.broadcast_to(jnp.sqrt(best), kn_ref.shape)

    q = q_ref[...]
    q0 = pl.multiple_of(qi * TQ_F, TQ_F)
    q_aug = jnp.concatenate([q, _aug_lanes(head_col(c_ref[pl.ds(q0, TQ_F), :]), False)], axis=1)

    def row_sums(x):
        hi = x.astype(BF16)
        lo = (x - hi.astype(F32)).astype(BF16)
        ones = jnp.ones((8, x.shape[1]), BF16)
        return (_dot_nt(ones, hi) + _dot_nt(ones, lo))[:1]

    qf = q.astype(F32)
    diag = row_sums(q_aug.astype(F32) * kaug_ref[pl.ds(q0, TQ_F), :].astype(F32))
    ub = jnp.sqrt(row_sums(qf * qf)) * kn_ref[:, :1] * NORM_SLACK + cq_ref[...]
    gap = jnp.max(ub - diag, axis=1, keepdims=True)
    tile_idx = lax.broadcasted_iota(jnp.int32, (1, LANES), 1)
    live = jnp.logical_and(tile_idx < n_full, gap - clast_ref[...] > FOX_DEAD_LOG2)
    n_tiles = 1 + jnp.sum(live.astype(jnp.int32))

    vt_per_tile = TK_F // VT_TK
    last_tile = vt_ref.shape[0] // vt_per_tile - 1

    def kb_of(i):
        return jnp.clip(n_full - i, 0, last_tile)

    def qk_t(i):
        ks = pl.multiple_of(kb_of(i) * TK_F, TK_F)
        return _dot_nt(kaug_ref[pl.ds(ks, TK_F), :], q_aug)

    def pv_t(i, slot):
        kb = kb_of(i)
        out = None
        for r in range(vt_per_tile):
            part = _dot(vt_ref[kb * vt_per_tile + r], p_ref[slot, r * VT_TK:(r + 1) * VT_TK, :])
            out = part if out is None else out + part
        return out

    st = qk_t(0)
    z_ref[1] = qk_t(1)
    key = n_full * TK_F + lax.broadcasted_iota(jnp.int32, (TK_F, TQ_F), 0)
    qry = qi * TQ_F + lax.broadcasted_iota(jnp.int32, (TK_F, TQ_F), 1)
    st = jnp.where(key <= qry, st, -jnp.inf)
    m0 = _col_reduce(st, jnp.maximum, jnp.max)
    p0 = jnp.exp2(st - m0)
    p_ref[0] = p0.astype(BF16)
    m_ref[...] = m0
    l_ref[...] = _col_reduce(p0, jnp.add, jnp.sum)
    a_ref[...] = jnp.ones_like(m0)
    acc_ref[...] = jnp.zeros_like(acc_ref)

    def step(i, slot, active=None):
        z_ref[1 - slot] = qk_t(i + 1)
        acc_ref[...] = a_ref[...] * acc_ref[...] + pv_t(i - 1, 1 - slot)
        z = z_ref[slot]
        if active is not None:
            z = jnp.where(active, z, -jnp.inf)
        m = m_ref[...]
        m_new = jnp.maximum(m, _col_reduce(z, jnp.maximum, jnp.max))
        alpha = jnp.exp2(m - m_new)
        p = jnp.exp2(z_ref[slot] - m_new) if active is None else jnp.exp2(z - m_new)
        p_ref[slot] = p.astype(BF16)
        l_ref[...] = alpha * l_ref[...] + _col_reduce(p, jnp.add, jnp.sum)
        m_ref[...] = m_new
        a_ref[...] = alpha

    def body(j, _):
        i = 1 + 2 * j
        step(i, 1)
        step(i + 1, 0, active=i + 1 < n_tiles)
        return 0

    n_pairs = n_tiles // 2
    lax.fori_loop(0, n_pairs, body, 0)
    acc = a_ref[...] * acc_ref[...] + pv_t(2 * n_pairs, 0)
    _store_head_out(o_ref, acc / l_ref[...], g_ref[...])


def _fox_attn(qk, v_t, c, g_col):
    b, s, _ = qk.shape
    base = 2 * N_HEADS
    n_kt = s // TK_F
    assert TK_F % TQ_F == 0 and TK_F % VT_TK == 0
    assert n_kt <= LANES
    c_hs = jnp.transpose(c[:, :, :N_HEADS], (0, 2, 1))
    c_q = c_hs.reshape(b, N_HEADS, s // TQ_F, 1, TQ_F)
    c_last = jnp.pad(c_hs.reshape(b, N_HEADS, n_kt, TK_F)[..., TK_F - 1], ((0, 0), (0, 0), (0, LANES - n_kt)))
    c_last = c_last.reshape(b, N_HEADS, 1, LANES)
    return pl.pallas_call(
        _fox_kernel,
        grid=(b, N_HEADS, s // TQ_F),
        in_specs=[
            pl.BlockSpec((None, TQ_F, HEAD_DIM), lambda bi, h, qi: (bi, qi, base + h)),
            pl.BlockSpec((None, s, HEAD_DIM), lambda bi, h, qi: (bi, 0, base + N_HEADS + h)),
            pl.BlockSpec((None, None, s // VT_TK, HEAD_DIM, VT_TK),
                         lambda bi, h, qi: (bi, N_HEADS + h, 0, 0, 0)),
            pl.BlockSpec((None, s, LANES), lambda bi, h, qi: (bi, 0, 0)),
            pl.BlockSpec((None, None, None, 1, TQ_F), lambda bi, h, qi: (bi, h, qi, 0, 0)),
            pl.BlockSpec((None, None, 1, LANES), lambda bi, h, qi: (bi, h, 0, 0)),
            pl.BlockSpec((HEAD_DIM, 1), lambda bi, h, qi: (h, 0)),
        ],
        out_specs=pl.BlockSpec((None, TQ_F, HEAD_DIM), lambda bi, h, qi: (bi, qi, h)),
        scratch_shapes=[
            pltpu.VMEM((s, 2 * HEAD_DIM), BF16),
            pltpu.VMEM((1, LANES), F32),
            pltpu.VMEM((2, TK_F, TQ_F), F32),
            pltpu.VMEM((2, TK_F, TQ_F), BF16),
            pltpu.VMEM((1, TQ_F), F32),
            pltpu.VMEM((1, TQ_F), F32),
            pltpu.VMEM((1, TQ_F), F32),
            pltpu.VMEM((HEAD_DIM, TQ_F), F32),
        ],
        out_shape=jax.ShapeDtypeStruct((b, s, GROUP_W), BF16),
        compiler_params=_params(("arbitrary", "arbitrary", "arbitrary")),
        name="fox_attn",
    )(qk, qk, v_t, c, c_q, c_last, g_col)


def _out_proj_kernel(ms_ref, mf_ref, w_ref, x_ref, g_ref, x1_ref, h2_ref):
    acc = _dot(ms_ref[...], w_ref[:GROUP_W, :]) + _dot(mf_ref[...], w_ref[GROUP_W:, :])
    x1 = x_ref[...] + acc
    x1_ref[...] = x1
    var = jnp.mean(x1 * x1, axis=-1, keepdims=True)
    h2_ref[...] = (x1 * lax.rsqrt(var + EPS) * g_ref[...]).astype(BF16)


def _out_proj(mixed_sb, mixed_fox, w_out, x2, g_mlp):
    m, d = x2.shape
    return pl.pallas_call(
        _out_proj_kernel,
        grid=(m // TM_OUT,),
        in_specs=[
            pl.BlockSpec((TM_OUT, GROUP_W), lambda i: (i, 0)),
            pl.BlockSpec((TM_OUT, GROUP_W), lambda i: (i, 0)),
            pl.BlockSpec((2 * GROUP_W, d), lambda i: (0, 0)),
            pl.BlockSpec((TM_OUT, d), lambda i: (i, 0)),
            pl.BlockSpec((1, d), lambda i: (0, 0)),
        ],
        out_specs=[
            pl.BlockSpec((TM_OUT, d), lambda i: (i, 0)),
            pl.BlockSpec((TM_OUT, d), lambda i: (i, 0)),
        ],
        out_shape=[
            jax.ShapeDtypeStruct((m, d), F32),
            jax.ShapeDtypeStruct((m, d), BF16),
        ],
        compiler_params=_params(("arbitrary",)),
        name="out_proj",
    )(mixed_sb, mixed_fox, w_out, x2, g_mlp)


def _mlp_kernel(h_ref, x1_ref, wu_ref, wd_ref, g_ref, o_ref):
    f = pl.program_id(1)
    u = jnp.maximum(_dot(h_ref[...], wu_ref[...]), 0.0)
    part = _dot((u * u).astype(BF16), wd_ref[...])

    @pl.when(f == 0)
    def _():
        o_ref[...] = x1_ref[...] + part

    @pl.when(f > 0)
    def _():
        o_ref[...] += part

    @pl.when(f == pl.num_programs(1) - 1)
    def _():
        x2 = o_ref[...]
        var = jnp.mean(x2 * x2, axis=-1, keepdims=True)
        o_ref[...] = x2 * lax.rsqrt(var + EPS) * g_ref[...]


def _mlp(h2, x1, w_up, w_down, g_final):
    m, d = x1.shape
    dff = w_up.shape[1]
    return pl.pallas_call(
        _mlp_kernel,
        grid=(m // TM_MLP, dff // TF_MLP),
        in_specs=[
            pl.BlockSpec((TM_MLP, d), lambda i, f: (i, 0)),
            pl.BlockSpec((TM_MLP, d), lambda i, f: (i, 0)),
            pl.BlockSpec((d, TF_MLP), lambda i, f: (0, f)),
            pl.BlockSpec((TF_MLP, d), lambda i, f: (f, 0)),
            pl.BlockSpec((1, d), lambda i, f: (0, 0)),
        ],
        out_specs=pl.BlockSpec((TM_MLP, d), lambda i, f: (i, 0)),
        out_shape=jax.ShapeDtypeStruct((m, d), F32),
        compiler_params=_params(("arbitrary", "arbitrary")),
        name="mlp",
    )(h2, x1, w_up, w_down, g_final)


def kernel(x, g_attn, w_in, b_f, g_out_sb, g_out_fox, w_out, g_mlp, w_up, w_down, g_final):
    b, s, d = x.shape
    n_qkv = 6 * GROUP_W
    assert s % TQ_S == 0 and s % TK_F == 0 and (b * s) % TM_IN == 0
    x2 = x.reshape(b * s, d)

    idx = jnp.arange(TK_S)
    u_excl = (idx[None, :] > idx[:, None]).astype(BF16)
    u2 = jnp.concatenate([u_excl, u_excl], axis=1)
    cidx = jnp.arange(CS_BLK)
    tri_incl = (cidx[None, :] <= cidx[:, None]).astype(BF16)

    gw = GROUP_W
    for l in range(g_attn.shape[0]):
        w = w_in[l]
        w_qkv = jnp.concatenate([w[:, 0:2 * gw], w[:, 3 * gw:5 * gw], w[:, 2 * gw:3 * gw],
                                 w[:, 5 * gw:6 * gw]], axis=1).astype(BF16)
        w_f = jnp.pad(w[:, n_qkv:], ((0, 0), (0, LANES - N_HEADS))).astype(BF16)
        b_pad = jnp.pad(b_f[l], (0, LANES - N_HEADS)).reshape(1, LANES)

        qk, v, f_logit = _in_proj(x2, g_attn[l].reshape(1, d), w_qkv, w_f)
        qk = qk.reshape(b, s, N_QK_GROUPS * gw)
        v_t = jnp.transpose(v.reshape(b, s // VT_TK, VT_TK, 2 * N_HEADS, HEAD_DIM), (0, 3, 1, 4, 2))
        c = _forget_cs(f_logit.reshape(b, s, LANES), b_pad, tri_incl)

        mixed_sb = _sb_attn(qk, v_t, g_out_sb[l].reshape(gw, 1), u2)
        mixed_fox = _fox_attn(qk, v_t, c, g_out_fox[l].reshape(gw, 1))

        x1, h2 = _out_proj(mixed_sb.reshape(b * s, GROUP_W), mixed_fox.reshape(b * s, GROUP_W),
                           w_out[l].astype(BF16), x2, g_mlp[l].reshape(1, d))
        assert g_attn.shape[0] == 1
        x2 = _mlp(h2, x1, w_up[l].astype(BF16), w_down[l].astype(BF16), g_final.reshape(1, d))
    return x2.reshape(b, s, d)
```

```python
import functools

import jax
import jax.numpy as jnp
from jax import lax
from jax.experimental import pallas as pl
from jax.experimental.pallas import tpu as pltpu

F32 = jnp.float32
BF16 = jnp.bfloat16

HEAD_DIM = 128
N_HEADS = 8
GROUP_W = N_HEADS * HEAD_DIM
EPS = 1e-6
SCALE = HEAD_DIM ** -0.5
LOG2E = 1.4426950408889634
SB_DEAD_LOG2 = -152.0
FOX_DEAD_LOG2 = -152.0
NORM_SLACK = 1.01
M_INIT = -1e30
KN_BLK = 512
LANES = 128

VMEM_LIMIT = 56 * 1024 * 1024

TM_IN = 1024
TN_IN = 1024
TQ_S = 1024
TK_S = 256
TQ_F = 512
TK_F = 512
VT_TK = 256
CS_BLK = 256
TM_OUT = 512
TM_MLP = 512
TF_MLP = 1024


def _params(sem):
    return pltpu.CompilerParams(dimension_semantics=sem, vmem_limit_bytes=VMEM_LIMIT)


def _dot(a, b):
    return jnp.dot(a, b, preferred_element_type=F32)


def _dot_nt(a, b):
    return lax.dot_general(a, b, (((1,), (1,)), ((), ())), preferred_element_type=F32)


def _log_sigmoid(x):
    return jnp.minimum(x, 0.0) - jnp.log(1.0 + jnp.exp(-jnp.abs(x)))


N_QK_GROUPS = 4


def _in_proj_kernel(x_ref, g_ref, w_ref, wf_ref, qk_ref, v_ref, f_ref, h_ref):
    j = pl.program_id(1)

    @pl.when(j == 0)
    def _():
        x = x_ref[...]
        var = jnp.mean(x * x, axis=-1, keepdims=True)
        h = (x * lax.rsqrt(var + EPS) * g_ref[...]).astype(BF16)
        h_ref[...] = h
        f_ref[...] = _dot_nt(h, wf_ref[...])

    scale = jnp.where((j == 0) | (j == 2), SCALE * LOG2E, 1.0).astype(F32)
    out = (_dot_nt(h_ref[...], w_ref[...]) * scale).astype(BF16)

    @pl.when(j < N_QK_GROUPS)
    def _():
        qk_ref[...] = out

    @pl.when(j >= N_QK_GROUPS)
    def _():
        v_ref[...] = out


def _w_group(j):
    return jnp.where(j < 2, j, jnp.where(j < 4, j + 1, jnp.where(j == 4, 2, 5)))


def _in_proj(x2, g, w_qkv_t, w_f_t):
    m, d = x2.shape
    n = 6 * GROUP_W
    assert TN_IN == GROUP_W and w_qkv_t.shape[0] >= n
    return pl.pallas_call(
        _in_proj_kernel,
        grid=(m // TM_IN, n // TN_IN),
        in_specs=[
            pl.BlockSpec((TM_IN, d), lambda i, j: (i, 0)),
            pl.BlockSpec((1, d), lambda i, j: (0, 0)),
            pl.BlockSpec((TN_IN, d), lambda i, j: (_w_group(j), 0)),
            pl.BlockSpec((LANES, d), lambda i, j: (0, 0)),
        ],
        out_specs=[
            pl.BlockSpec((TM_IN, TN_IN), lambda i, j: (i, jnp.minimum(j, N_QK_GROUPS - 1))),
            pl.BlockSpec((TM_IN, TN_IN), lambda i, j: (i, jnp.maximum(j - N_QK_GROUPS, 0))),
            pl.BlockSpec((TM_IN, LANES), lambda i, j: (i, 0)),
        ],
        out_shape=[
            jax.ShapeDtypeStruct((m, N_QK_GROUPS * GROUP_W), BF16),
            jax.ShapeDtypeStruct((m, 2 * GROUP_W), BF16),
            jax.ShapeDtypeStruct((m, LANES), F32),
        ],
        scratch_shapes=[pltpu.VMEM((TM_IN, d), BF16)],
        compiler_params=_params(("arbitrary", "arbitrary")),
        name="in_proj",
    )(x2, g, w_qkv_t, w_f_t)


def _forget_cs_kernel(f_ref, b_ref, tri_ref, c_ref):
    n_blk = f_ref.shape[0] // CS_BLK
    tri = tri_ref[...]

    def body(i, carry):
        r0 = pl.multiple_of(i * CS_BLK, CS_BLK)
        lf = _log_sigmoid(f_ref[pl.ds(r0, CS_BLK), :] + b_ref[...])
        p1 = lf.astype(BF16)
        r1 = lf - p1.astype(F32)
        p2 = r1.astype(BF16)
        p3 = (r1 - p2.astype(F32)).astype(BF16)
        c = _dot(tri, p1) + _dot(tri, p2) + _dot(tri, p3) + carry
        c_ref[pl.ds(r0, CS_BLK), :] = c * LOG2E
        return c[CS_BLK - 1:CS_BLK, :]

    lax.fori_loop(0, n_blk, body, jnp.zeros((1, LANES), F32))


def _forget_cs(f_logit, b_pad, tri_incl):
    b, s, _ = f_logit.shape
    return pl.pallas_call(
        _forget_cs_kernel,
        grid=(b,),
        in_specs=[
            pl.BlockSpec((None, s, LANES), lambda i: (i, 0, 0)),
            pl.BlockSpec((1, LANES), lambda i: (0, 0)),
            pl.BlockSpec((CS_BLK, CS_BLK), lambda i: (0, 0)),
        ],
        out_specs=pl.BlockSpec((None, s, LANES), lambda i: (i, 0, 0)),
        out_shape=jax.ShapeDtypeStruct((b, s, LANES), F32),
        compiler_params=_params(("arbitrary",)),
        name="forget_cs",
    )(f_logit, b_pad, tri_incl)


def _head_rmsnorm(o, g):
    return o * lax.rsqrt(jnp.mean(o * o, axis=-1, keepdims=True) + EPS) * g


def _col_reduce(x, pair, full):
    while x.shape[0] > 8:
        half = x.shape[0] // 2
        x = pair(x[:half], x[half:])
    return full(x, axis=0, keepdims=True)


def _store_head_out(o_ref, acc_t, g_col):
    o = acc_t * lax.rsqrt(jnp.mean(acc_t * acc_t, axis=0, keepdims=True) + EPS) * g_col
    o_ref[...] = jnp.transpose(o).astype(o_ref.dtype)


def _sb_kernel(q_ref, k_ref, vt_ref, g_ref, u2_ref, o_ref):
    qi = pl.program_id(2)
    u2 = u2_ref[...]
    n_sub = TQ_S // TK_S
    key = lax.broadcasted_iota(jnp.int32, (TK_S, TK_S), 0)
    qry = lax.broadcasted_iota(jnp.int32, (TK_S, TK_S), 1)
    causal = key < qry

    def stage_q(q, kb):
        ks = pl.multiple_of(kb * TK_S, TK_S)
        return _dot_nt(k_ref[pl.ds(ks, TK_S), :], q)

    def stage_e(z, diagonal):
        zn = jnp.minimum(z, 0.0)
        zp = zn - z
        sp = jnp.log2(1.0 + jnp.exp2(zn + zp))
        lb = zn - sp
        lk = zp - sp
        if diagonal:
            lk = jnp.where(causal, lk, 0.0)
        hi = lk.astype(BF16)
        lo = (lk - hi.astype(F32)).astype(BF16)
        return lb, jnp.concatenate([hi, lo], axis=0), _col_reduce(lk, jnp.add, jnp.sum)

    def stage_c(hilo):
        return _dot(u2, hilo)

    def stage_x(lb, rest, carry, diagonal):
        if diagonal:
            a = jnp.where(causal, jnp.exp2(lb + rest), 0.0)
        else:
            a = jnp.exp2(lb + (rest + carry))
        return a.astype(BF16)

    def stage_v(kb, a, gate=None):
        vt = vt_ref[kb]
        if gate is not None:
            vt = vt * gate
        return _dot(vt, a)

    items = []
    for sub in range(n_sub):
        kd = qi * n_sub + sub
        q = q_ref[sub * TK_S:(sub + 1) * TK_S, :]
        items.append(dict(q=q, kb=kd, diagonal=True, gate=None))
        items.append(dict(q=q, kb=jnp.maximum(kd - 1, 0), diagonal=False,
                          gate=jnp.where(kd > 0, 1.0, 0.0).astype(BF16)))
    n_items = len(items)
    for t in range(n_items + 4):
        if 4 <= t:
            it = items[t - 4]
            it["pv"] = stage_v(it["kb"], it["a"], it["gate"])
        if 2 <= t < n_items + 2:
            it = items[t - 2]
            it["rest"] = stage_c(it["hilo"])
        if t < n_items:
            it = items[t]
            it["z"] = stage_q(it["q"], it["kb"])
        if 3 <= t < n_items + 3:
            it = items[t - 3]
            carry_in = None if it["diagonal"] else items[t - 4]["colsum"]
            it["a"] = stage_x(it["lb"], it["rest"], carry_in, it["diagonal"])
        if 1 <= t < n_items + 1:
            it = items[t - 1]
            it["lb"], it["hilo"], it["colsum"] = stage_e(it["z"], it["diagonal"])

    qs = [items[2 * r]["q"] for r in range(n_sub)]
    state = []
    for r in range(n_sub):
        d, p = items[2 * r], items[2 * r + 1]
        state += [d["colsum"] + p["colsum"], d["pv"] + p["pv"]]

    kd_last = qi * n_sub + n_sub - 1

    def cond(st):
        worst = st[1]
        for r in range(1, n_sub):
            worst = jnp.maximum(worst, st[1 + 2 * r])
        return jnp.logical_and(kd_last - 2 - st[0] >= 0, jnp.max(worst) > SB_DEAD_LOG2)

    def body(st):
        j, out = st[0], []
        for r in range(n_sub):
            kb = qi * n_sub + r - 2 - j
            gate = jnp.where(kb >= 0, 1.0, 0.0).astype(BF16)
            kb = jnp.maximum(kb, 0)
            lb, hilo, colsum = stage_e(stage_q(qs[r], kb), False)
            a = stage_x(lb, stage_c(hilo), st[1 + 2 * r], False)
            out += [st[1 + 2 * r] + colsum, st[2 + 2 * r] + stage_v(kb, a, gate)]
        return (j + 1, *out)

    st = lax.while_loop(cond, body, (0, *state))
    _store_head_out(o_ref, jnp.concatenate([st[2 + 2 * r] for r in range(n_sub)], axis=1), g_ref[...])


def _sb_attn(qk, v_t, g_col, u2):
    b, s, _ = qk.shape
    assert VT_TK == TK_S and TQ_S % TK_S == 0
    return pl.pallas_call(
        _sb_kernel,
        grid=(b, N_HEADS, s // TQ_S),
        in_specs=[
            pl.BlockSpec((None, TQ_S, HEAD_DIM), lambda bi, h, qi: (bi, qi, h)),
            pl.BlockSpec((None, s, HEAD_DIM), lambda bi, h, qi: (bi, 0, N_HEADS + h)),
            pl.BlockSpec((None, None, s // VT_TK, HEAD_DIM, VT_TK), lambda bi, h, qi: (bi, h, 0, 0, 0)),
            pl.BlockSpec((HEAD_DIM, 1), lambda bi, h, qi: (h, 0)),
            pl.BlockSpec((TK_S, 2 * TK_S), lambda bi, h, qi: (0, 0)),
        ],
        out_specs=pl.BlockSpec((None, TQ_S, HEAD_DIM), lambda bi, h, qi: (bi, qi, h)),
        out_shape=jax.ShapeDtypeStruct((b, s, GROUP_W), BF16),
        compiler_params=_params(("arbitrary", "arbitrary", "arbitrary")),
        name="sb_attn",
    )(qk, qk, v_t, g_col, u2)


def _split3(x):
    p1 = x.astype(BF16).astype(F32)
    r = x - p1
    p2 = r.astype(BF16).astype(F32)
    p3 = (r - p2).astype(BF16).astype(F32)
    return p1, p2, p3


def _aug_lanes(x, first):
    p1, p2, p3 = _split3(x)
    lane = lax.broadcasted_iota(jnp.int32, (x.shape[0], LANES), 1)
    a = 0 if first else 3
    b = 3 - a
    ones = jnp.where((lane >= b) & (lane < b + 3), 1.0, 0.0)
    ext = jnp.where(lane == a, p1, jnp.where(lane == a + 1, p2, jnp.where(lane == a + 2, p3, ones)))
    return ext.astype(BF16)


def _fox_kernel(q_ref, k_ref, vt_ref, c_ref, cq_ref, clast_ref, g_ref, o_ref,
                kaug_ref, kn_ref, z_ref, p_ref, m_ref, l_ref, a_ref, acc_ref):
    h = pl.program_id(1)
    qi = pl.program_id(2)
    n_full = (qi * TQ_F) // TK_F

    def head_col(c_blk):
        lane = lax.broadcasted_iota(jnp.int32, c_blk.shape, 1)
        return jnp.sum(jnp.where(lane == h, c_blk, 0.0), axis=-1, keepdims=True)

    @pl.when(qi == 0)
    def _():
        def chunk(i, best):
            r0 = pl.multiple_of(i * KN_BLK, KN_BLK)
            k = k_ref[pl.ds(r0, KN_BLK), :]
            kaug_ref[pl.ds(r0, KN_BLK), :HEAD_DIM] = k
            kaug_ref[pl.ds(r0, KN_BLK), HEAD_DIM:] = _aug_lanes(-head_col(c_ref[pl.ds(r0, KN_BLK), :]), True)
            kf = k.astype(F32)
            n2 = jnp.sum(kf * kf, axis=-1, keepdims=True)
            return jnp.maximum(best, jnp.max(n2, axis=0, keepdims=True))

        best = lax.fori_loop(0, k_ref.shape[0] // KN_BLK, chunk, jnp.zeros((1, 1), F32))
        kn_ref[...] = jnp.broadcast_to(jnp.sqrt(best), kn_ref.shape)

    q = q_ref[...]
    q0 = pl.multiple_of(qi * TQ_F, TQ_F)
    q_aug = jnp.concatenate([q, _aug_lanes(head_col(c_ref[pl.ds(q0, TQ_F), :]), False)], axis=1)

    def row_sums(x):
        hi = x.astype(BF16)
        lo = (x - hi.astype(F32)).astype(BF16)
        ones = jnp.ones((8, x.shape[1]), BF16)
        return (_dot_nt(ones, hi) + _dot_nt(ones, lo))[:1]

    qf = q.astype(F32)
    diag = row_sums(q_aug.astype(F32) * kaug_ref[pl.ds(q0, TQ_F), :].astype(F32))
    ub = jnp.sqrt(row_sums(qf * qf)) * kn_ref[:, :1] * NORM_SLACK + cq_ref[...]
    gap = jnp.max(ub - diag, axis=1, keepdims=True)
    tile_idx = lax.broadcasted_iota(jnp.int32, (1, LANES), 1)
    live = jnp.logical_and(tile_idx < n_full, gap - clast_ref[...] > FOX_DEAD_LOG2)
    n_tiles = 1 + jnp.sum(live.astype(jnp.int32))

    vt_per_tile = TK_F // VT_TK
    last_tile = vt_ref.shape[0] // vt_per_tile - 1

    def kb_of(i):
        return jnp.clip(n_full - i, 0, last_tile)

    def qk_t(i):
        ks = pl.multiple_of(kb_of(i) * TK_F, TK_F)
        return _dot_nt(kaug_ref[pl.ds(ks, TK_F), :], q_aug)

    def pv_t(i, slot):
        kb = kb_of(i)
        out = None
        for r in range(vt_per_tile):
            part = _dot(vt_ref[kb * vt_per_tile + r], p_ref[slot, r * VT_TK:(r + 1) * VT_TK, :])
            out = part if out is None else out + part
        return out

    st = qk_t(0)
    z_ref[1] = qk_t(1)
    key = n_full * TK_F + lax.broadcasted_iota(jnp.int32, (TK_F, TQ_F), 0)
    qry = qi * TQ_F + lax.broadcasted_iota(jnp.int32, (TK_F, TQ_F), 1)
    st = jnp.where(key <= qry, st, -jnp.inf)
    m0 = _col_reduce(st, jnp.maximum, jnp.max)
    p0 = jnp.exp2(st - m0)
    p_ref[0] = p0.astype(BF16)
    m_ref[...] = m0
    l_ref[...] = _col_reduce(p0, jnp.add, jnp.sum)
    a_ref[...] = jnp.ones_like(m0)
    acc_ref[...] = jnp.zeros_like(acc_ref)

    def step(i, slot, active=None):
        z_ref[1 - slot] = qk_t(i + 1)
        acc_ref[...] = a_ref[...] * acc_ref[...] + pv_t(i - 1, 1 - slot)
        z = z_ref[slot]
        if active is not None:
            z = jnp.where(active, z, -jnp.inf)
        m = m_ref[...]
        m_new = jnp.maximum(m, _col_reduce(z, jnp.maximum, jnp.max))
        alpha = jnp.exp2(m - m_new)
        p = jnp.exp2(z_ref[slot] - m_new) if active is None else jnp.exp2(z - m_new)
        p_ref[slot] = p.astype(BF16)
        l_ref[...] = alpha * l_ref[...] + _col_reduce(p, jnp.add, jnp.sum)
        m_ref[...] = m_new
        a_ref[...] = alpha

    def body(j, _):
        i = 1 + 2 * j
        step(i, 1)
        step(i + 1, 0, active=i + 1 < n_tiles)
        return 0

    n_pairs = n_tiles // 2
    lax.fori_loop(0, n_pairs, body, 0)
    acc = a_ref[...] * acc_ref[...] + pv_t(2 * n_pairs, 0)
    _store_head_out(o_ref, acc / l_ref[...], g_ref[...])


def _fox_attn(qk, v_t, c, g_col):
    b, s, _ = qk.shape
    base = 2 * N_HEADS
    n_kt = s // TK_F
    assert TK_F % TQ_F == 0 and TK_F % VT_TK == 0
    assert n_kt <= LANES
    c_hs = jnp.transpose(c[:, :, :N_HEADS], (0, 2, 1))
    c_q = c_hs.reshape(b, N_HEADS, s // TQ_F, 1, TQ_F)
    c_last = jnp.pad(c_hs.reshape(b, N_HEADS, n_kt, TK_F)[..., TK_F - 1], ((0, 0), (0, 0), (0, LANES - n_kt)))
    c_last = c_last.reshape(b, N_HEADS, 1, LANES)
    return pl.pallas_call(
        _fox_kernel,
        grid=(b, N_HEADS, s // TQ_F),
        in_specs=[
            pl.BlockSpec((None, TQ_F, HEAD_DIM), lambda bi, h, qi: (bi, qi, base + h)),
            pl.BlockSpec((None, s, HEAD_DIM), lambda bi, h, qi: (bi, 0, base + N_HEADS + h)),
            pl.BlockSpec((None, None, s // VT_TK, HEAD_DIM, VT_TK),
                         lambda bi, h, qi: (bi, N_HEADS + h, 0, 0, 0)),
            pl.BlockSpec((None, s, LANES), lambda bi, h, qi: (bi, 0, 0)),
            pl.BlockSpec((None, None, None, 1, TQ_F), lambda bi, h, qi: (bi, h, qi, 0, 0)),
            pl.BlockSpec((None, None, 1, LANES), lambda bi, h, qi: (bi, h, 0, 0)),
            pl.BlockSpec((HEAD_DIM, 1), lambda bi, h, qi: (h, 0)),
        ],
        out_specs=pl.BlockSpec((None, TQ_F, HEAD_DIM), lambda bi, h, qi: (bi, qi, h)),
        scratch_shapes=[
            pltpu.VMEM((s, 2 * HEAD_DIM), BF16),
            pltpu.VMEM((1, LANES), F32),
            pltpu.VMEM((2, TK_F, TQ_F), F32),
            pltpu.VMEM((2, TK_F, TQ_F), BF16),
            pltpu.VMEM((1, TQ_F), F32),
            pltpu.VMEM((1, TQ_F), F32),
            pltpu.VMEM((1, TQ_F), F32),
            pltpu.VMEM((HEAD_DIM, TQ_F), F32),
        ],
        out_shape=jax.ShapeDtypeStruct((b, s, GROUP_W), BF16),
        compiler_params=_params(("arbitrary", "arbitrary", "arbitrary")),
        name="fox_attn",
    )(qk, qk, v_t, c, c_q, c_last, g_col)


def _out_proj_kernel(ms_ref, mf_ref, w_ref, x_ref, g_ref, x1_ref, h2_ref):
    acc = _dot(ms_ref[...], w_ref[:GROUP_W, :]) + _dot(mf_ref[...], w_ref[GROUP_W:, :])
    x1 = x_ref[...] + acc
    x1_ref[...] = x1
    var = jnp.mean(x1 * x1, axis=-1, keepdims=True)
    h2_ref[...] = (x1 * lax.rsqrt(var + EPS) * g_ref[...]).astype(BF16)


def _out_proj(mixed_sb, mixed_fox, w_out, x2, g_mlp):
    m, d = x2.shape
    return pl.pallas_call(
        _out_proj_kernel,
        grid=(m // TM_OUT,),
        in_specs=[
            pl.BlockSpec((TM_OUT, GROUP_W), lambda i: (i, 0)),
            pl.BlockSpec((TM_OUT, GROUP_W), lambda i: (i, 0)),
            pl.BlockSpec((2 * GROUP_W, d), lambda i: (0, 0)),
            pl.BlockSpec((TM_OUT, d), lambda i: (i, 0)),
            pl.BlockSpec((1, d), lambda i: (0, 0)),
        ],
        out_specs=[
            pl.BlockSpec((TM_OUT, d), lambda i: (i, 0)),
            pl.BlockSpec((TM_OUT, d), lambda i: (i, 0)),
        ],
        out_shape=[
            jax.ShapeDtypeStruct((m, d), F32),
            jax.ShapeDtypeStruct((m, d), BF16),
        ],
        compiler_params=_params(("arbitrary",)),
        name="out_proj",
    )(mixed_sb, mixed_fox, w_out, x2, g_mlp)


def _mlp_kernel(h_ref, x1_ref, wu_ref, wd_ref, g_ref, o_ref):
    f = pl.program_id(1)

    @pl.when(f == 0)
    def _():
        o_ref[...] = x1_ref[...]

    u = jnp.maximum(_dot(h_ref[...], wu_ref[...]), 0.0)
    o_ref[...] += _dot((u * u).astype(BF16), wd_ref[...])

    @pl.when(f == pl.num_programs(1) - 1)
    def _():
        x2 = o_ref[...]
        var = jnp.mean(x2 * x2, axis=-1, keepdims=True)
        o_ref[...] = x2 * lax.rsqrt(var + EPS) * g_ref[...]


def _mlp(h2, x1, w_up, w_down, g_final):
    m, d = x1.shape
    dff = w_up.shape[1]
    return pl.pallas_call(
        _mlp_kernel,
        grid=(m // TM_MLP, dff // TF_MLP),
        in_specs=[
            pl.BlockSpec((TM_MLP, d), lambda i, f: (i, 0)),
            pl.BlockSpec((TM_MLP, d), lambda i, f: (i, 0)),
            pl.BlockSpec((d, TF_MLP), lambda i, f: (0, f)),
            pl.BlockSpec((TF_MLP, d), lambda i, f: (f, 0)),
            pl.BlockSpec((1, d), lambda i, f: (0, 0)),
        ],
        out_specs=pl.BlockSpec((TM_MLP, d), lambda i, f: (i, 0)),
        out_shape=jax.ShapeDtypeStruct((m, d), F32),
        compiler_params=_params(("arbitrary", "arbitrary")),
        name="mlp",
    )(h2, x1, w_up, w_down, g_final)


def kernel(x, g_attn, w_in, b_f, g_out_sb, g_out_fox, w_out, g_mlp, w_up, w_down, g_final):
    b, s, d = x.shape
    n_qkv = 6 * GROUP_W
    assert s % TQ_S == 0 and s % TK_F == 0 and (b * s) % TM_IN == 0
    x2 = x.reshape(b * s, d)

    idx = jnp.arange(TK_S)
    u_excl = (idx[None, :] > idx[:, None]).astype(BF16)
    u2 = jnp.concatenate([u_excl, u_excl], axis=1)
    cidx = jnp.arange(CS_BLK)
    tri_incl = (cidx[None, :] <= cidx[:, None]).astype(BF16)

    gw = GROUP_W
    for l in range(g_attn.shape[0]):
        w_t = jnp.swapaxes(w_in[l], 0, 1)
        w_qkv_t = w_t.astype(BF16)
        w_f_t = jnp.pad(w_t[n_qkv:], ((0, LANES - N_HEADS), (0, 0))).astype(BF16)
        b_pad = jnp.pad(b_f[l], (0, LANES - N_HEADS)).reshape(1, LANES)

        qk, v, f_logit = _in_proj(x2, g_attn[l].reshape(1, d), w_qkv_t, w_f_t)
        qk = qk.reshape(b, s, N_QK_GROUPS * gw)
        v_t = jnp.transpose(v.reshape(b, s // VT_TK, VT_TK, 2 * N_HEADS, HEAD_DIM), (0, 3, 1, 4, 2))
        c = _forget_cs(f_logit.reshape(b, s, LANES), b_pad, tri_incl)

        mixed_sb = _sb_attn(qk, v_t, g_out_sb[l].reshape(gw, 1), u2)
        mixed_fox = _fox_attn(qk, v_t, c, g_out_fox[l].reshape(gw, 1))

        x1, h2 = _out_proj(mixed_sb.reshape(b * s, GROUP_W), mixed_fox.reshape(b * s, GROUP_W),
                           w_out[l].astype(BF16), x2, g_mlp[l].reshape(1, d))
        assert g_attn.shape[0] == 1
        x2 = _mlp(h2, x1, w_up[l].astype(BF16), w_down[l].astype(BF16), g_final.reshape(1, d))
    return x2.reshape(b, s, d)
```

```python
import functools

import jax
import jax.numpy as jnp
from jax import lax
from jax.experimental import pallas as pl
from jax.experimental.pallas import tpu as pltpu

F32 = jnp.float32
BF16 = jnp.bfloat16

HEAD_DIM = 128
N_HEADS = 8
GROUP_W = N_HEADS * HEAD_DIM
EPS = 1e-6
SCALE = HEAD_DIM ** -0.5
LOG2E = 1.4426950408889634
SB_DEAD_LOG2 = -152.0
FOX_DEAD_LOG2 = -152.0
NORM_SLACK = 1.01
M_INIT = -1e30
KN_BLK = 512
LANES = 128

VMEM_LIMIT = 56 * 1024 * 1024

TM_IN = 1024
TN_IN = 1024
TQ_S = 2048
TK_S = 256
TQ_F = 512
TK_F = 512
VT_TK = 256
CS_BLK = 256
TM_OUT = 512
TM_MLP = 512
TF_MLP = 1024


def _params(sem):
    return pltpu.CompilerParams(dimension_semantics=sem, vmem_limit_bytes=VMEM_LIMIT)


def _dot(a, b):
    return jnp.dot(a, b, preferred_element_type=F32)


def _dot_nt(a, b):
    return lax.dot_general(a, b, (((1,), (1,)), ((), ())), preferred_element_type=F32)


def _log_sigmoid(x):
    return jnp.minimum(x, 0.0) - jnp.log(1.0 + jnp.exp(-jnp.abs(x)))


N_QK_GROUPS = 4


def _in_proj_kernel(x_ref, g_ref, w_ref, wf_ref, qk_ref, v_ref, f_ref, h_ref):
    j = pl.program_id(1)

    @pl.when(j == 0)
    def _():
        x = x_ref[...]
        var = jnp.mean(x * x, axis=-1, keepdims=True)
        h = (x * lax.rsqrt(var + EPS) * g_ref[...]).astype(BF16)
        h_ref[...] = h
        f_ref[...] = _dot_nt(h, wf_ref[...])

    scale = jnp.where((j == 0) | (j == 2), SCALE * LOG2E, 1.0).astype(F32)
    out = (_dot_nt(h_ref[...], w_ref[...]) * scale).astype(BF16)

    @pl.when(j < N_QK_GROUPS)
    def _():
        qk_ref[...] = out

    @pl.when(j >= N_QK_GROUPS)
    def _():
        v_ref[...] = out


def _w_group(j):
    return jnp.where(j < 2, j, jnp.where(j < 4, j + 1, jnp.where(j == 4, 2, 5)))


def _in_proj(x2, g, w_qkv_t, w_f_t):
    m, d = x2.shape
    n = 6 * GROUP_W
    assert TN_IN == GROUP_W and w_qkv_t.shape[0] >= n
    return pl.pallas_call(
        _in_proj_kernel,
        grid=(m // TM_IN, n // TN_IN),
        in_specs=[
            pl.BlockSpec((TM_IN, d), lambda i, j: (i, 0)),
            pl.BlockSpec((1, d), lambda i, j: (0, 0)),
            pl.BlockSpec((TN_IN, d), lambda i, j: (_w_group(j), 0)),
            pl.BlockSpec((LANES, d), lambda i, j: (0, 0)),
        ],
        out_specs=[
            pl.BlockSpec((TM_IN, TN_IN), lambda i, j: (i, jnp.minimum(j, N_QK_GROUPS - 1))),
            pl.BlockSpec((TM_IN, TN_IN), lambda i, j: (i, jnp.maximum(j - N_QK_GROUPS, 0))),
            pl.BlockSpec((TM_IN, LANES), lambda i, j: (i, 0)),
        ],
        out_shape=[
            jax.ShapeDtypeStruct((m, N_QK_GROUPS * GROUP_W), BF16),
            jax.ShapeDtypeStruct((m, 2 * GROUP_W), BF16),
            jax.ShapeDtypeStruct((m, LANES), F32),
        ],
        scratch_shapes=[pltpu.VMEM((TM_IN, d), BF16)],
        compiler_params=_params(("arbitrary", "arbitrary")),
        name="in_proj",
    )(x2, g, w_qkv_t, w_f_t)


def _split3(x):
    p1 = x.astype(BF16)
    r1 = x - p1.astype(F32)
    p2 = r1.astype(BF16)
    p3 = (r1 - p2.astype(F32)).astype(BF16)
    return p1, p2, p3


def _forget_cs_kernel(f_ref, b_ref, tri_ref, selq_ref, selk_ref, oneq_ref, onek_ref,
                      c_ref, extq_ref, extk_ref):
    n_blk = f_ref.shape[0] // CS_BLK
    tri = tri_ref[...]

    def body(i, carry):
        r0 = pl.multiple_of(i * CS_BLK, CS_BLK)
        lf = _log_sigmoid(f_ref[pl.ds(r0, CS_BLK), :] + b_ref[...])
        p1, p2, p3 = _split3(lf)
        c = _dot(tri, p1) + _dot(tri, p2) + _dot(tri, p3) + carry
        c2 = c * LOG2E
        c_ref[pl.ds(r0, CS_BLK), :] = c2
        lane = lax.broadcasted_iota(jnp.int32, c2.shape, 1)
        packed = jnp.zeros_like(c2)
        for k, piece in enumerate(_split3(c2)):
            piece = jnp.where(lane < N_HEADS, piece.astype(F32), 0.0)
            packed = packed + (piece if k == 0 else pltpu.roll(piece, N_HEADS * k, axis=1))
        packed = packed.astype(BF16)
        extq_ref[pl.ds(r0, CS_BLK), :] = (_dot(packed, selq_ref[...]) + oneq_ref[...]).astype(BF16)
        extk_ref[pl.ds(r0, CS_BLK), :] = (_dot(packed, selk_ref[...]) + onek_ref[...]).astype(BF16)
        return c[CS_BLK - 1:CS_BLK, :]

    lax.fori_loop(0, n_blk, body, jnp.zeros((1, LANES), F32))


def _aug_constants():
    row = jnp.arange(LANES)
    col = jnp.arange(GROUP_W)
    piece, head = row // N_HEADS, row % N_HEADS
    blk, lane = col // HEAD_DIM, col % HEAD_DIM
    same = (head[:, None] == blk[None, :]) & (piece[:, None] < 3)
    selq = jnp.where(same & (lane[None, :] == 3 + piece[:, None]), 1.0, 0.0).astype(BF16)
    selk = jnp.where(same & (lane[None, :] == piece[:, None]), -1.0, 0.0).astype(BF16)
    oneq = jnp.where(lane < 3, 1.0, 0.0).astype(F32).reshape(1, GROUP_W)
    onek = jnp.where((lane >= 3) & (lane < 6), 1.0, 0.0).astype(F32).reshape(1, GROUP_W)
    return selq, selk, oneq, onek


def _forget_cs(f_logit, b_pad, tri_incl):
    b, s, _ = f_logit.shape
    const = lambda shape: pl.BlockSpec(shape, lambda i: (0, 0))
    return pl.pallas_call(
        _forget_cs_kernel,
        grid=(b,),
        in_specs=[
            pl.BlockSpec((None, s, LANES), lambda i: (i, 0, 0)),
            const((1, LANES)),
            const((CS_BLK, CS_BLK)),
            const((LANES, GROUP_W)),
            const((LANES, GROUP_W)),
            const((1, GROUP_W)),
            const((1, GROUP_W)),
        ],
        out_specs=[
            pl.BlockSpec((None, s, LANES), lambda i: (i, 0, 0)),
            pl.BlockSpec((None, s, GROUP_W), lambda i: (i, 0, 0)),
            pl.BlockSpec((None, s, GROUP_W), lambda i: (i, 0, 0)),
        ],
        out_shape=[
            jax.ShapeDtypeStruct((b, s, LANES), F32),
            jax.ShapeDtypeStruct((b, s, GROUP_W), BF16),
            jax.ShapeDtypeStruct((b, s, GROUP_W), BF16),
        ],
        compiler_params=_params(("arbitrary",)),
        name="forget_cs",
    )(f_logit, b_pad, tri_incl, *_aug_constants())


def _head_rmsnorm(o, g):
    return o * lax.rsqrt(jnp.mean(o * o, axis=-1, keepdims=True) + EPS) * g


def _col_reduce(x, pair, full):
    while x.shape[0] > 8:
        half = x.shape[0] // 2
        x = pair(x[:half], x[half:])
    return full(x, axis=0, keepdims=True)


def _store_head_out(o_ref, acc_t, g_col):
    o = acc_t * lax.rsqrt(jnp.mean(acc_t * acc_t, axis=0, keepdims=True) + EPS) * g_col
    o_ref[...] = jnp.transpose(o).astype(o_ref.dtype)


def _sb_kernel(q_ref, k_ref, vt_ref, g_ref, u2_ref, o_ref):
    qi = pl.program_id(2)
    u2 = u2_ref[...]
    n_sub = TQ_S // TK_S
    key = lax.broadcasted_iota(jnp.int32, (TK_S, TK_S), 0)
    qry = lax.broadcasted_iota(jnp.int32, (TK_S, TK_S), 1)
    causal = key < qry

    def stage_q(q, kb):
        ks = pl.multiple_of(kb * TK_S, TK_S)
        return _dot_nt(k_ref[pl.ds(ks, TK_S), :], q)

    def stage_e(z, diagonal):
        zn = jnp.minimum(z, 0.0)
        zp = zn - z
        sp = jnp.log2(1.0 + jnp.exp2(zn + zp))
        lb = zn - sp
        lk = zp - sp
        if diagonal:
            lk = jnp.where(causal, lk, 0.0)
        hi = lk.astype(BF16)
        lo = (lk - hi.astype(F32)).astype(BF16)
        return lb, jnp.concatenate([hi, lo], axis=0), _col_reduce(lk, jnp.add, jnp.sum)

    def stage_c(hilo):
        return _dot(u2, hilo)

    def stage_x(lb, rest, carry, diagonal):
        if diagonal:
            a = jnp.where(causal, jnp.exp2(lb + rest), 0.0)
        else:
            a = jnp.exp2(lb + (rest + carry))
        return a.astype(BF16)

    def stage_v(kb, a, gate=None):
        vt = vt_ref[kb]
        if gate is not None:
            vt = vt * gate
        return _dot(vt, a)

    items = []
    for sub in range(n_sub):
        kd = qi * n_sub + sub
        q = q_ref[sub * TK_S:(sub + 1) * TK_S, :]
        items.append(dict(q=q, kb=kd, diagonal=True, gate=None))
        items.append(dict(q=q, kb=jnp.maximum(kd - 1, 0), diagonal=False,
                          gate=jnp.where(kd > 0, 1.0, 0.0).astype(BF16)))
    n_items = len(items)
    for t in range(n_items + 4):
        if 4 <= t:
            it = items[t - 4]
            it["pv"] = stage_v(it["kb"], it["a"], it["gate"])
        if 2 <= t < n_items + 2:
            it = items[t - 2]
            it["rest"] = stage_c(it["hilo"])
        if t < n_items:
            it = items[t]
            it["z"] = stage_q(it["q"], it["kb"])
        if 3 <= t < n_items + 3:
            it = items[t - 3]
            carry_in = None if it["diagonal"] else items[t - 4]["colsum"]
            it["a"] = stage_x(it["lb"], it["rest"], carry_in, it["diagonal"])
        if 1 <= t < n_items + 1:
            it = items[t - 1]
            it["lb"], it["hilo"], it["colsum"] = stage_e(it["z"], it["diagonal"])

    qs = [items[2 * r]["q"] for r in range(n_sub)]
    state = []
    for r in range(n_sub):
        d, p = items[2 * r], items[2 * r + 1]
        state += [d["colsum"] + p["colsum"], d["pv"] + p["pv"]]

    kd_last = qi * n_sub + n_sub - 1

    def cond(st):
        worst = st[1]
        for r in range(1, n_sub):
            worst = jnp.maximum(worst, st[1 + 2 * r])
        return jnp.logical_and(kd_last - 2 - st[0] >= 0, jnp.max(worst) > SB_DEAD_LOG2)

    def body(st):
        j, out = st[0], []
        for r in range(n_sub):
            kb = qi * n_sub + r - 2 - j
            gate = jnp.where(kb >= 0, 1.0, 0.0).astype(BF16)
            kb = jnp.maximum(kb, 0)
            lb, hilo, colsum = stage_e(stage_q(qs[r], kb), False)
            a = stage_x(lb, stage_c(hilo), st[1 + 2 * r], False)
            out += [st[1 + 2 * r] + colsum, st[2 + 2 * r] + stage_v(kb, a, gate)]
        return (j + 1, *out)

    st = lax.while_loop(cond, body, (0, *state))
    _store_head_out(o_ref, jnp.concatenate([st[2 + 2 * r] for r in range(n_sub)], axis=1), g_ref[...])


def _sb_attn(qk, v_t, g_col, u2):
    b, s, _ = qk.shape
    assert VT_TK == TK_S and TQ_S % TK_S == 0
    return pl.pallas_call(
        _sb_kernel,
        grid=(b, N_HEADS, s // TQ_S),
        in_specs=[
            pl.BlockSpec((None, TQ_S, HEAD_DIM), lambda bi, h, qi: (bi, qi, h)),
            pl.BlockSpec((None, s, HEAD_DIM), lambda bi, h, qi: (bi, 0, N_HEADS + h)),
            pl.BlockSpec((None, None, s // VT_TK, HEAD_DIM, VT_TK), lambda bi, h, qi: (bi, h, 0, 0, 0)),
            pl.BlockSpec((HEAD_DIM, 1), lambda bi, h, qi: (h, 0)),
            pl.BlockSpec((TK_S, 2 * TK_S), lambda bi, h, qi: (0, 0)),
        ],
        out_specs=pl.BlockSpec((None, TQ_S, HEAD_DIM), lambda bi, h, qi: (bi, qi, h)),
        out_shape=jax.ShapeDtypeStruct((b, s, GROUP_W), BF16),
        compiler_params=_params(("arbitrary", "arbitrary", "arbitrary")),
        name="sb_attn",
    )(qk, qk, v_t, g_col, u2)


def _fox_kernel(q_ref, k_ref, extq_ref, extk_ref, vt_ref, cq_ref, clast_ref, g_ref, o_ref,
                kn_ref, z_ref, p_ref, m_ref, l_ref, a_ref, acc_ref):
    bi, h, qi = pl.program_id(0), pl.program_id(1), pl.program_id(2)
    n_full = (qi * TQ_F) // TK_F

    @pl.when(qi == 0)
    def _():
        def chunk(i, best):
            r0 = pl.multiple_of(i * KN_BLK, KN_BLK)
            kf = k_ref[pl.ds(r0, KN_BLK), :].astype(F32)
            n2 = jnp.sum(kf * kf, axis=-1, keepdims=True)
            return jnp.maximum(best, jnp.max(n2, axis=0, keepdims=True))

        best = lax.fori_loop(0, k_ref.shape[0] // KN_BLK, chunk, jnp.zeros((1, 1), F32))
        kn_ref[...] = jnp.broadcast_to(jnp.sqrt(best), kn_ref.shape)

    q = q_ref[...]
    q_aug = jnp.concatenate([q, extq_ref[...]], axis=1)

    qf = q.astype(F32)
    q2_up = (qf * qf * (1.0 + 2.0 ** -7)).astype(BF16)
    qn2 = _dot_nt(jnp.ones((8, HEAD_DIM), BF16), q2_up)[:1]
    ub = jnp.sqrt(qn2) * kn_ref[:, :1] * NORM_SLACK + cq_ref[...]

    vt_per_tile = TK_F // VT_TK
    last_tile = vt_ref.shape[0] // vt_per_tile - 1

    def kb_of(i):
        return jnp.clip(n_full - i, 0, last_tile)

    def qk_t(i):
        ks = pl.multiple_of(kb_of(i) * TK_F, TK_F)
        k_aug = jnp.concatenate([k_ref[pl.ds(ks, TK_F), :], extk_ref[pl.ds(ks, TK_F), :]], axis=1)
        return _dot_nt(k_aug, q_aug)

    def pv_t(i, slot):
        kb = kb_of(i)
        out = None
        for r in range(vt_per_tile):
            part = _dot(vt_ref[kb * vt_per_tile + r], p_ref[slot, r * VT_TK:(r + 1) * VT_TK, :])
            out = part if out is None else out + part
        return out

    st = qk_t(0)
    z_ref[1] = qk_t(1)
    key = n_full * TK_F + lax.broadcasted_iota(jnp.int32, (TK_F, TQ_F), 0)
    qry = qi * TQ_F + lax.broadcasted_iota(jnp.int32, (TK_F, TQ_F), 1)
    st = jnp.where(key <= qry, st, -jnp.inf)
    m0 = _col_reduce(st, jnp.maximum, jnp.max)
    p0 = jnp.exp2(st - m0)
    p_ref[0] = p0.astype(BF16)
    m_ref[...] = m0
    l_ref[...] = _col_reduce(p0, jnp.add, jnp.sum)
    a_ref[...] = jnp.ones_like(m0)
    acc_ref[...] = jnp.zeros_like(acc_ref)

    gap = jnp.max(ub - m0)
    n_tiles = 1
    for i in range(clast_ref.shape[2]):
        alive = jnp.logical_and(i < n_full, gap - clast_ref[bi, h, i] > FOX_DEAD_LOG2)
        n_tiles = n_tiles + alive.astype(jnp.int32)

    def step(i, slot, active=None):
        z_ref[1 - slot] = qk_t(i + 1)
        acc_ref[...] = a_ref[...] * acc_ref[...] + pv_t(i - 1, 1 - slot)
        z = z_ref[slot]
        if active is not None:
            z = jnp.where(active, z, -jnp.inf)
        m = m_ref[...]
        m_new = jnp.maximum(m, _col_reduce(z, jnp.maximum, jnp.max))
        alpha = jnp.exp2(m - m_new)
        p = jnp.exp2(z_ref[slot] - m_new) if active is None else jnp.exp2(z - m_new)
        p_ref[slot] = p.astype(BF16)
        l_ref[...] = alpha * l_ref[...] + _col_reduce(p, jnp.add, jnp.sum)
        m_ref[...] = m_new
        a_ref[...] = alpha

    def body(j, _):
        i = 1 + 2 * j
        step(i, 1)
        step(i + 1, 0, active=i + 1 < n_tiles)
        return 0

    n_pairs = n_tiles // 2
    lax.fori_loop(0, n_pairs, body, 0)
    acc = a_ref[...] * acc_ref[...] + pv_t(2 * n_pairs, 0)
    _store_head_out(o_ref, acc / l_ref[...], g_ref[...])


def _fox_attn(qk, v_t, c, ext_q, ext_k, g_col):
    b, s, _ = qk.shape
    base = 2 * N_HEADS
    n_kt = s // TK_F
    assert TK_F % TQ_F == 0 and TK_F % VT_TK == 0
    c_hs = jnp.transpose(c[:, :, :N_HEADS], (0, 2, 1))
    c_q = c_hs.reshape(b, N_HEADS, s // TQ_F, 1, TQ_F)
    c_last = c_hs.reshape(b, N_HEADS, n_kt, TK_F)[..., TK_F - 1]
    return pl.pallas_call(
        _fox_kernel,
        grid=(b, N_HEADS, s // TQ_F),
        in_specs=[
            pl.BlockSpec((None, TQ_F, HEAD_DIM), lambda bi, h, qi: (bi, qi, base + h)),
            pl.BlockSpec((None, s, HEAD_DIM), lambda bi, h, qi: (bi, 0, base + N_HEADS + h)),
            pl.BlockSpec((None, TQ_F, HEAD_DIM), lambda bi, h, qi: (bi, qi, h)),
            pl.BlockSpec((None, s, HEAD_DIM), lambda bi, h, qi: (bi, 0, h)),
            pl.BlockSpec((None, None, s // VT_TK, HEAD_DIM, VT_TK),
                         lambda bi, h, qi: (bi, N_HEADS + h, 0, 0, 0)),
            pl.BlockSpec((None, None, None, 1, TQ_F), lambda bi, h, qi: (bi, h, qi, 0, 0)),
            pl.BlockSpec(memory_space=pltpu.SMEM),
            pl.BlockSpec((HEAD_DIM, 1), lambda bi, h, qi: (h, 0)),
        ],
        out_specs=pl.BlockSpec((None, TQ_F, HEAD_DIM), lambda bi, h, qi: (bi, qi, h)),
        scratch_shapes=[
            pltpu.VMEM((1, LANES), F32),
            pltpu.VMEM((2, TK_F, TQ_F), F32),
            pltpu.VMEM((2, TK_F, TQ_F), BF16),
            pltpu.VMEM((1, TQ_F), F32),
            pltpu.VMEM((1, TQ_F), F32),
            pltpu.VMEM((1, TQ_F), F32),
            pltpu.VMEM((HEAD_DIM, TQ_F), F32),
        ],
        out_shape=jax.ShapeDtypeStruct((b, s, GROUP_W), BF16),
        compiler_params=_params(("arbitrary", "arbitrary", "arbitrary")),
        name="fox_attn",
    )(qk, qk, ext_q, ext_k, v_t, c_q, c_last, g_col)


def _out_proj_kernel(ms_ref, mf_ref, w_ref, x_ref, g_ref, x1_ref, h2_ref):
    acc = _dot(ms_ref[...], w_ref[:GROUP_W, :]) + _dot(mf_ref[...], w_ref[GROUP_W:, :])
    x1 = x_ref[...] + acc
    x1_ref[...] = x1
    var = jnp.mean(x1 * x1, axis=-1, keepdims=True)
    h2_ref[...] = (x1 * lax.rsqrt(var + EPS) * g_ref[...]).astype(BF16)


def _out_proj(mixed_sb, mixed_fox, w_out, x2, g_mlp):
    m, d = x2.shape
    return pl.pallas_call(
        _out_proj_kernel,
        grid=(m // TM_OUT,),
        in_specs=[
            pl.BlockSpec((TM_OUT, GROUP_W), lambda i: (i, 0)),
            pl.BlockSpec((TM_OUT, GROUP_W), lambda i: (i, 0)),
            pl.BlockSpec((2 * GROUP_W, d), lambda i: (0, 0)),
            pl.BlockSpec((TM_OUT, d), lambda i: (i, 0)),
            pl.BlockSpec((1, d), lambda i: (0, 0)),
        ],
        out_specs=[
            pl.BlockSpec((TM_OUT, d), lambda i: (i, 0)),
            pl.BlockSpec((TM_OUT, d), lambda i: (i, 0)),
        ],
        out_shape=[
            jax.ShapeDtypeStruct((m, d), F32),
            jax.ShapeDtypeStruct((m, d), BF16),
        ],
        compiler_params=_params(("arbitrary",)),
        name="out_proj",
    )(mixed_sb, mixed_fox, w_out, x2, g_mlp)


def _mlp_kernel(h_ref, x1_ref, wu_ref, wd_ref, g_ref, o_ref):
    f = pl.program_id(1)

    @pl.when(f == 0)
    def _():
        o_ref[...] = x1_ref[...]

    u = jnp.maximum(_dot(h_ref[...], wu_ref[...]), 0.0)
    o_ref[...] += _dot((u * u).astype(BF16), wd_ref[...])

    @pl.when(f == pl.num_programs(1) - 1)
    def _():
        x2 = o_ref[...]
        var = jnp.mean(x2 * x2, axis=-1, keepdims=True)
        o_ref[...] = x2 * lax.rsqrt(var + EPS) * g_ref[...]


def _mlp(h2, x1, w_up, w_down, g_final):
    m, d = x1.shape
    dff = w_up.shape[1]
    return pl.pallas_call(
        _mlp_kernel,
        grid=(m // TM_MLP, dff // TF_MLP),
        in_specs=[
            pl.BlockSpec((TM_MLP, d), lambda i, f: (i, 0)),
            pl.BlockSpec((TM_MLP, d), lambda i, f: (i, 0)),
            pl.BlockSpec((d, TF_MLP), lambda i, f: (0, f)),
            pl.BlockSpec((TF_MLP, d), lambda i, f: (f, 0)),
            pl.BlockSpec((1, d), lambda i, f: (0, 0)),
        ],
        out_specs=pl.BlockSpec((TM_MLP, d), lambda i, f: (i, 0)),
        out_shape=jax.ShapeDtypeStruct((m, d), F32),
        compiler_params=_params(("arbitrary", "arbitrary")),
        name="mlp",
    )(h2, x1, w_up, w_down, g_final)


def kernel(x, g_attn, w_in, b_f, g_out_sb, g_out_fox, w_out, g_mlp, w_up, w_down, g_final):
    b, s, d = x.shape
    n_qkv = 6 * GROUP_W
    assert s % TQ_S == 0 and s % TK_F == 0 and (b * s) % TM_IN == 0
    x2 = x.reshape(b * s, d)

    idx = jnp.arange(TK_S)
    u_excl = (idx[None, :] > idx[:, None]).astype(BF16)
    u2 = jnp.concatenate([u_excl, u_excl], axis=1)
    cidx = jnp.arange(CS_BLK)
    tri_incl = (cidx[None, :] <= cidx[:, None]).astype(BF16)

    gw = GROUP_W
    for l in range(g_attn.shape[0]):
        w_t = jnp.swapaxes(w_in[l], 0, 1)
        w_qkv_t = w_t.astype(BF16)
        w_f_t = jnp.pad(w_t[n_qkv:], ((0, LANES - N_HEADS), (0, 0))).astype(BF16)
        b_pad = jnp.pad(b_f[l], (0, LANES - N_HEADS)).reshape(1, LANES)

        qk, v, f_logit = _in_proj(x2, g_attn[l].reshape(1, d), w_qkv_t, w_f_t)
        qk = qk.reshape(b, s, N_QK_GROUPS * gw)
        v_t = jnp.transpose(v.reshape(b, s // VT_TK, VT_TK, 2 * N_HEADS, HEAD_DIM), (0, 3, 1, 4, 2))
        c, ext_q, ext_k = _forget_cs(f_logit.reshape(b, s, LANES), b_pad, tri_incl)

        mixed_sb = _sb_attn(qk, v_t, g_out_sb[l].reshape(gw, 1), u2)
        mixed_fox = _fox_attn(qk, v_t, c, ext_q, ext_k, g_out_fox[l].reshape(gw, 1))

        x1, h2 = _out_proj(mixed_sb.reshape(b * s, GROUP_W), mixed_fox.reshape(b * s, GROUP_W),
                           w_out[l].astype(BF16), x2, g_mlp[l].reshape(1, d))
        assert g_attn.shape[0] == 1
        x2 = _mlp(h2, x1, w_up[l].astype(BF16), w_down[l].astype(BF16), g_final.reshape(1, d))
    return x2.reshape(b, s, d)
```

```python
import functools

import jax
import jax.numpy as jnp
from jax import lax
from jax.experimental import pallas as pl
from jax.experimental.pallas import tpu as pltpu

F32 = jnp.float32
BF16 = jnp.bfloat16

HEAD_DIM = 128
N_HEADS = 8
GROUP_W = N_HEADS * HEAD_DIM
EPS = 1e-6
SCALE = HEAD_DIM ** -0.5
LOG2E = 1.4426950408889634
SB_DEAD_LOG2 = -152.0
FOX_DEAD_LOG2 = -152.0
NORM_SLACK = 1.01
M_INIT = -1e30
KN_BLK = 512
LANES = 128

VMEM_LIMIT = 56 * 1024 * 1024

TM_IN = 1024
TN_IN = 1024
TQ_S = 2048
TK_S = 256
TQ_F = 512
TK_F = 512
CS_BLK = 256
TM_OUT = 512
TM_MLP = 512
TF_MLP = 1024


def _params(sem):
    return pltpu.CompilerParams(dimension_semantics=sem, vmem_limit_bytes=VMEM_LIMIT)


def _dot(a, b):
    return jnp.dot(a, b, preferred_element_type=F32)


def _dot_nt(a, b):
    return lax.dot_general(a, b, (((1,), (1,)), ((), ())), preferred_element_type=F32)


def _log_sigmoid(x):
    return jnp.minimum(x, 0.0) - jnp.log(1.0 + jnp.exp(-jnp.abs(x)))


N_QK_GROUPS = 4


def _in_proj_kernel(x_ref, g_ref, w_ref, wf_ref, qk_ref, vt_ref, f_ref, h_ref):
    j = pl.program_id(1)

    @pl.when(j == 0)
    def _():
        x = x_ref[...]
        var = jnp.mean(x * x, axis=-1, keepdims=True)
        h = (x * lax.rsqrt(var + EPS) * g_ref[...]).astype(BF16)
        h_ref[...] = h
        f_ref[...] = _dot_nt(h, wf_ref[...])

    @pl.when(j < N_QK_GROUPS)
    def _():
        scale = jnp.where((j == 0) | (j == 2), SCALE * LOG2E, 1.0).astype(F32)
        qk_ref[...] = (_dot_nt(h_ref[...], w_ref[...]) * scale).astype(BF16)

    @pl.when(j >= N_QK_GROUPS)
    def _():
        vt_ref[...] = _dot_nt(w_ref[...], h_ref[...]).astype(BF16)


def _w_group(j):
    return jnp.where(j < 2, j, jnp.where(j < 4, j + 1, jnp.where(j == 4, 2, 5)))


def _in_proj(x2, g, w_qkv_t, w_f_t):
    m, d = x2.shape
    n = 6 * GROUP_W
    assert TN_IN == GROUP_W and w_qkv_t.shape[0] >= n
    return pl.pallas_call(
        _in_proj_kernel,
        grid=(m // TM_IN, n // TN_IN),
        in_specs=[
            pl.BlockSpec((TM_IN, d), lambda i, j: (i, 0)),
            pl.BlockSpec((1, d), lambda i, j: (0, 0)),
            pl.BlockSpec((TN_IN, d), lambda i, j: (_w_group(j), 0)),
            pl.BlockSpec((LANES, d), lambda i, j: (0, 0)),
        ],
        out_specs=[
            pl.BlockSpec((TM_IN, TN_IN), lambda i, j: (i, jnp.minimum(j, N_QK_GROUPS - 1))),
            pl.BlockSpec((TN_IN, TM_IN), lambda i, j: (jnp.maximum(j - N_QK_GROUPS, 0), i)),
            pl.BlockSpec((TM_IN, LANES), lambda i, j: (i, 0)),
        ],
        out_shape=[
            jax.ShapeDtypeStruct((m, N_QK_GROUPS * GROUP_W), BF16),
            jax.ShapeDtypeStruct((2 * GROUP_W, m), BF16),
            jax.ShapeDtypeStruct((m, LANES), F32),
        ],
        scratch_shapes=[pltpu.VMEM((TM_IN, d), BF16)],
        compiler_params=_params(("arbitrary", "arbitrary")),
        name="in_proj",
    )(x2, g, w_qkv_t, w_f_t)


def _split3(x):
    p1 = x.astype(BF16)
    r1 = x - p1.astype(F32)
    p2 = r1.astype(BF16)
    p3 = (r1 - p2.astype(F32)).astype(BF16)
    return p1, p2, p3


def _forget_cs_kernel(f_ref, b_ref, tri_ref, selq_ref, selk_ref, oneq_ref, onek_ref,
                      c_ref, extq_ref, extk_ref):
    n_blk = f_ref.shape[0] // CS_BLK
    tri = tri_ref[...]

    def body(i, carry):
        r0 = pl.multiple_of(i * CS_BLK, CS_BLK)
        lf = _log_sigmoid(f_ref[pl.ds(r0, CS_BLK), :] + b_ref[...])
        p1, p2, p3 = _split3(lf)
        c = _dot(tri, p1) + _dot(tri, p2) + _dot(tri, p3) + carry
        c2 = c * LOG2E
        c_ref[pl.ds(r0, CS_BLK), :] = c2
        lane = lax.broadcasted_iota(jnp.int32, c2.shape, 1)
        packed = jnp.zeros_like(c2)
        for k, piece in enumerate(_split3(c2)):
            piece = jnp.where(lane < N_HEADS, piece.astype(F32), 0.0)
            packed = packed + (piece if k == 0 else pltpu.roll(piece, N_HEADS * k, axis=1))
        packed = packed.astype(BF16)
        extq_ref[pl.ds(r0, CS_BLK), :] = (_dot(packed, selq_ref[...]) + oneq_ref[...]).astype(BF16)
        extk_ref[pl.ds(r0, CS_BLK), :] = (_dot(packed, selk_ref[...]) + onek_ref[...]).astype(BF16)
        return c[CS_BLK - 1:CS_BLK, :]

    lax.fori_loop(0, n_blk, body, jnp.zeros((1, LANES), F32))


def _aug_constants():
    row = jnp.arange(LANES)
    col = jnp.arange(GROUP_W)
    piece, head = row // N_HEADS, row % N_HEADS
    blk, lane = col // HEAD_DIM, col % HEAD_DIM
    same = (head[:, None] == blk[None, :]) & (piece[:, None] < 3)
    selq = jnp.where(same & (lane[None, :] == 3 + piece[:, None]), 1.0, 0.0).astype(BF16)
    selk = jnp.where(same & (lane[None, :] == piece[:, None]), -1.0, 0.0).astype(BF16)
    oneq = jnp.where(lane < 3, 1.0, 0.0).astype(F32).reshape(1, GROUP_W)
    onek = jnp.where((lane >= 3) & (lane < 6), 1.0, 0.0).astype(F32).reshape(1, GROUP_W)
    return selq, selk, oneq, onek


def _forget_cs(f_logit, b_pad, tri_incl):
    b, s, _ = f_logit.shape
    const = lambda shape: pl.BlockSpec(shape, lambda i: (0, 0))
    return pl.pallas_call(
        _forget_cs_kernel,
        grid=(b,),
        in_specs=[
            pl.BlockSpec((None, s, LANES), lambda i: (i, 0, 0)),
            const((1, LANES)),
            const((CS_BLK, CS_BLK)),
            const((LANES, GROUP_W)),
            const((LANES, GROUP_W)),
            const((1, GROUP_W)),
            const((1, GROUP_W)),
        ],
        out_specs=[
            pl.BlockSpec((None, s, LANES), lambda i: (i, 0, 0)),
            pl.BlockSpec((None, s, GROUP_W), lambda i: (i, 0, 0)),
            pl.BlockSpec((None, s, GROUP_W), lambda i: (i, 0, 0)),
        ],
        out_shape=[
            jax.ShapeDtypeStruct((b, s, LANES), F32),
            jax.ShapeDtypeStruct((b, s, GROUP_W), BF16),
            jax.ShapeDtypeStruct((b, s, GROUP_W), BF16),
        ],
        compiler_params=_params(("arbitrary",)),
        name="forget_cs",
    )(f_logit, b_pad, tri_incl, *_aug_constants())


def _head_rmsnorm(o, g):
    return o * lax.rsqrt(jnp.mean(o * o, axis=-1, keepdims=True) + EPS) * g


def _col_reduce(x, pair, full):
    while x.shape[0] > 8:
        half = x.shape[0] // 2
        x = pair(x[:half], x[half:])
    return full(x, axis=0, keepdims=True)


def _store_head_out(o_ref, acc_t, g_col):
    o = acc_t * lax.rsqrt(jnp.mean(acc_t * acc_t, axis=0, keepdims=True) + EPS) * g_col
    o_ref[...] = jnp.transpose(o).astype(o_ref.dtype)


def _cast_rows_spec(w, n_steps, step_of):
    rows, cols = w.shape
    assert rows % n_steps == 0 and (rows // n_steps) % 16 == 0
    return pl.BlockSpec((rows // n_steps, cols), lambda bi, h, qi: (step_of(bi, h, qi), 0))


def _sb_kernel(q_ref, k_ref, vt_ref, g_ref, u2_ref, w_ref, o_ref, wb_ref):
    wb_ref[...] = w_ref[...].astype(BF16)

    qi = pl.program_id(2)
    u2 = u2_ref[...]
    n_sub = TQ_S // TK_S
    key = lax.broadcasted_iota(jnp.int32, (TK_S, TK_S), 0)
    qry = lax.broadcasted_iota(jnp.int32, (TK_S, TK_S), 1)
    causal = key < qry

    def stage_q(q, kb):
        ks = pl.multiple_of(kb * TK_S, TK_S)
        return _dot_nt(k_ref[pl.ds(ks, TK_S), :], q)

    def stage_e(z, diagonal):
        zn = jnp.minimum(z, 0.0)
        zp = zn - z
        sp = jnp.log2(1.0 + jnp.exp2(zn + zp))
        lb = zn - sp
        lk = zp - sp
        if diagonal:
            lk = jnp.where(causal, lk, 0.0)
        hi = lk.astype(BF16)
        lo = (lk - hi.astype(F32)).astype(BF16)
        return lb, jnp.concatenate([hi, lo], axis=0), _col_reduce(lk, jnp.add, jnp.sum)

    def stage_c(hilo):
        return _dot(u2, hilo)

    def stage_x(lb, rest, carry, diagonal):
        if diagonal:
            a = jnp.where(causal, jnp.exp2(lb + rest), 0.0)
        else:
            a = jnp.exp2(lb + (rest + carry))
        return a.astype(BF16)

    def stage_v(kb, a, gate=None):
        vt = vt_ref[:, pl.ds(pl.multiple_of(kb * TK_S, TK_S), TK_S)]
        if gate is not None:
            vt = vt * gate
        return _dot(vt, a)

    items = []
    for sub in range(n_sub):
        kd = qi * n_sub + sub
        q = q_ref[sub * TK_S:(sub + 1) * TK_S, :]
        items.append(dict(q=q, kb=kd, diagonal=True, gate=None))
        items.append(dict(q=q, kb=jnp.maximum(kd - 1, 0), diagonal=False,
                          gate=jnp.where(kd > 0, 1.0, 0.0).astype(BF16)))
    n_items = len(items)
    for t in range(n_items + 4):
        if 4 <= t:
            it = items[t - 4]
            it["pv"] = stage_v(it["kb"], it["a"], it["gate"])
        if 2 <= t < n_items + 2:
            it = items[t - 2]
            it["rest"] = stage_c(it["hilo"])
        if t < n_items:
            it = items[t]
            it["z"] = stage_q(it["q"], it["kb"])
        if 3 <= t < n_items + 3:
            it = items[t - 3]
            carry_in = None if it["diagonal"] else items[t - 4]["colsum"]
            it["a"] = stage_x(it["lb"], it["rest"], carry_in, it["diagonal"])
        if 1 <= t < n_items + 1:
            it = items[t - 1]
            it["lb"], it["hilo"], it["colsum"] = stage_e(it["z"], it["diagonal"])

    qs = [items[2 * r]["q"] for r in range(n_sub)]
    state = []
    for r in range(n_sub):
        d, p = items[2 * r], items[2 * r + 1]
        state += [d["colsum"] + p["colsum"], d["pv"] + p["pv"]]

    kd_last = qi * n_sub + n_sub - 1

    def cond(st):
        worst = st[1]
        for r in range(1, n_sub):
            worst = jnp.maximum(worst, st[1 + 2 * r])
        return jnp.logical_and(kd_last - 2 - st[0] >= 0, jnp.max(worst) > SB_DEAD_LOG2)

    def body(st):
        j, out = st[0], []
        for r in range(n_sub):
            kb = qi * n_sub + r - 2 - j
            gate = jnp.where(kb >= 0, 1.0, 0.0).astype(BF16)
            kb = jnp.maximum(kb, 0)
            lb, hilo, colsum = stage_e(stage_q(qs[r], kb), False)
            a = stage_x(lb, stage_c(hilo), st[1 + 2 * r], False)
            out += [st[1 + 2 * r] + colsum, st[2 + 2 * r] + stage_v(kb, a, gate)]
        return (j + 1, *out)

    st = lax.while_loop(cond, body, (0, *state))
    _store_head_out(o_ref, jnp.concatenate([st[2 + 2 * r] for r in range(n_sub)], axis=1), g_ref[...])


def _sb_attn(qk, v_t, g_col, u2, w_cast):
    b, s, _ = qk.shape
    assert TQ_S % TK_S == 0
    nq = s // TQ_S
    w_spec = _cast_rows_spec(w_cast, b * N_HEADS * nq, lambda bi, h, qi: (bi * N_HEADS + h) * nq + qi)
    return pl.pallas_call(
        _sb_kernel,
        grid=(b, N_HEADS, nq),
        in_specs=[
            pl.BlockSpec((None, TQ_S, HEAD_DIM), lambda bi, h, qi: (bi, qi, h)),
            pl.BlockSpec((None, s, HEAD_DIM), lambda bi, h, qi: (bi, 0, N_HEADS + h)),
            pl.BlockSpec((None, HEAD_DIM, s), lambda bi, h, qi: (h, 0, bi)),
            pl.BlockSpec((HEAD_DIM, 1), lambda bi, h, qi: (h, 0)),
            pl.BlockSpec((TK_S, 2 * TK_S), lambda bi, h, qi: (0, 0)),
            w_spec,
        ],
        out_specs=[pl.BlockSpec((None, TQ_S, HEAD_DIM), lambda bi, h, qi: (bi, qi, h)), w_spec],
        out_shape=[jax.ShapeDtypeStruct((b, s, GROUP_W), BF16),
                   jax.ShapeDtypeStruct(w_cast.shape, BF16)],
        compiler_params=_params(("arbitrary", "arbitrary", "arbitrary")),
        name="sb_attn",
    )(qk, qk, v_t, g_col, u2, w_cast)


def _fox_kernel(q_ref, k_ref, extq_ref, extk_ref, vt_ref, cq_ref, clast_ref, g_ref, wu_ref, wd_ref,
                o_ref, wub_ref, wdb_ref,
                kn_ref, z_ref, p_ref, m_ref, l_ref, a_ref, acc_ref):
    wub_ref[...] = wu_ref[...].astype(BF16)
    wdb_ref[...] = wd_ref[...].astype(BF16)

    bi, h, qi = pl.program_id(0), pl.program_id(1), pl.program_id(2)
    n_full = (qi * TQ_F) // TK_F

    @pl.when(qi == 0)
    def _():
        def chunk(i, best):
            r0 = pl.multiple_of(i * KN_BLK, KN_BLK)
            kf = k_ref[pl.ds(r0, KN_BLK), :].astype(F32)
            n2 = jnp.sum(kf * kf, axis=-1, keepdims=True)
            return jnp.maximum(best, jnp.max(n2, axis=0, keepdims=True))

        best = lax.fori_loop(0, k_ref.shape[0] // KN_BLK, chunk, jnp.zeros((1, 1), F32))
        kn_ref[...] = jnp.broadcast_to(jnp.sqrt(best), kn_ref.shape)

    q = q_ref[...]
    q_aug = jnp.concatenate([q, extq_ref[...]], axis=1)

    qf = q.astype(F32)
    q2_up = (qf * qf * (1.0 + 2.0 ** -7)).astype(BF16)
    qn2 = _dot_nt(jnp.ones((8, HEAD_DIM), BF16), q2_up)[:1]
    ub = jnp.sqrt(qn2) * kn_ref[:, :1] * NORM_SLACK + cq_ref[...]

    last_tile = vt_ref.shape[1] // TK_F - 1

    def kb_of(i):
        return jnp.clip(n_full - i, 0, last_tile)

    def qk_t(i):
        ks = pl.multiple_of(kb_of(i) * TK_F, TK_F)
        k_aug = jnp.concatenate([k_ref[pl.ds(ks, TK_F), :], extk_ref[pl.ds(ks, TK_F), :]], axis=1)
        return _dot_nt(k_aug, q_aug)

    def pv_t(i, slot):
        ks = pl.multiple_of(kb_of(i) * TK_F, TK_F)
        return _dot(vt_ref[:, pl.ds(ks, TK_F)], p_ref[slot])

    st = qk_t(0)
    z_ref[1] = qk_t(1)
    key = n_full * TK_F + lax.broadcasted_iota(jnp.int32, (TK_F, TQ_F), 0)
    qry = qi * TQ_F + lax.broadcasted_iota(jnp.int32, (TK_F, TQ_F), 1)
    st = jnp.where(key <= qry, st, -jnp.inf)
    m0 = _col_reduce(st, jnp.maximum, jnp.max)
    p0 = jnp.exp2(st - m0)
    p_ref[0] = p0.astype(BF16)
    m_ref[...] = m0
    l_ref[...] = _col_reduce(p0, jnp.add, jnp.sum)
    a_ref[...] = jnp.ones_like(m0)
    acc_ref[...] = jnp.zeros_like(acc_ref)

    gap = jnp.max(ub - m0)
    n_tiles = 1
    for i in range(clast_ref.shape[2]):
        alive = jnp.logical_and(i < n_full, gap - clast_ref[bi, h, i] > FOX_DEAD_LOG2)
        n_tiles = n_tiles + alive.astype(jnp.int32)

    def step(i, slot, active=None):
        z_ref[1 - slot] = qk_t(i + 1)
        acc_ref[...] = a_ref[...] * acc_ref[...] + pv_t(i - 1, 1 - slot)
        z = z_ref[slot]
        if active is not None:
            z = jnp.where(active, z, -jnp.inf)
        m = m_ref[...]
        m_new = jnp.maximum(m, _col_reduce(z, jnp.maximum, jnp.max))
        alpha = jnp.exp2(m - m_new)
        p = jnp.exp2(z_ref[slot] - m_new) if active is None else jnp.exp2(z - m_new)
        p_ref[slot] = p.astype(BF16)
        l_ref[...] = alpha * l_ref[...] + _col_reduce(p, jnp.add, jnp.sum)
        m_ref[...] = m_new
        a_ref[...] = alpha

    def body(j, _):
        i = 1 + 2 * j
        step(i, 1)
        step(i + 1, 0, active=i + 1 < n_tiles)
        return 0

    n_pairs = n_tiles // 2
    lax.fori_loop(0, n_pairs, body, 0)
    acc = a_ref[...] * acc_ref[...] + pv_t(2 * n_pairs, 0)
    _store_head_out(o_ref, acc / l_ref[...], g_ref[...])


def _fox_attn(qk, v_t, c, ext_q, ext_k, g_col, w_cast_a, w_cast_b):
    b, s, _ = qk.shape
    base = 2 * N_HEADS
    n_kt = s // TK_F
    nq = s // TQ_F
    assert TK_F % TQ_F == 0
    step_of = lambda bi, h, qi: (bi * N_HEADS + h) * nq + qi
    wa_spec = _cast_rows_spec(w_cast_a, b * N_HEADS * nq, step_of)
    wb_spec = _cast_rows_spec(w_cast_b, b * N_HEADS * nq, step_of)
    c_hs = jnp.transpose(c[:, :, :N_HEADS], (0, 2, 1))
    c_q = c_hs.reshape(b, N_HEADS, s // TQ_F, 1, TQ_F)
    c_last = c_hs.reshape(b, N_HEADS, n_kt, TK_F)[..., TK_F - 1]
    return pl.pallas_call(
        _fox_kernel,
        grid=(b, N_HEADS, s // TQ_F),
        in_specs=[
            pl.BlockSpec((None, TQ_F, HEAD_DIM), lambda bi, h, qi: (bi, qi, base + h)),
            pl.BlockSpec((None, s, HEAD_DIM), lambda bi, h, qi: (bi, 0, base + N_HEADS + h)),
            pl.BlockSpec((None, TQ_F, HEAD_DIM), lambda bi, h, qi: (bi, qi, h)),
            pl.BlockSpec((None, s, HEAD_DIM), lambda bi, h, qi: (bi, 0, h)),
            pl.BlockSpec((None, HEAD_DIM, s), lambda bi, h, qi: (N_HEADS + h, 0, bi)),
            pl.BlockSpec((None, None, None, 1, TQ_F), lambda bi, h, qi: (bi, h, qi, 0, 0)),
            pl.BlockSpec(memory_space=pltpu.SMEM),
            pl.BlockSpec((HEAD_DIM, 1), lambda bi, h, qi: (h, 0)),
            wa_spec,
            wb_spec,
        ],
        out_specs=[pl.BlockSpec((None, TQ_F, HEAD_DIM), lambda bi, h, qi: (bi, qi, h)), wa_spec, wb_spec],
        scratch_shapes=[
            pltpu.VMEM((1, LANES), F32),
            pltpu.VMEM((2, TK_F, TQ_F), F32),
            pltpu.VMEM((2, TK_F, TQ_F), BF16),
            pltpu.VMEM((1, TQ_F), F32),
            pltpu.VMEM((1, TQ_F), F32),
            pltpu.VMEM((1, TQ_F), F32),
            pltpu.VMEM((HEAD_DIM, TQ_F), F32),
        ],
        out_shape=[jax.ShapeDtypeStruct((b, s, GROUP_W), BF16),
                   jax.ShapeDtypeStruct(w_cast_a.shape, BF16),
                   jax.ShapeDtypeStruct(w_cast_b.shape, BF16)],
        compiler_params=_params(("arbitrary", "arbitrary", "arbitrary")),
        name="fox_attn",
    )(qk, qk, ext_q, ext_k, v_t, c_q, c_last, g_col, w_cast_a, w_cast_b)


def _out_proj_kernel(ms_ref, mf_ref, w_ref, x_ref, g_ref, x1_ref, h2_ref):
    acc = _dot(ms_ref[...], w_ref[:GROUP_W, :]) + _dot(mf_ref[...], w_ref[GROUP_W:, :])
    x1 = x_ref[...] + acc
    x1_ref[...] = x1
    var = jnp.mean(x1 * x1, axis=-1, keepdims=True)
    h2_ref[...] = (x1 * lax.rsqrt(var + EPS) * g_ref[...]).astype(BF16)


def _out_proj(mixed_sb, mixed_fox, w_out, x2, g_mlp):
    m, d = x2.shape
    return pl.pallas_call(
        _out_proj_kernel,
        grid=(m // TM_OUT,),
        in_specs=[
            pl.BlockSpec((TM_OUT, GROUP_W), lambda i: (i, 0)),
            pl.BlockSpec((TM_OUT, GROUP_W), lambda i: (i, 0)),
            pl.BlockSpec((2 * GROUP_W, d), lambda i: (0, 0)),
            pl.BlockSpec((TM_OUT, d), lambda i: (i, 0)),
            pl.BlockSpec((1, d), lambda i: (0, 0)),
        ],
        out_specs=[
            pl.BlockSpec((TM_OUT, d), lambda i: (i, 0)),
            pl.BlockSpec((TM_OUT, d), lambda i: (i, 0)),
        ],
        out_shape=[
            jax.ShapeDtypeStruct((m, d), F32),
            jax.ShapeDtypeStruct((m, d), BF16),
        ],
        compiler_params=_params(("arbitrary",)),
        name="out_proj",
    )(mixed_sb, mixed_fox, w_out, x2, g_mlp)


def _mlp_kernel(h_ref, x1_ref, wu_ref, wd_ref, g_ref, o_ref):
    f = pl.program_id(1)

    @pl.when(f == 0)
    def _():
        o_ref[...] = x1_ref[...]

    u = jnp.maximum(_dot(h_ref[...], wu_ref[...]), 0.0)
    o_ref[...] += _dot((u * u).astype(BF16), wd_ref[...])

    @pl.when(f == pl.num_programs(1) - 1)
    def _():
        x2 = o_ref[...]
        var = jnp.mean(x2 * x2, axis=-1, keepdims=True)
        o_ref[...] = x2 * lax.rsqrt(var + EPS) * g_ref[...]


def _mlp(h2, x1, w_up, w_down, g_final):
    m, d = x1.shape
    dff = w_up.shape[1]
    return pl.pallas_call(
        _mlp_kernel,
        grid=(m // TM_MLP, dff // TF_MLP),
        in_specs=[
            pl.BlockSpec((TM_MLP, d), lambda i, f: (i, 0)),
            pl.BlockSpec((TM_MLP, d), lambda i, f: (i, 0)),
            pl.BlockSpec((d, TF_MLP), lambda i, f: (0, f)),
            pl.BlockSpec((TF_MLP, d), lambda i, f: (f, 0)),
            pl.BlockSpec((1, d), lambda i, f: (0, 0)),
        ],
        out_specs=pl.BlockSpec((TM_MLP, d), lambda i, f: (i, 0)),
        out_shape=jax.ShapeDtypeStruct((m, d), F32),
        compiler_params=_params(("arbitrary", "arbitrary")),
        name="mlp",
    )(h2, x1, w_up, w_down, g_final)


def kernel(x, g_attn, w_in, b_f, g_out_sb, g_out_fox, w_out, g_mlp, w_up, w_down, g_final):
    b, s, d = x.shape
    n_qkv = 6 * GROUP_W
    assert s % TQ_S == 0 and s % TK_F == 0 and (b * s) % TM_IN == 0
    x2 = x.reshape(b * s, d)

    idx = jnp.arange(TK_S)
    u_excl = (idx[None, :] > idx[:, None]).astype(BF16)
    u2 = jnp.concatenate([u_excl, u_excl], axis=1)
    cidx = jnp.arange(CS_BLK)
    tri_incl = (cidx[None, :] <= cidx[:, None]).astype(BF16)

    gw = GROUP_W
    for l in range(g_attn.shape[0]):
        w_t = jnp.swapaxes(w_in[l], 0, 1)
        w_qkv_t = w_t.astype(BF16)
        w_f_t = jnp.pad(w_t[n_qkv:], ((0, LANES - N_HEADS), (0, 0))).astype(BF16)
        b_pad = jnp.pad(b_f[l], (0, LANES - N_HEADS)).reshape(1, LANES)

        qk, v_t, f_logit = _in_proj(x2, g_attn[l].reshape(1, d), w_qkv_t, w_f_t)
        qk = qk.reshape(b, s, N_QK_GROUPS * gw)
        v_t = v_t.reshape(2 * N_HEADS, HEAD_DIM, b * s)
        c, ext_q, ext_k = _forget_cs(f_logit.reshape(b, s, LANES), b_pad, tri_incl)

        mixed_sb, w_out_b = _sb_attn(qk, v_t, g_out_sb[l].reshape(gw, 1), u2, w_out[l])
        mixed_fox, w_up_b, w_down_b = _fox_attn(qk, v_t, c, ext_q, ext_k, g_out_fox[l].reshape(gw, 1),
                                                w_up[l], w_down[l])

        x1, h2 = _out_proj(mixed_sb.reshape(b * s, GROUP_W), mixed_fox.reshape(b * s, GROUP_W),
                           w_out_b, x2, g_mlp[l].reshape(1, d))
        assert g_attn.shape[0] == 1
        x2 = _mlp(h2, x1, w_up_b, w_down_b, g_final.reshape(1, d))
    return x2.reshape(b, s, d)
```

```python
import functools

import jax
import jax.numpy as jnp
from jax import lax
from jax.experimental import pallas as pl
from jax.experimental.pallas import tpu as pltpu

F32 = jnp.float32
BF16 = jnp.bfloat16

HEAD_DIM = 128
N_HEADS = 8
GROUP_W = N_HEADS * HEAD_DIM
EPS = 1e-6
SCALE = HEAD_DIM ** -0.5
LOG2E = 1.4426950408889634
SB_DEAD_LOG2 = -152.0
FOX_DEAD_LOG2 = -152.0
NORM_SLACK = 1.01
M_INIT = -1e30
KN_BLK = 512
LANES = 128

VMEM_LIMIT = 56 * 1024 * 1024
VMEM_LIMIT_MLP = 58 * 1024 * 1024

TM_IN = 1024
TN_IN = 1024
IN_CHUNK = 256
TQ_S = 2048
TK_S = 256
TQ_F = 512
TK_F = 512
CS_BLK = 256
TM_OUT = 512
TM_MLP = 512
TF_MLP = 2048
MLP_CHUNK = 1024


def _params(sem, vmem_limit=VMEM_LIMIT):
    return pltpu.CompilerParams(dimension_semantics=sem, vmem_limit_bytes=vmem_limit)


def _dot(a, b):
    return jnp.dot(a, b, preferred_element_type=F32)


def _dot_nt(a, b):
    return lax.dot_general(a, b, (((1,), (1,)), ((), ())), preferred_element_type=F32)


def _log_sigmoid(x):
    return jnp.minimum(x, 0.0) - jnp.log(1.0 + jnp.exp(-jnp.abs(x)))


N_QK_GROUPS = 4


def _in_proj_kernel(x_ref, g_ref, w_ref, wf_ref, qk_ref, vt_ref, f_ref, h_ref):
    j = pl.program_id(1)

    @pl.when(j == 0)
    def _():
        x = x_ref[...]
        var = jnp.mean(x * x, axis=-1, keepdims=True)
        h = (x * lax.rsqrt(var + EPS) * g_ref[...]).astype(BF16)
        h_ref[...] = h
        f_ref[...] = _dot_nt(h, wf_ref[...])

    @pl.when(j < N_QK_GROUPS)
    def _():
        scale = jnp.where((j == 0) | (j == 2), SCALE * LOG2E, 1.0).astype(F32)
        for c0 in range(0, TN_IN, IN_CHUNK):
            c1 = c0 + IN_CHUNK
            qk_ref[:, c0:c1] = (_dot_nt(h_ref[...], w_ref[c0:c1, :]) * scale).astype(BF16)

    @pl.when(j >= N_QK_GROUPS)
    def _():
        for c0 in range(0, TN_IN, IN_CHUNK):
            c1 = c0 + IN_CHUNK
            vt_ref[c0:c1, :] = _dot_nt(w_ref[c0:c1, :], h_ref[...]).astype(BF16)


def _w_group(j):
    return jnp.where(j < 2, j, jnp.where(j < 4, j + 1, jnp.where(j == 4, 2, 5)))


def _in_proj(x2, g, w_qkv_t, w_f_t):
    m, d = x2.shape
    n = 6 * GROUP_W
    assert TN_IN == GROUP_W and w_qkv_t.shape[0] >= n
    return pl.pallas_call(
        _in_proj_kernel,
        grid=(m // TM_IN, n // TN_IN),
        in_specs=[
            pl.BlockSpec((TM_IN, d), lambda i, j: (i, 0)),
            pl.BlockSpec((1, d), lambda i, j: (0, 0)),
            pl.BlockSpec((TN_IN, d), lambda i, j: (_w_group(j), 0)),
            pl.BlockSpec((LANES, d), lambda i, j: (0, 0)),
        ],
        out_specs=[
            pl.BlockSpec((TM_IN, TN_IN), lambda i, j: (i, jnp.minimum(j, N_QK_GROUPS - 1))),
            pl.BlockSpec((TN_IN, TM_IN), lambda i, j: (jnp.maximum(j - N_QK_GROUPS, 0), i)),
            pl.BlockSpec((TM_IN, LANES), lambda i, j: (i, 0)),
        ],
        out_shape=[
            jax.ShapeDtypeStruct((m, N_QK_GROUPS * GROUP_W), BF16),
            jax.ShapeDtypeStruct((2 * GROUP_W, m), BF16),
            jax.ShapeDtypeStruct((m, LANES), F32),
        ],
        scratch_shapes=[pltpu.VMEM((TM_IN, d), BF16)],
        compiler_params=_params(("arbitrary", "arbitrary")),
        name="in_proj",
    )(x2, g, w_qkv_t, w_f_t)


def _split3(x):
    p1 = x.astype(BF16)
    r1 = x - p1.astype(F32)
    p2 = r1.astype(BF16)
    p3 = (r1 - p2.astype(F32)).astype(BF16)
    return p1, p2, p3


def _forget_cs_kernel(f_ref, b_ref, tri_ref, selq_ref, selk_ref, oneq_ref, onek_ref,
                      c_ref, extq_ref, extk_ref):
    n_blk = f_ref.shape[0] // CS_BLK
    tri = tri_ref[...]

    def body(i, carry):
        r0 = pl.multiple_of(i * CS_BLK, CS_BLK)
        lf = _log_sigmoid(f_ref[pl.ds(r0, CS_BLK), :] + b_ref[...])
        p1, p2, p3 = _split3(lf)
        c = _dot(tri, p1) + _dot(tri, p2) + _dot(tri, p3) + carry
        c2 = c * LOG2E
        c_ref[pl.ds(r0, CS_BLK), :] = c2
        lane = lax.broadcasted_iota(jnp.int32, c2.shape, 1)
        packed = jnp.zeros_like(c2)
        for k, piece in enumerate(_split3(c2)):
            piece = jnp.where(lane < N_HEADS, piece.astype(F32), 0.0)
            packed = packed + (piece if k == 0 else pltpu.roll(piece, N_HEADS * k, axis=1))
        packed = packed.astype(BF16)
        extq_ref[pl.ds(r0, CS_BLK), :] = (_dot(packed, selq_ref[...]) + oneq_ref[...]).astype(BF16)
        extk_ref[pl.ds(r0, CS_BLK), :] = (_dot(packed, selk_ref[...]) + onek_ref[...]).astype(BF16)
        return c[CS_BLK - 1:CS_BLK, :]

    lax.fori_loop(0, n_blk, body, jnp.zeros((1, LANES), F32))


def _aug_constants():
    row = jnp.arange(LANES)
    col = jnp.arange(GROUP_W)
    piece, head = row // N_HEADS, row % N_HEADS
    blk, lane = col // HEAD_DIM, col % HEAD_DIM
    same = (head[:, None] == blk[None, :]) & (piece[:, None] < 3)
    selq = jnp.where(same & (lane[None, :] == 3 + piece[:, None]), 1.0, 0.0).astype(BF16)
    selk = jnp.where(same & (lane[None, :] == piece[:, None]), -1.0, 0.0).astype(BF16)
    oneq = jnp.where(lane < 3, 1.0, 0.0).astype(F32).reshape(1, GROUP_W)
    onek = jnp.where((lane >= 3) & (lane < 6), 1.0, 0.0).astype(F32).reshape(1, GROUP_W)
    return selq, selk, oneq, onek


def _forget_cs(f_logit, b_pad, tri_incl):
    b, s, _ = f_logit.shape
    const = lambda shape: pl.BlockSpec(shape, lambda i: (0, 0))
    return pl.pallas_call(
        _forget_cs_kernel,
        grid=(b,),
        in_specs=[
            pl.BlockSpec((None, s, LANES), lambda i: (i, 0, 0)),
            const((1, LANES)),
            const((CS_BLK, CS_BLK)),
            const((LANES, GROUP_W)),
            const((LANES, GROUP_W)),
            const((1, GROUP_W)),
            const((1, GROUP_W)),
        ],
        out_specs=[
            pl.BlockSpec((None, s, LANES), lambda i: (i, 0, 0)),
            pl.BlockSpec((None, s, GROUP_W), lambda i: (i, 0, 0)),
            pl.BlockSpec((None, s, GROUP_W), lambda i: (i, 0, 0)),
        ],
        out_shape=[
            jax.ShapeDtypeStruct((b, s, LANES), F32),
            jax.ShapeDtypeStruct((b, s, GROUP_W), BF16),
            jax.ShapeDtypeStruct((b, s, GROUP_W), BF16),
        ],
        compiler_params=_params(("arbitrary",)),
        name="forget_cs",
    )(f_logit, b_pad, tri_incl, *_aug_constants())


def _head_rmsnorm(o, g):
    return o * lax.rsqrt(jnp.mean(o * o, axis=-1, keepdims=True) + EPS) * g


def _col_reduce(x, pair, full):
    while x.shape[0] > 8:
        half = x.shape[0] // 2
        x = pair(x[:half], x[half:])
    return full(x, axis=0, keepdims=True)


def _store_head_out(o_ref, acc_t, g_col):
    o = acc_t * lax.rsqrt(jnp.mean(acc_t * acc_t, axis=0, keepdims=True) + EPS) * g_col
    o_ref[...] = jnp.transpose(o).astype(o_ref.dtype)


def _cast_rows_spec(w, n_steps, step_of):
    rows, cols = w.shape
    assert rows % n_steps == 0 and (rows // n_steps) % 16 == 0
    return pl.BlockSpec((rows // n_steps, cols), lambda bi, h, qi: (step_of(bi, h, qi), 0))


def _sb_kernel(q_ref, k_ref, vt_ref, g_ref, u2_ref, w_ref, o_ref, wb_ref):
    wb_ref[...] = w_ref[...].astype(BF16)

    qi = pl.program_id(2)
    u2 = u2_ref[...]
    n_sub = TQ_S // TK_S
    key = lax.broadcasted_iota(jnp.int32, (TK_S, TK_S), 0)
    qry = lax.broadcasted_iota(jnp.int32, (TK_S, TK_S), 1)
    causal = key < qry

    def stage_q(q, kb):
        ks = pl.multiple_of(kb * TK_S, TK_S)
        return _dot_nt(k_ref[pl.ds(ks, TK_S), :], q)

    def stage_e(z, diagonal):
        zn = jnp.minimum(z, 0.0)
        zp = zn - z
        sp = jnp.log2(1.0 + jnp.exp2(zn + zp))
        lb = zn - sp
        lk = zp - sp
        if diagonal:
            lk = jnp.where(causal, lk, 0.0)
        hi = lk.astype(BF16)
        lo = (lk - hi.astype(F32)).astype(BF16)
        return lb, jnp.concatenate([hi, lo], axis=0), _col_reduce(lk, jnp.add, jnp.sum)

    def stage_c(hilo):
        return _dot(u2, hilo)

    def stage_x(lb, rest, carry, diagonal):
        if diagonal:
            a = jnp.where(causal, jnp.exp2(lb + rest), 0.0)
        else:
            a = jnp.exp2(lb + (rest + carry))
        return a.astype(BF16)

    def stage_v(kb, a, gate=None):
        vt = vt_ref[:, pl.ds(pl.multiple_of(kb * TK_S, TK_S), TK_S)]
        if gate is not None:
            vt = vt * gate
        return _dot(vt, a)

    items = []
    for sub in range(n_sub):
        kd = qi * n_sub + sub
        q = q_ref[sub * TK_S:(sub + 1) * TK_S, :]
        items.append(dict(q=q, kb=kd, diagonal=True, gate=None))
        items.append(dict(q=q, kb=jnp.maximum(kd - 1, 0), diagonal=False,
                          gate=jnp.where(kd > 0, 1.0, 0.0).astype(BF16)))
    n_items = len(items)
    for t in range(n_items + 4):
        if 4 <= t:
            it = items[t - 4]
            it["pv"] = stage_v(it["kb"], it["a"], it["gate"])
        if 2 <= t < n_items + 2:
            it = items[t - 2]
            it["rest"] = stage_c(it["hilo"])
        if t < n_items:
            it = items[t]
            it["z"] = stage_q(it["q"], it["kb"])
        if 3 <= t < n_items + 3:
            it = items[t - 3]
            carry_in = None if it["diagonal"] else items[t - 4]["colsum"]
            it["a"] = stage_x(it["lb"], it["rest"], carry_in, it["diagonal"])
        if 1 <= t < n_items + 1:
            it = items[t - 1]
            it["lb"], it["hilo"], it["colsum"] = stage_e(it["z"], it["diagonal"])

    qs = [items[2 * r]["q"] for r in range(n_sub)]
    state = []
    for r in range(n_sub):
        d, p = items[2 * r], items[2 * r + 1]
        state += [d["colsum"] + p["colsum"], d["pv"] + p["pv"]]

    kd_last = qi * n_sub + n_sub - 1

    def cond(st):
        worst = st[1]
        for r in range(1, n_sub):
            worst = jnp.maximum(worst, st[1 + 2 * r])
        return jnp.logical_and(kd_last - 2 - st[0] >= 0, jnp.max(worst) > SB_DEAD_LOG2)

    def body(st):
        j, out = st[0], []
        for r in range(n_sub):
            kb = qi * n_sub + r - 2 - j
            gate = jnp.where(kb >= 0, 1.0, 0.0).astype(BF16)
            kb = jnp.maximum(kb, 0)
            lb, hilo, colsum = stage_e(stage_q(qs[r], kb), False)
            a = stage_x(lb, stage_c(hilo), st[1 + 2 * r], False)
            out += [st[1 + 2 * r] + colsum, st[2 + 2 * r] + stage_v(kb, a, gate)]
        return (j + 1, *out)

    st = lax.while_loop(cond, body, (0, *state))
    _store_head_out(o_ref, jnp.concatenate([st[2 + 2 * r] for r in range(n_sub)], axis=1), g_ref[...])


def _sb_attn(qk, v_t, g_col, u2, w_cast):
    b, s, _ = qk.shape
    assert TQ_S % TK_S == 0
    nq = s // TQ_S
    w_spec = _cast_rows_spec(w_cast, b * N_HEADS * nq, lambda bi, h, qi: (bi * N_HEADS + h) * nq + qi)
    return pl.pallas_call(
        _sb_kernel,
        grid=(b, N_HEADS, nq),
        in_specs=[
            pl.BlockSpec((None, TQ_S, HEAD_DIM), lambda bi, h, qi: (bi, qi, h)),
            pl.BlockSpec((None, s, HEAD_DIM), lambda bi, h, qi: (bi, 0, N_HEADS + h)),
            pl.BlockSpec((None, HEAD_DIM, s), lambda bi, h, qi: (h, 0, bi)),
            pl.BlockSpec((HEAD_DIM, 1), lambda bi, h, qi: (h, 0)),
            pl.BlockSpec((TK_S, 2 * TK_S), lambda bi, h, qi: (0, 0)),
            w_spec,
        ],
        out_specs=[pl.BlockSpec((None, TQ_S, HEAD_DIM), lambda bi, h, qi: (bi, qi, h)), w_spec],
        out_shape=[jax.ShapeDtypeStruct((b, s, GROUP_W), BF16),
                   jax.ShapeDtypeStruct(w_cast.shape, BF16)],
        compiler_params=_params(("arbitrary", "arbitrary", "arbitrary")),
        name="sb_attn",
    )(qk, qk, v_t, g_col, u2, w_cast)


def _fox_kernel(q_ref, k_ref, extq_ref, extk_ref, vt_ref, cq_ref, clast_ref, g_ref, wu_ref, wd_ref,
                o_ref, wub_ref, wdb_ref,
                kn_ref, z_ref, p_ref, m_ref, l_ref, a_ref, acc_ref):
    wub_ref[...] = wu_ref[...].astype(BF16)
    wdb_ref[...] = wd_ref[...].astype(BF16)

    bi, h, qi = pl.program_id(0), pl.program_id(1), pl.program_id(2)
    n_full = (qi * TQ_F) // TK_F

    def sq_norms_ub(rows):
        rf = rows.astype(F32)
        r2_up = (rf * rf * (1.0 + 2.0 ** -7)).astype(BF16)
        return _dot_nt(jnp.ones((8, HEAD_DIM), BF16), r2_up)[:1]

    @pl.when(qi == 0)
    def _():
        best = sq_norms_ub(k_ref[:KN_BLK, :])
        for r0 in range(KN_BLK, k_ref.shape[0], KN_BLK):
            best = jnp.maximum(best, sq_norms_ub(k_ref[r0:r0 + KN_BLK, :]))
        kn_ref[...] = jnp.broadcast_to(jnp.sqrt(jnp.max(best, axis=1, keepdims=True)), kn_ref.shape)

    q = q_ref[...]
    q_aug = jnp.concatenate([q, extq_ref[...]], axis=1)

    ub = jnp.sqrt(sq_norms_ub(q)) * kn_ref[:, :1] * NORM_SLACK + cq_ref[...]

    last_tile = vt_ref.shape[1] // TK_F - 1

    def kb_of(i):
        return jnp.clip(n_full - i, 0, last_tile)

    def qk_t(i):
        ks = pl.multiple_of(kb_of(i) * TK_F, TK_F)
        k_aug = jnp.concatenate([k_ref[pl.ds(ks, TK_F), :], extk_ref[pl.ds(ks, TK_F), :]], axis=1)
        return _dot_nt(k_aug, q_aug)

    def pv_t(i, slot):
        ks = pl.multiple_of(kb_of(i) * TK_F, TK_F)
        return _dot(vt_ref[:, pl.ds(ks, TK_F)], p_ref[slot])

    st = qk_t(0)
    z_ref[1] = qk_t(1)
    key = n_full * TK_F + lax.broadcasted_iota(jnp.int32, (TK_F, TQ_F), 0)
    qry = qi * TQ_F + lax.broadcasted_iota(jnp.int32, (TK_F, TQ_F), 1)
    st = jnp.where(key <= qry, st, -jnp.inf)
    m0 = _col_reduce(st, jnp.maximum, jnp.max)
    p0 = jnp.exp2(st - m0)
    p_ref[0] = p0.astype(BF16)
    m_ref[...] = m0
    l_ref[...] = _col_reduce(p0, jnp.add, jnp.sum)
    a_ref[...] = jnp.ones_like(m0)
    acc_ref[...] = jnp.zeros_like(acc_ref)

    gap = jnp.max(ub - m0)
    n_tiles = 1
    for i in range(clast_ref.shape[2]):
        alive = jnp.logical_and(i < n_full, gap - clast_ref[bi, h, i] > FOX_DEAD_LOG2)
        n_tiles = n_tiles + alive.astype(jnp.int32)

    def step(i, slot, active=None):
        z_ref[1 - slot] = qk_t(i + 1)
        acc_ref[...] = a_ref[...] * acc_ref[...] + pv_t(i - 1, 1 - slot)
        z = z_ref[slot]
        if active is not None:
            z = jnp.where(active, z, -jnp.inf)
        m = m_ref[...]
        m_new = jnp.maximum(m, _col_reduce(z, jnp.maximum, jnp.max))
        alpha = jnp.exp2(m - m_new)
        p = jnp.exp2(z_ref[slot] - m_new) if active is None else jnp.exp2(z - m_new)
        p_ref[slot] = p.astype(BF16)
        l_ref[...] = alpha * l_ref[...] + _col_reduce(p, jnp.add, jnp.sum)
        m_ref[...] = m_new
        a_ref[...] = alpha

    def body(j, _):
        i = 1 + 2 * j
        step(i, 1)
        step(i + 1, 0, active=i + 1 < n_tiles)
        return 0

    n_pairs = n_tiles // 2
    lax.fori_loop(0, n_pairs, body, 0)
    acc = a_ref[...] * acc_ref[...] + pv_t(2 * n_pairs, 0)
    _store_head_out(o_ref, acc / l_ref[...], g_ref[...])


def _fox_attn(qk, v_t, c, ext_q, ext_k, g_col, w_cast_a, w_cast_b):
    b, s, _ = qk.shape
    base = 2 * N_HEADS
    n_kt = s // TK_F
    nq = s // TQ_F
    assert TK_F % TQ_F == 0
    step_of = lambda bi, h, qi: (bi * N_HEADS + h) * nq + qi
    wa_spec = _cast_rows_spec(w_cast_a, b * N_HEADS * nq, step_of)
    wb_spec = _cast_rows_spec(w_cast_b, b * N_HEADS * nq, step_of)
    c_hs = jnp.transpose(c[:, :, :N_HEADS], (0, 2, 1))
    c_q = c_hs.reshape(b, N_HEADS, s // TQ_F, 1, TQ_F)
    c_last = c_hs.reshape(b, N_HEADS, n_kt, TK_F)[..., TK_F - 1]
    return pl.pallas_call(
        _fox_kernel,
        grid=(b, N_HEADS, s // TQ_F),
        in_specs=[
            pl.BlockSpec((None, TQ_F, HEAD_DIM), lambda bi, h, qi: (bi, qi, base + h)),
            pl.BlockSpec((None, s, HEAD_DIM), lambda bi, h, qi: (bi, 0, base + N_HEADS + h)),
            pl.BlockSpec((None, TQ_F, HEAD_DIM), lambda bi, h, qi: (bi, qi, h)),
            pl.BlockSpec((None, s, HEAD_DIM), lambda bi, h, qi: (bi, 0, h)),
            pl.BlockSpec((None, HEAD_DIM, s), lambda bi, h, qi: (N_HEADS + h, 0, bi)),
            pl.BlockSpec((None, None, None, 1, TQ_F), lambda bi, h, qi: (bi, h, qi, 0, 0)),
            pl.BlockSpec(memory_space=pltpu.SMEM),
            pl.BlockSpec((HEAD_DIM, 1), lambda bi, h, qi: (h, 0)),
            wa_spec,
            wb_spec,
        ],
        out_specs=[pl.BlockSpec((None, TQ_F, HEAD_DIM), lambda bi, h, qi: (bi, qi, h)), wa_spec, wb_spec],
        scratch_shapes=[
            pltpu.VMEM((1, LANES), F32),
            pltpu.VMEM((2, TK_F, TQ_F), F32),
            pltpu.VMEM((2, TK_F, TQ_F), BF16),
            pltpu.VMEM((1, TQ_F), F32),
            pltpu.VMEM((1, TQ_F), F32),
            pltpu.VMEM((1, TQ_F), F32),
            pltpu.VMEM((HEAD_DIM, TQ_F), F32),
        ],
        out_shape=[jax.ShapeDtypeStruct((b, s, GROUP_W), BF16),
                   jax.ShapeDtypeStruct(w_cast_a.shape, BF16),
                   jax.ShapeDtypeStruct(w_cast_b.shape, BF16)],
        compiler_params=_params(("arbitrary", "arbitrary", "arbitrary")),
        name="fox_attn",
    )(qk, qk, ext_q, ext_k, v_t, c_q, c_last, g_col, w_cast_a, w_cast_b)


def _out_proj_kernel(ms_ref, mf_ref, w_ref, x_ref, g_ref, x1_ref, h2_ref):
    acc = _dot(ms_ref[...], w_ref[:GROUP_W, :]) + _dot(mf_ref[...], w_ref[GROUP_W:, :])
    x1 = x_ref[...] + acc
    x1_ref[...] = x1
    var = jnp.mean(x1 * x1, axis=-1, keepdims=True)
    h2_ref[...] = (x1 * lax.rsqrt(var + EPS) * g_ref[...]).astype(BF16)


def _out_proj(mixed_sb, mixed_fox, w_out, x2, g_mlp):
    m, d = x2.shape
    return pl.pallas_call(
        _out_proj_kernel,
        grid=(m // TM_OUT,),
        in_specs=[
            pl.BlockSpec((TM_OUT, GROUP_W), lambda i: (i, 0)),
            pl.BlockSpec((TM_OUT, GROUP_W), lambda i: (i, 0)),
            pl.BlockSpec((2 * GROUP_W, d), lambda i: (0, 0)),
            pl.BlockSpec((TM_OUT, d), lambda i: (i, 0)),
            pl.BlockSpec((1, d), lambda i: (0, 0)),
        ],
        out_specs=[
            pl.BlockSpec((TM_OUT, d), lambda i: (i, 0)),
            pl.BlockSpec((TM_OUT, d), lambda i: (i, 0)),
        ],
        out_shape=[
            jax.ShapeDtypeStruct((m, d), F32),
            jax.ShapeDtypeStruct((m, d), BF16),
        ],
        compiler_params=_params(("arbitrary",)),
        name="out_proj",
    )(mixed_sb, mixed_fox, w_out, x2, g_mlp)


def _mlp_kernel(h_ref, x1_ref, wu_ref, wd_ref, g_ref, o_ref):
    f = pl.program_id(1)

    @pl.when(f == 0)
    def _():
        o_ref[...] = x1_ref[...]

    for c0 in range(0, wu_ref.shape[1], MLP_CHUNK):
        c1 = c0 + MLP_CHUNK
        u = jnp.maximum(_dot(h_ref[...], wu_ref[:, c0:c1]), 0.0)
        o_ref[...] += _dot((u * u).astype(BF16), wd_ref[c0:c1, :])

    @pl.when(f == pl.num_programs(1) - 1)
    def _():
        x2 = o_ref[...]
        var = jnp.mean(x2 * x2, axis=-1, keepdims=True)
        o_ref[...] = x2 * lax.rsqrt(var + EPS) * g_ref[...]


def _mlp(h2, x1, w_up, w_down, g_final):
    m, d = x1.shape
    dff = w_up.shape[1]
    return pl.pallas_call(
        _mlp_kernel,
        grid=(m // TM_MLP, dff // TF_MLP),
        in_specs=[
            pl.BlockSpec((TM_MLP, d), lambda i, f: (i, 0)),
            pl.BlockSpec((TM_MLP, d), lambda i, f: (i, 0)),
            pl.BlockSpec((d, TF_MLP), lambda i, f: (0, f)),
            pl.BlockSpec((TF_MLP, d), lambda i, f: (f, 0)),
            pl.BlockSpec((1, d), lambda i, f: (0, 0)),
        ],
        out_specs=pl.BlockSpec((TM_MLP, d), lambda i, f: (i, 0)),
        out_shape=jax.ShapeDtypeStruct((m, d), F32),
        compiler_params=_params(("arbitrary", "arbitrary"), VMEM_LIMIT_MLP),
        name="mlp",
    )(h2, x1, w_up, w_down, g_final)


def kernel(x, g_attn, w_in, b_f, g_out_sb, g_out_fox, w_out, g_mlp, w_up, w_down, g_final):
    b, s, d = x.shape
    n_qkv = 6 * GROUP_W
    assert s % TQ_S == 0 and s % TK_F == 0 and (b * s) % TM_IN == 0
    x2 = x.reshape(b * s, d)

    idx = jnp.arange(TK_S)
    u_excl = (idx[None, :] > idx[:, None]).astype(BF16)
    u2 = jnp.concatenate([u_excl, u_excl], axis=1)
    cidx = jnp.arange(CS_BLK)
    tri_incl = (cidx[None, :] <= cidx[:, None]).astype(BF16)

    gw = GROUP_W
    for l in range(g_attn.shape[0]):
        w_t = jnp.swapaxes(w_in[l], 0, 1)
        w_qkv_t = w_t.astype(BF16)
        w_f_t = jnp.pad(w_t[n_qkv:], ((0, LANES - N_HEADS), (0, 0))).astype(BF16)
        b_pad = jnp.pad(b_f[l], (0, LANES - N_HEADS)).reshape(1, LANES)

        qk, v_t, f_logit = _in_proj(x2, g_attn[l].reshape(1, d), w_qkv_t, w_f_t)
        qk = qk.reshape(b, s, N_QK_GROUPS * gw)
        v_t = v_t.reshape(2 * N_HEADS, HEAD_DIM, b * s)
        c, ext_q, ext_k = _forget_cs(f_logit.reshape(b, s, LANES), b_pad, tri_incl)

        mixed_sb, w_out_b = _sb_attn(qk, v_t, g_out_sb[l].reshape(gw, 1), u2, w_out[l])
        mixed_fox, w_up_b, w_down_b = _fox_attn(qk, v_t, c, ext_q, ext_k, g_out_fox[l].reshape(gw, 1),
                                                w_up[l], w_down[l])

        x1, h2 = _out_proj(mixed_sb.reshape(b * s, GROUP_W), mixed_fox.reshape(b * s, GROUP_W),
                           w_out_b, x2, g_mlp[l].reshape(1, d))
        assert g_attn.shape[0] == 1
        x2 = _mlp(h2, x1, w_up_b, w_down_b, g_final.reshape(1, d))
    return x2.reshape(b, s, d)
```

```python
import functools

import jax
import jax.numpy as jnp
from jax import lax
from jax.experimental import pallas as pl
from jax.experimental.pallas import tpu as pltpu

F32 = jnp.float32
BF16 = jnp.bfloat16

HEAD_DIM = 128
N_HEADS = 8
GROUP_W = N_HEADS * HEAD_DIM
EPS = 1e-6
SCALE = HEAD_DIM ** -0.5
LOG2E = 1.4426950408889634
SB_DEAD_LOG2 = -152.0
FOX_DEAD_LOG2 = -152.0
NORM_SLACK = 1.01
M_INIT = -1e30
KN_BLK = 512
LANES = 128

VMEM_LIMIT = 56 * 1024 * 1024
VMEM_LIMIT_MLP = 58 * 1024 * 1024

TM_IN = 1024
TN_IN = 1024
IN_CHUNK = 256
TQ_S = 2048
TK_S = 256
TQ_F = 512
TK_F = 512
CS_BLK = 256
TM_OUT = 512
TM_MLP = 512
TF_MLP = 2048
MLP_CHUNK = 1024


def _params(sem, vmem_limit=VMEM_LIMIT):
    return pltpu.CompilerParams(dimension_semantics=sem, vmem_limit_bytes=vmem_limit)


def _dot(a, b):
    return jnp.dot(a, b, preferred_element_type=F32)


def _dot_nt(a, b):
    return lax.dot_general(a, b, (((1,), (1,)), ((), ())), preferred_element_type=F32)


def _log_sigmoid(x):
    return jnp.minimum(x, 0.0) - jnp.log(1.0 + jnp.exp(-jnp.abs(x)))


N_QK_GROUPS = 4


def _in_proj_kernel(x_ref, g_ref, w_ref, wf_ref, qk_ref, vt_ref, f_ref, h_ref):
    j = pl.program_id(1)

    @pl.when(j == 0)
    def _():
        x = x_ref[...]
        var = jnp.mean(x * x, axis=-1, keepdims=True)
        h = (x * lax.rsqrt(var + EPS) * g_ref[...]).astype(BF16)
        h_ref[...] = h
        f_ref[...] = _dot_nt(h, wf_ref[...])

    @pl.when(j < N_QK_GROUPS)
    def _():
        scale = jnp.where((j == 0) | (j == 2), SCALE * LOG2E, 1.0).astype(F32)
        for c0 in range(0, TN_IN, IN_CHUNK):
            c1 = c0 + IN_CHUNK
            w = w_ref[c0:c1, :].astype(BF16)
            qk_ref[:, c0:c1] = (_dot_nt(h_ref[...], w) * scale).astype(BF16)

    @pl.when(j >= N_QK_GROUPS)
    def _():
        for c0 in range(0, TN_IN, IN_CHUNK):
            c1 = c0 + IN_CHUNK
            w = w_ref[c0:c1, :].astype(BF16)
            vt_ref[c0:c1, :] = _dot_nt(w, h_ref[...]).astype(BF16)


def _w_group(j):
    return jnp.where(j < 2, j, jnp.where(j < 4, j + 1, jnp.where(j == 4, 2, 5)))


def _in_proj(x2, g, w_qkv_t, w_f_t):
    m, d = x2.shape
    n = 6 * GROUP_W
    assert TN_IN == GROUP_W and w_qkv_t.shape[0] >= n
    return pl.pallas_call(
        _in_proj_kernel,
        grid=(m // TM_IN, n // TN_IN),
        in_specs=[
            pl.BlockSpec((TM_IN, d), lambda i, j: (i, 0)),
            pl.BlockSpec((1, d), lambda i, j: (0, 0)),
            pl.BlockSpec((TN_IN, d), lambda i, j: (_w_group(j), 0)),
            pl.BlockSpec((LANES, d), lambda i, j: (0, 0)),
        ],
        out_specs=[
            pl.BlockSpec((TM_IN, TN_IN), lambda i, j: (i, jnp.minimum(j, N_QK_GROUPS - 1))),
            pl.BlockSpec((TN_IN, TM_IN), lambda i, j: (jnp.maximum(j - N_QK_GROUPS, 0), i)),
            pl.BlockSpec((TM_IN, LANES), lambda i, j: (i, 0)),
        ],
        out_shape=[
            jax.ShapeDtypeStruct((m, N_QK_GROUPS * GROUP_W), BF16),
            jax.ShapeDtypeStruct((2 * GROUP_W, m), BF16),
            jax.ShapeDtypeStruct((m, LANES), F32),
        ],
        scratch_shapes=[pltpu.VMEM((TM_IN, d), BF16)],
        compiler_params=_params(("arbitrary", "arbitrary")),
        name="in_proj",
    )(x2, g, w_qkv_t, w_f_t)


def _split3(x):
    p1 = x.astype(BF16)
    r1 = x - p1.astype(F32)
    p2 = r1.astype(BF16)
    p3 = (r1 - p2.astype(F32)).astype(BF16)
    return p1, p2, p3


def _forget_cs_kernel(f_ref, b_ref, tri_ref, selq_ref, selk_ref, oneq_ref, onek_ref,
                      c_ref, extq_ref, extk_ref):
    n_blk = f_ref.shape[0] // CS_BLK
    tri = tri_ref[...]

    def body(i, carry):
        r0 = pl.multiple_of(i * CS_BLK, CS_BLK)
        lf = _log_sigmoid(f_ref[pl.ds(r0, CS_BLK), :] + b_ref[...])
        p1, p2, p3 = _split3(lf)
        c = _dot(tri, p1) + _dot(tri, p2) + _dot(tri, p3) + carry
        c2 = c * LOG2E
        c_ref[pl.ds(r0, CS_BLK), :] = c2
        lane = lax.broadcasted_iota(jnp.int32, c2.shape, 1)
        packed = jnp.zeros_like(c2)
        for k, piece in enumerate(_split3(c2)):
            piece = jnp.where(lane < N_HEADS, piece.astype(F32), 0.0)
            packed = packed + (piece if k == 0 else pltpu.roll(piece, N_HEADS * k, axis=1))
        packed = packed.astype(BF16)
        extq_ref[pl.ds(r0, CS_BLK), :] = (_dot(packed, selq_ref[...]) + oneq_ref[...]).astype(BF16)
        extk_ref[pl.ds(r0, CS_BLK), :] = (_dot(packed, selk_ref[...]) + onek_ref[...]).astype(BF16)
        return c[CS_BLK - 1:CS_BLK, :]

    lax.fori_loop(0, n_blk, body, jnp.zeros((1, LANES), F32))


def _aug_constants():
    row = jnp.arange(LANES)
    col = jnp.arange(GROUP_W)
    piece, head = row // N_HEADS, row % N_HEADS
    blk, lane = col // HEAD_DIM, col % HEAD_DIM
    same = (head[:, None] == blk[None, :]) & (piece[:, None] < 3)
    selq = jnp.where(same & (lane[None, :] == 3 + piece[:, None]), 1.0, 0.0).astype(BF16)
    selk = jnp.where(same & (lane[None, :] == piece[:, None]), -1.0, 0.0).astype(BF16)
    oneq = jnp.where(lane < 3, 1.0, 0.0).astype(F32).reshape(1, GROUP_W)
    onek = jnp.where((lane >= 3) & (lane < 6), 1.0, 0.0).astype(F32).reshape(1, GROUP_W)
    return selq, selk, oneq, onek


def _forget_cs(f_logit, b_pad, tri_incl):
    b, s, _ = f_logit.shape
    const = lambda shape: pl.BlockSpec(shape, lambda i: (0, 0))
    return pl.pallas_call(
        _forget_cs_kernel,
        grid=(b,),
        in_specs=[
            pl.BlockSpec((None, s, LANES), lambda i: (i, 0, 0)),
            const((1, LANES)),
            const((CS_BLK, CS_BLK)),
            const((LANES, GROUP_W)),
            const((LANES, GROUP_W)),
            const((1, GROUP_W)),
            const((1, GROUP_W)),
        ],
        out_specs=[
            pl.BlockSpec((None, s, LANES), lambda i: (i, 0, 0)),
            pl.BlockSpec((None, s, GROUP_W), lambda i: (i, 0, 0)),
            pl.BlockSpec((None, s, GROUP_W), lambda i: (i, 0, 0)),
        ],
        out_shape=[
            jax.ShapeDtypeStruct((b, s, LANES), F32),
            jax.ShapeDtypeStruct((b, s, GROUP_W), BF16),
            jax.ShapeDtypeStruct((b, s, GROUP_W), BF16),
        ],
        compiler_params=_params(("arbitrary",)),
        name="forget_cs",
    )(f_logit, b_pad, tri_incl, *_aug_constants())


def _head_rmsnorm(o, g):
    return o * lax.rsqrt(jnp.mean(o * o, axis=-1, keepdims=True) + EPS) * g


def _col_reduce(x, pair, full):
    while x.shape[0] > 8:
        half = x.shape[0] // 2
        x = pair(x[:half], x[half:])
    return full(x, axis=0, keepdims=True)


def _store_head_out(o_ref, acc_t, g_col):
    o = acc_t * lax.rsqrt(jnp.mean(acc_t * acc_t, axis=0, keepdims=True) + EPS) * g_col
    o_ref[...] = jnp.transpose(o).astype(o_ref.dtype)


def _cast_rows_spec(w, n_steps, step_of):
    rows, cols = w.shape
    assert rows % n_steps == 0 and (rows // n_steps) % 16 == 0
    return pl.BlockSpec((rows // n_steps, cols), lambda bi, h, qi: (step_of(bi, h, qi), 0))


def _sb_kernel(q_ref, k_ref, vt_ref, g_ref, u2_ref, w_ref, o_ref, wb_ref):
    wb_ref[...] = w_ref[...].astype(BF16)

    qi = pl.program_id(2)
    u2 = u2_ref[...]
    n_sub = TQ_S // TK_S
    key = lax.broadcasted_iota(jnp.int32, (TK_S, TK_S), 0)
    qry = lax.broadcasted_iota(jnp.int32, (TK_S, TK_S), 1)
    causal = key < qry

    def stage_q(q, kb):
        ks = pl.multiple_of(kb * TK_S, TK_S)
        return _dot_nt(k_ref[pl.ds(ks, TK_S), :], q)

    def stage_e(z, diagonal):
        zn = jnp.minimum(z, 0.0)
        zp = zn - z
        sp = jnp.log2(1.0 + jnp.exp2(zn + zp))
        lb = zn - sp
        lk = zp - sp
        if diagonal:
            lk = jnp.where(causal, lk, 0.0)
        hi = lk.astype(BF16)
        lo = (lk - hi.astype(F32)).astype(BF16)
        return lb, jnp.concatenate([hi, lo], axis=0), _col_reduce(lk, jnp.add, jnp.sum)

    def stage_c(hilo):
        return _dot(u2, hilo)

    def stage_x(lb, rest, carry, diagonal):
        if diagonal:
            a = jnp.where(causal, jnp.exp2(lb + rest), 0.0)
        else:
            a = jnp.exp2(lb + (rest + carry))
        return a.astype(BF16)

    def stage_v(kb, a, gate=None):
        vt = vt_ref[:, pl.ds(pl.multiple_of(kb * TK_S, TK_S), TK_S)]
        if gate is not None:
            vt = vt * gate
        return _dot(vt, a)

    items = []
    for sub in range(n_sub):
        kd = qi * n_sub + sub
        q = q_ref[sub * TK_S:(sub + 1) * TK_S, :]
        items.append(dict(q=q, kb=kd, diagonal=True, gate=None))
        items.append(dict(q=q, kb=jnp.maximum(kd - 1, 0), diagonal=False,
                          gate=jnp.where(kd > 0, 1.0, 0.0).astype(BF16)))
    n_items = len(items)
    for t in range(n_items + 4):
        if 4 <= t:
            it = items[t - 4]
            it["pv"] = stage_v(it["kb"], it["a"], it["gate"])
        if 2 <= t < n_items + 2:
            it = items[t - 2]
            it["rest"] = stage_c(it["hilo"])
        if t < n_items:
            it = items[t]
            it["z"] = stage_q(it["q"], it["kb"])
        if 3 <= t < n_items + 3:
            it = items[t - 3]
            carry_in = None if it["diagonal"] else items[t - 4]["colsum"]
            it["a"] = stage_x(it["lb"], it["rest"], carry_in, it["diagonal"])
        if 1 <= t < n_items + 1:
            it = items[t - 1]
            it["lb"], it["hilo"], it["colsum"] = stage_e(it["z"], it["diagonal"])

    qs = [items[2 * r]["q"] for r in range(n_sub)]
    state = []
    for r in range(n_sub):
        d, p = items[2 * r], items[2 * r + 1]
        state += [d["colsum"] + p["colsum"], d["pv"] + p["pv"]]

    kd_last = qi * n_sub + n_sub - 1

    def cond(st):
        worst = st[1]
        for r in range(1, n_sub):
            worst = jnp.maximum(worst, st[1 + 2 * r])
        return jnp.logical_and(kd_last - 2 - st[0] >= 0, jnp.max(worst) > SB_DEAD_LOG2)

    def body(st):
        j, out = st[0], []
        for r in range(n_sub):
            kb = qi * n_sub + r - 2 - j
            gate = jnp.where(kb >= 0, 1.0, 0.0).astype(BF16)
            kb = jnp.maximum(kb, 0)
            lb, hilo, colsum = stage_e(stage_q(qs[r], kb), False)
            a = stage_x(lb, stage_c(hilo), st[1 + 2 * r], False)
            out += [st[1 + 2 * r] + colsum, st[2 + 2 * r] + stage_v(kb, a, gate)]
        return (j + 1, *out)

    st = lax.while_loop(cond, body, (0, *state))
    _store_head_out(o_ref, jnp.concatenate([st[2 + 2 * r] for r in range(n_sub)], axis=1), g_ref[...])


def _sb_attn(qk, v_t, g_col, u2, w_cast):
    b, s, _ = qk.shape
    assert TQ_S % TK_S == 0
    nq = s // TQ_S
    w_spec = _cast_rows_spec(w_cast, b * N_HEADS * nq, lambda bi, h, qi: (bi * N_HEADS + h) * nq + qi)
    return pl.pallas_call(
        _sb_kernel,
        grid=(b, N_HEADS, nq),
        in_specs=[
            pl.BlockSpec((None, TQ_S, HEAD_DIM), lambda bi, h, qi: (bi, qi, h)),
            pl.BlockSpec((None, s, HEAD_DIM), lambda bi, h, qi: (bi, 0, N_HEADS + h)),
            pl.BlockSpec((None, HEAD_DIM, s), lambda bi, h, qi: (h, 0, bi)),
            pl.BlockSpec((HEAD_DIM, 1), lambda bi, h, qi: (h, 0)),
            pl.BlockSpec((TK_S, 2 * TK_S), lambda bi, h, qi: (0, 0)),
            w_spec,
        ],
        out_specs=[pl.BlockSpec((None, TQ_S, HEAD_DIM), lambda bi, h, qi: (bi, qi, h)), w_spec],
        out_shape=[jax.ShapeDtypeStruct((b, s, GROUP_W), BF16),
                   jax.ShapeDtypeStruct(w_cast.shape, BF16)],
        compiler_params=_params(("arbitrary", "arbitrary", "arbitrary")),
        name="sb_attn",
    )(qk, qk, v_t, g_col, u2, w_cast)


def _fox_kernel(q_ref, k_ref, extq_ref, extk_ref, vt_ref, cq_ref, clast_ref, g_ref, wu_ref, wd_ref,
                o_ref, wub_ref, wdb_ref,
                kn_ref, z_ref, p_ref, m_ref, l_ref, a_ref, acc_ref):
    wub_ref[...] = wu_ref[...].astype(BF16)
    wdb_ref[...] = wd_ref[...].astype(BF16)

    bi, h, qi = pl.program_id(0), pl.program_id(1), pl.program_id(2)
    n_full = (qi * TQ_F) // TK_F

    def sq_norms_ub(rows):
        rf = rows.astype(F32)
        r2_up = (rf * rf * (1.0 + 2.0 ** -7)).astype(BF16)
        return _dot_nt(jnp.ones((8, HEAD_DIM), BF16), r2_up)[:1]

    @pl.when(qi == 0)
    def _():
        best = sq_norms_ub(k_ref[:KN_BLK, :])
        for r0 in range(KN_BLK, k_ref.shape[0], KN_BLK):
            best = jnp.maximum(best, sq_norms_ub(k_ref[r0:r0 + KN_BLK, :]))
        kn_ref[...] = jnp.broadcast_to(jnp.sqrt(jnp.max(best, axis=1, keepdims=True)), kn_ref.shape)

    q = q_ref[...]
    q_aug = jnp.concatenate([q, extq_ref[...]], axis=1)

    ub = jnp.sqrt(sq_norms_ub(q)) * kn_ref[:, :1] * NORM_SLACK + cq_ref[...]

    last_tile = vt_ref.shape[1] // TK_F - 1

    def kb_of(i):
        return jnp.clip(n_full - i, 0, last_tile)

    def qk_t(i):
        ks = pl.multiple_of(kb_of(i) * TK_F, TK_F)
        k_aug = jnp.concatenate([k_ref[pl.ds(ks, TK_F), :], extk_ref[pl.ds(ks, TK_F), :]], axis=1)
        return _dot_nt(k_aug, q_aug)

    def pv_t(i, slot):
        ks = pl.multiple_of(kb_of(i) * TK_F, TK_F)
        return _dot(vt_ref[:, pl.ds(ks, TK_F)], p_ref[slot])

    st = qk_t(0)
    z_ref[1] = qk_t(1)
    key = n_full * TK_F + lax.broadcasted_iota(jnp.int32, (TK_F, TQ_F), 0)
    qry = qi * TQ_F + lax.broadcasted_iota(jnp.int32, (TK_F, TQ_F), 1)
    st = jnp.where(key <= qry, st, -jnp.inf)
    m0 = _col_reduce(st, jnp.maximum, jnp.max)
    p0 = jnp.exp2(st - m0)
    p_ref[0] = p0.astype(BF16)
    m_ref[...] = m0
    l_ref[...] = _col_reduce(p0, jnp.add, jnp.sum)
    a_ref[...] = jnp.ones_like(m0)
    acc_ref[...] = jnp.zeros_like(acc_ref)

    gap = jnp.max(ub - m0)
    n_tiles = 1
    for i in range(clast_ref.shape[2]):
        alive = jnp.logical_and(i < n_full, gap - clast_ref[bi, h, i] > FOX_DEAD_LOG2)
        n_tiles = n_tiles + alive.astype(jnp.int32)

    def step(i, slot, active=None):
        z_ref[1 - slot] = qk_t(i + 1)
        acc_ref[...] = a_ref[...] * acc_ref[...] + pv_t(i - 1, 1 - slot)
        z = z_ref[slot]
        if active is not None:
            z = jnp.where(active, z, -jnp.inf)
        m = m_ref[...]
        m_new = jnp.maximum(m, _col_reduce(z, jnp.maximum, jnp.max))
        alpha = jnp.exp2(m - m_new)
        p = jnp.exp2(z_ref[slot] - m_new) if active is None else jnp.exp2(z - m_new)
        p_ref[slot] = p.astype(BF16)
        l_ref[...] = alpha * l_ref[...] + _col_reduce(p, jnp.add, jnp.sum)
        m_ref[...] = m_new
        a_ref[...] = alpha

    def body(j, _):
        i = 1 + 2 * j
        step(i, 1)
        step(i + 1, 0, active=i + 1 < n_tiles)
        return 0

    n_pairs = n_tiles // 2
    lax.fori_loop(0, n_pairs, body, 0)
    acc = a_ref[...] * acc_ref[...] + pv_t(2 * n_pairs, 0)
    _store_head_out(o_ref, acc / l_ref[...], g_ref[...])


def _fox_attn(qk, v_t, c, ext_q, ext_k, g_col, w_cast_a, w_cast_b):
    b, s, _ = qk.shape
    base = 2 * N_HEADS
    n_kt = s // TK_F
    nq = s // TQ_F
    assert TK_F % TQ_F == 0
    step_of = lambda bi, h, qi: (bi * N_HEADS + h) * nq + qi
    wa_spec = _cast_rows_spec(w_cast_a, b * N_HEADS * nq, step_of)
    wb_spec = _cast_rows_spec(w_cast_b, b * N_HEADS * nq, step_of)
    c_hs = jnp.transpose(c[:, :, :N_HEADS], (0, 2, 1))
    c_q = c_hs.reshape(b, N_HEADS, s // TQ_F, 1, TQ_F)
    c_last = c_hs.reshape(b, N_HEADS, n_kt, TK_F)[..., TK_F - 1]
    return pl.pallas_call(
        _fox_kernel,
        grid=(b, N_HEADS, s // TQ_F),
        in_specs=[
            pl.BlockSpec((None, TQ_F, HEAD_DIM), lambda bi, h, qi: (bi, qi, base + h)),
            pl.BlockSpec((None, s, HEAD_DIM), lambda bi, h, qi: (bi, 0, base + N_HEADS + h)),
            pl.BlockSpec((None, TQ_F, HEAD_DIM), lambda bi, h, qi: (bi, qi, h)),
            pl.BlockSpec((None, s, HEAD_DIM), lambda bi, h, qi: (bi, 0, h)),
            pl.BlockSpec((None, HEAD_DIM, s), lambda bi, h, qi: (N_HEADS + h, 0, bi)),
            pl.BlockSpec((None, None, None, 1, TQ_F), lambda bi, h, qi: (bi, h, qi, 0, 0)),
            pl.BlockSpec(memory_space=pltpu.SMEM),
            pl.BlockSpec((HEAD_DIM, 1), lambda bi, h, qi: (h, 0)),
            wa_spec,
            wb_spec,
        ],
        out_specs=[pl.BlockSpec((None, TQ_F, HEAD_DIM), lambda bi, h, qi: (bi, qi, h)), wa_spec, wb_spec],
        scratch_shapes=[
            pltpu.VMEM((1, LANES), F32),
            pltpu.VMEM((2, TK_F, TQ_F), F32),
            pltpu.VMEM((2, TK_F, TQ_F), BF16),
            pltpu.VMEM((1, TQ_F), F32),
            pltpu.VMEM((1, TQ_F), F32),
            pltpu.VMEM((1, TQ_F), F32),
            pltpu.VMEM((HEAD_DIM, TQ_F), F32),
        ],
        out_shape=[jax.ShapeDtypeStruct((b, s, GROUP_W), BF16),
                   jax.ShapeDtypeStruct(w_cast_a.shape, BF16),
                   jax.ShapeDtypeStruct(w_cast_b.shape, BF16)],
        compiler_params=_params(("arbitrary", "arbitrary", "arbitrary")),
        name="fox_attn",
    )(qk, qk, ext_q, ext_k, v_t, c_q, c_last, g_col, w_cast_a, w_cast_b)


def _out_proj_kernel(ms_ref, mf_ref, w_ref, x_ref, g_ref, x1_ref, h2_ref):
    acc = _dot(ms_ref[...], w_ref[:GROUP_W, :]) + _dot(mf_ref[...], w_ref[GROUP_W:, :])
    x1 = x_ref[...] + acc
    x1_ref[...] = x1
    var = jnp.mean(x1 * x1, axis=-1, keepdims=True)
    h2_ref[...] = (x1 * lax.rsqrt(var + EPS) * g_ref[...]).astype(BF16)


def _out_proj(mixed_sb, mixed_fox, w_out, x2, g_mlp):
    m, d = x2.shape
    return pl.pallas_call(
        _out_proj_kernel,
        grid=(m // TM_OUT,),
        in_specs=[
            pl.BlockSpec((TM_OUT, GROUP_W), lambda i: (i, 0)),
            pl.BlockSpec((TM_OUT, GROUP_W), lambda i: (i, 0)),
            pl.BlockSpec((2 * GROUP_W, d), lambda i: (0, 0)),
            pl.BlockSpec((TM_OUT, d), lambda i: (i, 0)),
            pl.BlockSpec((1, d), lambda i: (0, 0)),
        ],
        out_specs=[
            pl.BlockSpec((TM_OUT, d), lambda i: (i, 0)),
            pl.BlockSpec((TM_OUT, d), lambda i: (i, 0)),
        ],
        out_shape=[
            jax.ShapeDtypeStruct((m, d), F32),
            jax.ShapeDtypeStruct((m, d), BF16),
        ],
        compiler_params=_params(("arbitrary",)),
        name="out_proj",
    )(mixed_sb, mixed_fox, w_out, x2, g_mlp)


def _mlp_kernel(h_ref, x1_ref, wu_ref, wd_ref, g_ref, o_ref):
    f = pl.program_id(1)

    @pl.when(f == 0)
    def _():
        o_ref[...] = x1_ref[...]

    for c0 in range(0, wu_ref.shape[1], MLP_CHUNK):
        c1 = c0 + MLP_CHUNK
        u = jnp.maximum(_dot(h_ref[...], wu_ref[:, c0:c1]), 0.0)
        o_ref[...] += _dot((u * u).astype(BF16), wd_ref[c0:c1, :])

    @pl.when(f == pl.num_programs(1) - 1)
    def _():
        x2 = o_ref[...]
        var = jnp.mean(x2 * x2, axis=-1, keepdims=True)
        o_ref[...] = x2 * lax.rsqrt(var + EPS) * g_ref[...]


def _mlp(h2, x1, w_up, w_down, g_final):
    m, d = x1.shape
    dff = w_up.shape[1]
    return pl.pallas_call(
        _mlp_kernel,
        grid=(m // TM_MLP, dff // TF_MLP),
        in_specs=[
            pl.BlockSpec((TM_MLP, d), lambda i, f: (i, 0)),
            pl.BlockSpec((TM_MLP, d), lambda i, f: (i, 0)),
            pl.BlockSpec((d, TF_MLP), lambda i, f: (0, f)),
            pl.BlockSpec((TF_MLP, d), lambda i, f: (f, 0)),
            pl.BlockSpec((1, d), lambda i, f: (0, 0)),
        ],
        out_specs=pl.BlockSpec((TM_MLP, d), lambda i, f: (i, 0)),
        out_shape=jax.ShapeDtypeStruct((m, d), F32),
        compiler_params=_params(("arbitrary", "arbitrary"), VMEM_LIMIT_MLP),
        name="mlp",
    )(h2, x1, w_up, w_down, g_final)


def kernel(x, g_attn, w_in, b_f, g_out_sb, g_out_fox, w_out, g_mlp, w_up, w_down, g_final):
    b, s, d = x.shape
    n_qkv = 6 * GROUP_W
    assert s % TQ_S == 0 and s % TK_F == 0 and (b * s) % TM_IN == 0
    x2 = x.reshape(b * s, d)

    idx = jnp.arange(TK_S)
    u_excl = (idx[None, :] > idx[:, None]).astype(BF16)
    u2 = jnp.concatenate([u_excl, u_excl], axis=1)
    cidx = jnp.arange(CS_BLK)
    tri_incl = (cidx[None, :] <= cidx[:, None]).astype(BF16)

    gw = GROUP_W
    for l in range(g_attn.shape[0]):
        w_t = jnp.swapaxes(w_in[l], 0, 1)
        w_f_t = jnp.pad(w_t[n_qkv:], ((0, LANES - N_HEADS), (0, 0))).astype(BF16)
        b_pad = jnp.pad(b_f[l], (0, LANES - N_HEADS)).reshape(1, LANES)

        qk, v_t, f_logit = _in_proj(x2, g_attn[l].reshape(1, d), w_t, w_f_t)
        qk = qk.reshape(b, s, N_QK_GROUPS * gw)
        v_t = v_t.reshape(2 * N_HEADS, HEAD_DIM, b * s)
        c, ext_q, ext_k = _forget_cs(f_logit.reshape(b, s, LANES), b_pad, tri_incl)

        mixed_sb, w_out_b = _sb_attn(qk, v_t, g_out_sb[l].reshape(gw, 1), u2, w_out[l])
        mixed_fox, w_up_b, w_down_b = _fox_attn(qk, v_t, c, ext_q, ext_k, g_out_fox[l].reshape(gw, 1),
                                                w_up[l], w_down[l])

        x1, h2 = _out_proj(mixed_sb.reshape(b * s, GROUP_W), mixed_fox.reshape(b * s, GROUP_W),
                           w_out_b, x2, g_mlp[l].reshape(1, d))
        assert g_attn.shape[0] == 1
        x2 = _mlp(h2, x1, w_up_b, w_down_b, g_final.reshape(1, d))
    return x2.reshape(b, s, d)
```

```python
import functools

import jax
import jax.numpy as jnp
from jax import lax
from jax.experimental import pallas as pl
from jax.experimental.pallas import tpu as pltpu

F32 = jnp.float32
BF16 = jnp.bfloat16

HEAD_DIM = 128
N_HEADS = 8
GROUP_W = N_HEADS * HEAD_DIM
EPS = 1e-6
SCALE = HEAD_DIM ** -0.5
LOG2E = 1.4426950408889634
SB_DEAD_LOG2 = -152.0
FOX_DEAD_LOG2 = -152.0
NORM_SLACK = 1.01
M_INIT = -1e30
KN_BLK = 512
LANES = 128

VMEM_LIMIT = 56 * 1024 * 1024
VMEM_LIMIT_MLP = 58 * 1024 * 1024

TM_IN = 1024
TN_IN = 1024
IN_CHUNK = 256
TQ_S = 2048
TK_S = 256
TQ_F = 512
TK_F = 512
N_SUB_F = 4
CS_BLK = 256
TM_OUT = 512
TM_MLP = 512
TF_MLP = 2048
MLP_CHUNK = 1024


def _params(sem, vmem_limit=VMEM_LIMIT):
    return pltpu.CompilerParams(dimension_semantics=sem, vmem_limit_bytes=vmem_limit)


def _dot(a, b):
    return jnp.dot(a, b, preferred_element_type=F32)


def _dot_nt(a, b):
    return lax.dot_general(a, b, (((1,), (1,)), ((), ())), preferred_element_type=F32)


def _log_sigmoid(x):
    return jnp.minimum(x, 0.0) - jnp.log(1.0 + jnp.exp(-jnp.abs(x)))


N_QK_GROUPS = 4


def _in_proj_kernel(x_ref, g_ref, w_ref, wf_ref, qk_ref, vt_ref, f_ref, h_ref):
    j = pl.program_id(1)

    @pl.when(j == 0)
    def _():
        x = x_ref[...]
        var = jnp.mean(x * x, axis=-1, keepdims=True)
        h = (x * lax.rsqrt(var + EPS) * g_ref[...]).astype(BF16)
        h_ref[...] = h
        f_ref[...] = _dot_nt(h, wf_ref[...])

    @pl.when(j < N_QK_GROUPS)
    def _():
        scale = jnp.where((j == 0) | (j == 2), SCALE * LOG2E, 1.0).astype(F32)
        for c0 in range(0, TN_IN, IN_CHUNK):
            c1 = c0 + IN_CHUNK
            w = w_ref[c0:c1, :].astype(BF16)
            qk_ref[:, c0:c1] = (_dot_nt(h_ref[...], w) * scale).astype(BF16)

    @pl.when(j >= N_QK_GROUPS)
    def _():
        for c0 in range(0, TN_IN, IN_CHUNK):
            c1 = c0 + IN_CHUNK
            w = w_ref[c0:c1, :].astype(BF16)
            vt_ref[c0:c1, :] = _dot_nt(w, h_ref[...]).astype(BF16)


def _w_group(j):
    return jnp.where(j < 2, j, jnp.where(j < 4, j + 1, jnp.where(j == 4, 2, 5)))


def _in_proj(x2, g, w_qkv_t, w_f_t):
    m, d = x2.shape
    n = 6 * GROUP_W
    assert TN_IN == GROUP_W and w_qkv_t.shape[0] >= n
    return pl.pallas_call(
        _in_proj_kernel,
        grid=(m // TM_IN, n // TN_IN),
        in_specs=[
            pl.BlockSpec((TM_IN, d), lambda i, j: (i, 0)),
            pl.BlockSpec((1, d), lambda i, j: (0, 0)),
            pl.BlockSpec((TN_IN, d), lambda i, j: (_w_group(j), 0)),
            pl.BlockSpec((LANES, d), lambda i, j: (0, 0)),
        ],
        out_specs=[
            pl.BlockSpec((TM_IN, TN_IN), lambda i, j: (i, jnp.minimum(j, N_QK_GROUPS - 1))),
            pl.BlockSpec((TN_IN, TM_IN), lambda i, j: (jnp.maximum(j - N_QK_GROUPS, 0), i)),
            pl.BlockSpec((TM_IN, LANES), lambda i, j: (i, 0)),
        ],
        out_shape=[
            jax.ShapeDtypeStruct((m, N_QK_GROUPS * GROUP_W), BF16),
            jax.ShapeDtypeStruct((2 * GROUP_W, m), BF16),
            jax.ShapeDtypeStruct((m, LANES), F32),
        ],
        scratch_shapes=[pltpu.VMEM((TM_IN, d), BF16)],
        compiler_params=_params(("arbitrary", "arbitrary")),
        name="in_proj",
    )(x2, g, w_qkv_t, w_f_t)


def _split3(x):
    p1 = x.astype(BF16)
    r1 = x - p1.astype(F32)
    p2 = r1.astype(BF16)
    p3 = (r1 - p2.astype(F32)).astype(BF16)
    return p1, p2, p3


def _forget_cs_kernel(f_ref, b_ref, tri_ref, selq_ref, selk_ref, oneq_ref, onek_ref,
                      c_ref, extq_ref, extk_ref):
    n_blk = f_ref.shape[0] // CS_BLK
    tri = tri_ref[...]

    def body(i, carry):
        r0 = pl.multiple_of(i * CS_BLK, CS_BLK)
        lf = _log_sigmoid(f_ref[pl.ds(r0, CS_BLK), :] + b_ref[...])
        p1, p2, p3 = _split3(lf)
        c = _dot(tri, p1) + _dot(tri, p2) + _dot(tri, p3) + carry
        c2 = c * LOG2E
        c_ref[pl.ds(r0, CS_BLK), :] = c2
        lane = lax.broadcasted_iota(jnp.int32, c2.shape, 1)
        packed = jnp.zeros_like(c2)
        for k, piece in enumerate(_split3(c2)):
            piece = jnp.where(lane < N_HEADS, piece.astype(F32), 0.0)
            packed = packed + (piece if k == 0 else pltpu.roll(piece, N_HEADS * k, axis=1))
        packed = packed.astype(BF16)
        extq_ref[pl.ds(r0, CS_BLK), :] = (_dot(packed, selq_ref[...]) + oneq_ref[...]).astype(BF16)
        extk_ref[pl.ds(r0, CS_BLK), :] = (_dot(packed, selk_ref[...]) + onek_ref[...]).astype(BF16)
        return c[CS_BLK - 1:CS_BLK, :]

    lax.fori_loop(0, n_blk, body, jnp.zeros((1, LANES), F32))


def _aug_constants():
    row = jnp.arange(LANES)
    col = jnp.arange(GROUP_W)
    piece, head = row // N_HEADS, row % N_HEADS
    blk, lane = col // HEAD_DIM, col % HEAD_DIM
    same = (head[:, None] == blk[None, :]) & (piece[:, None] < 3)
    selq = jnp.where(same & (lane[None, :] == 3 + piece[:, None]), 1.0, 0.0).astype(BF16)
    selk = jnp.where(same & (lane[None, :] == piece[:, None]), -1.0, 0.0).astype(BF16)
    oneq = jnp.where(lane < 3, 1.0, 0.0).astype(F32).reshape(1, GROUP_W)
    onek = jnp.where((lane >= 3) & (lane < 6), 1.0, 0.0).astype(F32).reshape(1, GROUP_W)
    return selq, selk, oneq, onek


def _forget_cs(f_logit, b_pad, tri_incl):
    b, s, _ = f_logit.shape
    const = lambda shape: pl.BlockSpec(shape, lambda i: (0, 0))
    return pl.pallas_call(
        _forget_cs_kernel,
        grid=(b,),
        in_specs=[
            pl.BlockSpec((None, s, LANES), lambda i: (i, 0, 0)),
            const((1, LANES)),
            const((CS_BLK, CS_BLK)),
            const((LANES, GROUP_W)),
            const((LANES, GROUP_W)),
            const((1, GROUP_W)),
            const((1, GROUP_W)),
        ],
        out_specs=[
            pl.BlockSpec((None, s, LANES), lambda i: (i, 0, 0)),
            pl.BlockSpec((None, s, GROUP_W), lambda i: (i, 0, 0)),
            pl.BlockSpec((None, s, GROUP_W), lambda i: (i, 0, 0)),
        ],
        out_shape=[
            jax.ShapeDtypeStruct((b, s, LANES), F32),
            jax.ShapeDtypeStruct((b, s, GROUP_W), BF16),
            jax.ShapeDtypeStruct((b, s, GROUP_W), BF16),
        ],
        compiler_params=_params(("arbitrary",)),
        name="forget_cs",
    )(f_logit, b_pad, tri_incl, *_aug_constants())


def _head_rmsnorm(o, g):
    return o * lax.rsqrt(jnp.mean(o * o, axis=-1, keepdims=True) + EPS) * g


def _col_reduce(x, pair, full):
    while x.shape[0] > 8:
        half = x.shape[0] // 2
        x = pair(x[:half], x[half:])
    return full(x, axis=0, keepdims=True)


def _head_out(acc_t, g_col):
    o = acc_t * lax.rsqrt(jnp.mean(acc_t * acc_t, axis=0, keepdims=True) + EPS) * g_col
    return jnp.transpose(o)


def _store_head_out(o_ref, acc_t, g_col):
    o_ref[...] = _head_out(acc_t, g_col).astype(o_ref.dtype)


def _cast_rows_spec(w, n_steps, step_of):
    rows, cols = w.shape
    assert rows % n_steps == 0 and (rows // n_steps) % 16 == 0
    return pl.BlockSpec((rows // n_steps, cols), lambda bi, h, qi: (step_of(bi, h, qi), 0))


def _sb_kernel(q_ref, k_ref, vt_ref, g_ref, u2_ref, w_ref, o_ref, wb_ref):
    wb_ref[...] = w_ref[...].astype(BF16)

    qi = pl.program_id(2)
    u2 = u2_ref[...]
    n_sub = TQ_S // TK_S
    key = lax.broadcasted_iota(jnp.int32, (TK_S, TK_S), 0)
    qry = lax.broadcasted_iota(jnp.int32, (TK_S, TK_S), 1)
    causal = key < qry

    def stage_q(q, kb):
        ks = pl.multiple_of(kb * TK_S, TK_S)
        return _dot_nt(k_ref[pl.ds(ks, TK_S), :], q)

    def stage_e(z, diagonal):
        zn = jnp.minimum(z, 0.0)
        zp = zn - z
        sp = jnp.log2(1.0 + jnp.exp2(zn + zp))
        lb = zn - sp
        lk = zp - sp
        if diagonal:
            lk = jnp.where(causal, lk, 0.0)
        hi = lk.astype(BF16)
        lo = (lk - hi.astype(F32)).astype(BF16)
        return lb, jnp.concatenate([hi, lo], axis=0), _col_reduce(lk, jnp.add, jnp.sum)

    def stage_c(hilo):
        return _dot(u2, hilo)

    def stage_x(lb, rest, carry, diagonal):
        if diagonal:
            a = jnp.where(causal, jnp.exp2(lb + rest), 0.0)
        else:
            a = jnp.exp2(lb + (rest + carry))
        return a.astype(BF16)

    def stage_v(kb, a, gate=None):
        vt = vt_ref[:, pl.ds(pl.multiple_of(kb * TK_S, TK_S), TK_S)]
        if gate is not None:
            vt = vt * gate
        return _dot(vt, a)

    items = []
    for sub in range(n_sub):
        kd = qi * n_sub + sub
        q = q_ref[sub * TK_S:(sub + 1) * TK_S, :]
        items.append(dict(q=q, kb=kd, diagonal=True, gate=None))
        items.append(dict(q=q, kb=jnp.maximum(kd - 1, 0), diagonal=False,
                          gate=jnp.where(kd > 0, 1.0, 0.0).astype(BF16)))
    n_items = len(items)
    for t in range(n_items + 4):
        if 4 <= t:
            it = items[t - 4]
            it["pv"] = stage_v(it["kb"], it["a"], it["gate"])
        if 2 <= t < n_items + 2:
            it = items[t - 2]
            it["rest"] = stage_c(it["hilo"])
        if t < n_items:
            it = items[t]
            it["z"] = stage_q(it["q"], it["kb"])
        if 3 <= t < n_items + 3:
            it = items[t - 3]
            carry_in = None if it["diagonal"] else items[t - 4]["colsum"]
            it["a"] = stage_x(it["lb"], it["rest"], carry_in, it["diagonal"])
        if 1 <= t < n_items + 1:
            it = items[t - 1]
            it["lb"], it["hilo"], it["colsum"] = stage_e(it["z"], it["diagonal"])

    qs = [items[2 * r]["q"] for r in range(n_sub)]
    state = []
    for r in range(n_sub):
        d, p = items[2 * r], items[2 * r + 1]
        state += [d["colsum"] + p["colsum"], d["pv"] + p["pv"]]

    kd_last = qi * n_sub + n_sub - 1

    def cond(st):
        worst = st[1]
        for r in range(1, n_sub):
            worst = jnp.maximum(worst, st[1 + 2 * r])
        return jnp.logical_and(kd_last - 2 - st[0] >= 0, jnp.max(worst) > SB_DEAD_LOG2)

    def body(st):
        j, out = st[0], []
        for r in range(n_sub):
            kb = qi * n_sub + r - 2 - j
            gate = jnp.where(kb >= 0, 1.0, 0.0).astype(BF16)
            kb = jnp.maximum(kb, 0)
            lb, hilo, colsum = stage_e(stage_q(qs[r], kb), False)
            a = stage_x(lb, stage_c(hilo), st[1 + 2 * r], False)
            out += [st[1 + 2 * r] + colsum, st[2 + 2 * r] + stage_v(kb, a, gate)]
        return (j + 1, *out)

    st = lax.while_loop(cond, body, (0, *state))
    _store_head_out(o_ref, jnp.concatenate([st[2 + 2 * r] for r in range(n_sub)], axis=1), g_ref[...])


def _sb_attn(qk, v_t, g_col, u2, w_cast):
    b, s, _ = qk.shape
    assert TQ_S % TK_S == 0
    nq = s // TQ_S
    w_spec = _cast_rows_spec(w_cast, b * N_HEADS * nq, lambda bi, h, qi: (bi * N_HEADS + h) * nq + qi)
    return pl.pallas_call(
        _sb_kernel,
        grid=(b, N_HEADS, nq),
        in_specs=[
            pl.BlockSpec((None, TQ_S, HEAD_DIM), lambda bi, h, qi: (bi, qi, h)),
            pl.BlockSpec((None, s, HEAD_DIM), lambda bi, h, qi: (bi, 0, N_HEADS + h)),
            pl.BlockSpec((None, HEAD_DIM, s), lambda bi, h, qi: (h, 0, bi)),
            pl.BlockSpec((HEAD_DIM, 1), lambda bi, h, qi: (h, 0)),
            pl.BlockSpec((TK_S, 2 * TK_S), lambda bi, h, qi: (0, 0)),
            w_spec,
        ],
        out_specs=[pl.BlockSpec((None, TQ_S, HEAD_DIM), lambda bi, h, qi: (bi, qi, h)), w_spec],
        out_shape=[jax.ShapeDtypeStruct((b, s, GROUP_W), BF16),
                   jax.ShapeDtypeStruct(w_cast.shape, BF16)],
        compiler_params=_params(("arbitrary", "arbitrary", "arbitrary")),
        name="sb_attn",
    )(qk, qk, v_t, g_col, u2, w_cast)


def _fox_kernel(q_ref, k_ref, extq_ref, extk_ref, vt_ref, cq_ref, clast_ref, g_ref, wu_ref, wd_ref,
                o_ref, wub_ref, wdb_ref,
                kn_ref, z_ref, p_ref, m_ref, l_ref, a_ref, acc_ref):
    wub_ref[...] = wu_ref[...].astype(BF16)
    wdb_ref[...] = wd_ref[...].astype(BF16)

    bi, h, qs = pl.program_id(0), pl.program_id(1), pl.program_id(2)
    last_tile = vt_ref.shape[1] // TK_F - 1

    def sq_norms_ub(rows):
        rf = rows.astype(F32)
        r2_up = (rf * rf * (1.0 + 2.0 ** -7)).astype(BF16)
        return _dot_nt(jnp.ones((8, HEAD_DIM), BF16), r2_up)[:1]

    @pl.when(qs == 0)
    def _():
        best = sq_norms_ub(k_ref[:KN_BLK, :])
        for r0 in range(KN_BLK, k_ref.shape[0], KN_BLK):
            best = jnp.maximum(best, sq_norms_ub(k_ref[r0:r0 + KN_BLK, :]))
        kn_ref[...] = jnp.broadcast_to(jnp.sqrt(jnp.max(best, axis=1, keepdims=True)), kn_ref.shape)

    def make_block(sub):
        qi = qs * N_SUB_F + sub
        rows = slice(sub * TQ_F, (sub + 1) * TQ_F)
        n_full = (qi * TQ_F) // TK_F
        q = q_ref[rows, :]
        q_aug = jnp.concatenate([q, extq_ref[rows, :]], axis=1)
        ub = jnp.sqrt(sq_norms_ub(q)) * kn_ref[:, :1] * NORM_SLACK + cq_ref[:, rows]

        def kb_of(i):
            return jnp.clip(n_full - i, 0, last_tile)

        def qk_t(i):
            ks = pl.multiple_of(kb_of(i) * TK_F, TK_F)
            k_aug = jnp.concatenate([k_ref[pl.ds(ks, TK_F), :], extk_ref[pl.ds(ks, TK_F), :]], axis=1)
            return _dot_nt(k_aug, q_aug)

        def pv_t(i, slot):
            ks = pl.multiple_of(kb_of(i) * TK_F, TK_F)
            return _dot(vt_ref[:, pl.ds(ks, TK_F)], p_ref[sub, slot])

        def prologue():
            st = qk_t(0)
            z_ref[sub, 1] = qk_t(1)
            key = n_full * TK_F + lax.broadcasted_iota(jnp.int32, (TK_F, TQ_F), 0)
            qry = qi * TQ_F + lax.broadcasted_iota(jnp.int32, (TK_F, TQ_F), 1)
            st = jnp.where(key <= qry, st, -jnp.inf)
            m0 = _col_reduce(st, jnp.maximum, jnp.max)
            p0 = jnp.exp2(st - m0)
            p_ref[sub, 0] = p0.astype(BF16)
            m_ref[sub] = m0
            l_ref[sub] = _col_reduce(p0, jnp.add, jnp.sum)
            a_ref[sub] = jnp.ones_like(m0)
            acc_ref[sub] = jnp.zeros(acc_ref.shape[1:], F32)
            gap = jnp.max(ub - m0)
            n_tiles = 1
            for i in range(clast_ref.shape[2]):
                alive = jnp.logical_and(i < n_full, gap - clast_ref[bi, h, i] > FOX_DEAD_LOG2)
                n_tiles = n_tiles + alive.astype(jnp.int32)
            return n_tiles

        def step(i, slot, n_tiles=None):
            z_ref[sub, 1 - slot] = qk_t(i + 1)
            acc_ref[sub] = a_ref[sub] * acc_ref[sub] + pv_t(i - 1, 1 - slot)
            z = z_ref[sub, slot]
            if n_tiles is not None:
                z = jnp.where(i < n_tiles, z, -jnp.inf)
            m = m_ref[sub]
            m_new = jnp.maximum(m, _col_reduce(z, jnp.maximum, jnp.max))
            alpha = jnp.exp2(m - m_new)
            p = jnp.exp2(z_ref[sub, slot] - m_new) if n_tiles is None else jnp.exp2(z - m_new)
            p_ref[sub, slot] = p.astype(BF16)
            l_ref[sub] = alpha * l_ref[sub] + _col_reduce(p, jnp.add, jnp.sum)
            m_ref[sub] = m_new
            a_ref[sub] = alpha

        def walk(n_tiles):
            def body(j, _):
                i = 1 + 2 * j
                step(i, 1)
                step(i + 1, 0, n_tiles)
                return 0

            lax.fori_loop(0, n_tiles // 2, body, 0)

        def epilogue(n_tiles):
            acc = a_ref[sub] * acc_ref[sub] + pv_t(2 * (n_tiles // 2), 0)
            o_ref[rows, :] = _head_out(acc / l_ref[sub], g_ref[...]).astype(o_ref.dtype)

        return prologue, walk, epilogue

    blocks = [make_block(sub) for sub in range(N_SUB_F)]
    n_tiles = [prologue() for prologue, _, _ in blocks]
    for (_, walk, _), n in zip(blocks, n_tiles):
        walk(n)
    for (_, _, epilogue), n in zip(blocks, n_tiles):
        epilogue(n)


def _fox_attn(qk, v_t, c, ext_q, ext_k, g_col, w_cast_a, w_cast_b):
    b, s, _ = qk.shape
    base = 2 * N_HEADS
    n_kt = s // TK_F
    tq_step = N_SUB_F * TQ_F
    nq = s // tq_step
    assert TK_F % TQ_F == 0 and s % tq_step == 0
    step_of = lambda bi, h, qi: (bi * N_HEADS + h) * nq + qi
    wa_spec = _cast_rows_spec(w_cast_a, b * N_HEADS * nq, step_of)
    wb_spec = _cast_rows_spec(w_cast_b, b * N_HEADS * nq, step_of)
    c_hs = jnp.transpose(c[:, :, :N_HEADS], (0, 2, 1))
    c_q = c_hs.reshape(b, N_HEADS, nq, 1, tq_step)
    c_last = c_hs.reshape(b, N_HEADS, n_kt, TK_F)[..., TK_F - 1]
    return pl.pallas_call(
        _fox_kernel,
        grid=(b, N_HEADS, nq),
        in_specs=[
            pl.BlockSpec((None, tq_step, HEAD_DIM), lambda bi, h, qi: (bi, qi, base + h)),
            pl.BlockSpec((None, s, HEAD_DIM), lambda bi, h, qi: (bi, 0, base + N_HEADS + h)),
            pl.BlockSpec((None, tq_step, HEAD_DIM), lambda bi, h, qi: (bi, qi, h)),
            pl.BlockSpec((None, s, HEAD_DIM), lambda bi, h, qi: (bi, 0, h)),
            pl.BlockSpec((None, HEAD_DIM, s), lambda bi, h, qi: (N_HEADS + h, 0, bi)),
            pl.BlockSpec((None, None, None, 1, tq_step), lambda bi, h, qi: (bi, h, qi, 0, 0)),
            pl.BlockSpec(memory_space=pltpu.SMEM),
            pl.BlockSpec((HEAD_DIM, 1), lambda bi, h, qi: (h, 0)),
            wa_spec,
            wb_spec,
        ],
        out_specs=[pl.BlockSpec((None, tq_step, HEAD_DIM), lambda bi, h, qi: (bi, qi, h)), wa_spec, wb_spec],
        scratch_shapes=[
            pltpu.VMEM((1, LANES), F32),
            pltpu.VMEM((N_SUB_F, 2, TK_F, TQ_F), F32),
            pltpu.VMEM((N_SUB_F, 2, TK_F, TQ_F), BF16),
            pltpu.VMEM((N_SUB_F, 1, TQ_F), F32),
            pltpu.VMEM((N_SUB_F, 1, TQ_F), F32),
            pltpu.VMEM((N_SUB_F, 1, TQ_F), F32),
            pltpu.VMEM((N_SUB_F, HEAD_DIM, TQ_F), F32),
        ],
        out_shape=[jax.ShapeDtypeStruct((b, s, GROUP_W), BF16),
                   jax.ShapeDtypeStruct(w_cast_a.shape, BF16),
                   jax.ShapeDtypeStruct(w_cast_b.shape, BF16)],
        compiler_params=_params(("arbitrary", "arbitrary", "arbitrary")),
        name="fox_attn",
    )(qk, qk, ext_q, ext_k, v_t, c_q, c_last, g_col, w_cast_a, w_cast_b)


def _out_proj_kernel(ms_ref, mf_ref, w_ref, x_ref, g_ref, x1_ref, h2_ref):
    acc = _dot(ms_ref[...], w_ref[:GROUP_W, :]) + _dot(mf_ref[...], w_ref[GROUP_W:, :])
    x1 = x_ref[...] + acc
    x1_ref[...] = x1
    var = jnp.mean(x1 * x1, axis=-1, keepdims=True)
    h2_ref[...] = (x1 * lax.rsqrt(var + EPS) * g_ref[...]).astype(BF16)


def _out_proj(mixed_sb, mixed_fox, w_out, x2, g_mlp):
    m, d = x2.shape
    return pl.pallas_call(
        _out_proj_kernel,
        grid=(m // TM_OUT,),
        in_specs=[
            pl.BlockSpec((TM_OUT, GROUP_W), lambda i: (i, 0)),
            pl.BlockSpec((TM_OUT, GROUP_W), lambda i: (i, 0)),
            pl.BlockSpec((2 * GROUP_W, d), lambda i: (0, 0)),
            pl.BlockSpec((TM_OUT, d), lambda i: (i, 0)),
            pl.BlockSpec((1, d), lambda i: (0, 0)),
        ],
        out_specs=[
            pl.BlockSpec((TM_OUT, d), lambda i: (i, 0)),
            pl.BlockSpec((TM_OUT, d), lambda i: (i, 0)),
        ],
        out_shape=[
            jax.ShapeDtypeStruct((m, d), F32),
            jax.ShapeDtypeStruct((m, d), BF16),
        ],
        compiler_params=_params(("arbitrary",)),
        name="out_proj",
    )(mixed_sb, mixed_fox, w_out, x2, g_mlp)


def _mlp_kernel(h_ref, x1_ref, wu_ref, wd_ref, g_ref, o_ref):
    f = pl.program_id(1)

    @pl.when(f == 0)
    def _():
        o_ref[...] = x1_ref[...]

    for c0 in range(0, wu_ref.shape[1], MLP_CHUNK):
        c1 = c0 + MLP_CHUNK
        u = jnp.maximum(_dot(h_ref[...], wu_ref[:, c0:c1]), 0.0)
        o_ref[...] += _dot((u * u).astype(BF16), wd_ref[c0:c1, :])

    @pl.when(f == pl.num_programs(1) - 1)
    def _():
        x2 = o_ref[...]
        var = jnp.mean(x2 * x2, axis=-1, keepdims=True)
        o_ref[...] = x2 * lax.rsqrt(var + EPS) * g_ref[...]


def _mlp(h2, x1, w_up, w_down, g_final):
    m, d = x1.shape
    dff = w_up.shape[1]
    return pl.pallas_call(
        _mlp_kernel,
        grid=(m // TM_MLP, dff // TF_MLP),
        in_specs=[
            pl.BlockSpec((TM_MLP, d), lambda i, f: (i, 0)),
            pl.BlockSpec((TM_MLP, d), lambda i, f: (i, 0)),
            pl.BlockSpec((d, TF_MLP), lambda i, f: (0, f)),
            pl.BlockSpec((TF_MLP, d), lambda i, f: (f, 0)),
            pl.BlockSpec((1, d), lambda i, f: (0, 0)),
        ],
        out_specs=pl.BlockSpec((TM_MLP, d), lambda i, f: (i, 0)),
        out_shape=jax.ShapeDtypeStruct((m, d), F32),
        compiler_params=_params(("arbitrary", "arbitrary"), VMEM_LIMIT_MLP),
        name="mlp",
    )(h2, x1, w_up, w_down, g_final)


def kernel(x, g_attn, w_in, b_f, g_out_sb, g_out_fox, w_out, g_mlp, w_up, w_down, g_final):
    b, s, d = x.shape
    n_qkv = 6 * GROUP_W
    assert s % TQ_S == 0 and s % TK_F == 0 and (b * s) % TM_IN == 0
    x2 = x.reshape(b * s, d)

    idx = jnp.arange(TK_S)
    u_excl = (idx[None, :] > idx[:, None]).astype(BF16)
    u2 = jnp.concatenate([u_excl, u_excl], axis=1)
    cidx = jnp.arange(CS_BLK)
    tri_incl = (cidx[None, :] <= cidx[:, None]).astype(BF16)

    gw = GROUP_W
    for l in range(g_attn.shape[0]):
        w_t = jnp.swapaxes(w_in[l], 0, 1)
        w_f_t = jnp.pad(w_t[n_qkv:], ((0, LANES - N_HEADS), (0, 0))).astype(BF16)
        b_pad = jnp.pad(b_f[l], (0, LANES - N_HEADS)).reshape(1, LANES)

        qk, v_t, f_logit = _in_proj(x2, g_attn[l].reshape(1, d), w_t, w_f_t)
        qk = qk.reshape(b, s, N_QK_GROUPS * gw)
        v_t = v_t.reshape(2 * N_HEADS, HEAD_DIM, b * s)
        c, ext_q, ext_k = _forget_cs(f_logit.reshape(b, s, LANES), b_pad, tri_incl)

        mixed_sb, w_out_b = _sb_attn(qk, v_t, g_out_sb[l].reshape(gw, 1), u2, w_out[l])
        mixed_fox, w_up_b, w_down_b = _fox_attn(qk, v_t, c, ext_q, ext_k, g_out_fox[l].reshape(gw, 1),
                                                w_up[l], w_down[l])

        x1, h2 = _out_proj(mixed_sb.reshape(b * s, GROUP_W), mixed_fox.reshape(b * s, GROUP_W),
                           w_out_b, x2, g_mlp[l].reshape(1, d))
        assert g_attn.shape[0] == 1
        x2 = _mlp(h2, x1, w_up_b, w_down_b, g_final.reshape(1, d))
    return x2.reshape(b, s, d)
```

```python
import functools

import jax
import jax.numpy as jnp
from jax import lax
from jax.experimental import pallas as pl
from jax.experimental.pallas import tpu as pltpu

F32 = jnp.float32
BF16 = jnp.bfloat16

HEAD_DIM = 128
N_HEADS = 8
GROUP_W = N_HEADS * HEAD_DIM
EPS = 1e-6
SCALE = HEAD_DIM ** -0.5
LOG2E = 1.4426950408889634
SB_DEAD_LOG2 = -152.0
FOX_DEAD_LOG2 = -152.0
NORM_SLACK = 1.01
M_INIT = -1e30
KN_BLK = 512
LANES = 128

VMEM_LIMIT = 56 * 1024 * 1024
VMEM_LIMIT_MLP = 58 * 1024 * 1024

TM_IN = 1024
TN_IN = 1024
IN_CHUNK = 256
TQ_S = 2048
TK_S = 256
TQ_F = 512
TK_F = 512
N_SUB_F = 4
CS_BLK = 256
TM_OUT = 512
TM_MLP = 512
TF_MLP = 2048
MLP_CHUNK = 1024


def _params(sem, vmem_limit=VMEM_LIMIT):
    return pltpu.CompilerParams(dimension_semantics=sem, vmem_limit_bytes=vmem_limit)


def _dot(a, b):
    return jnp.dot(a, b, preferred_element_type=F32)


def _dot_nt(a, b):
    return lax.dot_general(a, b, (((1,), (1,)), ((), ())), preferred_element_type=F32)


def _log_sigmoid(x):
    return jnp.minimum(x, 0.0) - jnp.log(1.0 + jnp.exp(-jnp.abs(x)))


N_QK_GROUPS = 4


def _in_proj_kernel(x_ref, g_ref, w_ref, wf_ref, qk_ref, vt_ref, f_ref, h_ref):
    j = pl.program_id(1)

    @pl.when(j == 0)
    def _():
        x = x_ref[...]
        var = jnp.mean(x * x, axis=-1, keepdims=True)
        h = (x * lax.rsqrt(var + EPS) * g_ref[...]).astype(BF16)
        h_ref[...] = h
        f_ref[...] = _dot_nt(h, wf_ref[...])

    @pl.when(j < N_QK_GROUPS)
    def _():
        scale = jnp.where((j == 0) | (j == 2), SCALE * LOG2E, 1.0).astype(F32)
        for c0 in range(0, TN_IN, IN_CHUNK):
            c1 = c0 + IN_CHUNK
            w = w_ref[c0:c1, :].astype(BF16)
            qk_ref[:, c0:c1] = (_dot_nt(h_ref[...], w) * scale).astype(BF16)

    @pl.when(j >= N_QK_GROUPS)
    def _():
        for c0 in range(0, TN_IN, IN_CHUNK):
            c1 = c0 + IN_CHUNK
            w = w_ref[c0:c1, :].astype(BF16)
            vt_ref[c0:c1, :] = _dot_nt(w, h_ref[...]).astype(BF16)


def _w_group(j):
    return jnp.where(j < 2, j, jnp.where(j < 4, j + 1, jnp.where(j == 4, 2, 5)))


def _in_proj(x2, g, w_qkv_t, w_f_t):
    m, d = x2.shape
    n = 6 * GROUP_W
    assert TN_IN == GROUP_W and w_qkv_t.shape[0] >= n
    return pl.pallas_call(
        _in_proj_kernel,
        grid=(m // TM_IN, n // TN_IN),
        in_specs=[
            pl.BlockSpec((TM_IN, d), lambda i, j: (i, 0)),
            pl.BlockSpec((1, d), lambda i, j: (0, 0)),
            pl.BlockSpec((TN_IN, d), lambda i, j: (_w_group(j), 0)),
            pl.BlockSpec((LANES, d), lambda i, j: (0, 0)),
        ],
        out_specs=[
            pl.BlockSpec((TM_IN, TN_IN), lambda i, j: (i, jnp.minimum(j, N_QK_GROUPS - 1))),
            pl.BlockSpec((TN_IN, TM_IN), lambda i, j: (jnp.maximum(j - N_QK_GROUPS, 0), i)),
            pl.BlockSpec((TM_IN, LANES), lambda i, j: (i, 0)),
        ],
        out_shape=[
            jax.ShapeDtypeStruct((m, N_QK_GROUPS * GROUP_W), BF16),
            jax.ShapeDtypeStruct((2 * GROUP_W, m), BF16),
            jax.ShapeDtypeStruct((m, LANES), F32),
        ],
        scratch_shapes=[pltpu.VMEM((TM_IN, d), BF16)],
        compiler_params=_params(("arbitrary", "arbitrary")),
        name="in_proj",
    )(x2, g, w_qkv_t, w_f_t)


def _split3(x):
    p1 = x.astype(BF16)
    r1 = x - p1.astype(F32)
    p2 = r1.astype(BF16)
    p3 = (r1 - p2.astype(F32)).astype(BF16)
    return p1, p2, p3


def _forget_cs_kernel(f_ref, b_ref, tri_ref, selq_ref, selk_ref, oneq_ref, onek_ref,
                      c_ref, extq_ref, extk_ref):
    n_blk = f_ref.shape[0] // CS_BLK
    tri = tri_ref[...]

    def body(i, carry):
        r0 = pl.multiple_of(i * CS_BLK, CS_BLK)
        lf = _log_sigmoid(f_ref[pl.ds(r0, CS_BLK), :] + b_ref[...])
        p1, p2, p3 = _split3(lf)
        c = _dot(tri, p1) + _dot(tri, p2) + _dot(tri, p3) + carry
        c2 = c * LOG2E
        c_ref[pl.ds(r0, CS_BLK), :] = c2
        lane = lax.broadcasted_iota(jnp.int32, c2.shape, 1)
        packed = jnp.zeros_like(c2)
        for k, piece in enumerate(_split3(c2)):
            piece = jnp.where(lane < N_HEADS, piece.astype(F32), 0.0)
            packed = packed + (piece if k == 0 else pltpu.roll(piece, N_HEADS * k, axis=1))
        packed = packed.astype(BF16)
        extq_ref[pl.ds(r0, CS_BLK), :] = (_dot(packed, selq_ref[...]) + oneq_ref[...]).astype(BF16)
        extk_ref[pl.ds(r0, CS_BLK), :] = (_dot(packed, selk_ref[...]) + onek_ref[...]).astype(BF16)
        return c[CS_BLK - 1:CS_BLK, :]

    lax.fori_loop(0, n_blk, body, jnp.zeros((1, LANES), F32))


def _aug_constants():
    row = jnp.arange(LANES)
    col = jnp.arange(GROUP_W)
    piece, head = row // N_HEADS, row % N_HEADS
    blk, lane = col // HEAD_DIM, col % HEAD_DIM
    same = (head[:, None] == blk[None, :]) & (piece[:, None] < 3)
    selq = jnp.where(same & (lane[None, :] == 3 + piece[:, None]), 1.0, 0.0).astype(BF16)
    selk = jnp.where(same & (lane[None, :] == piece[:, None]), -1.0, 0.0).astype(BF16)
    oneq = jnp.where(lane < 3, 1.0, 0.0).astype(F32).reshape(1, GROUP_W)
    onek = jnp.where((lane >= 3) & (lane < 6), 1.0, 0.0).astype(F32).reshape(1, GROUP_W)
    return selq, selk, oneq, onek


def _forget_cs(f_logit, b_pad, tri_incl):
    b, s, _ = f_logit.shape
    const = lambda shape: pl.BlockSpec(shape, lambda i: (0, 0))
    return pl.pallas_call(
        _forget_cs_kernel,
        grid=(b,),
        in_specs=[
            pl.BlockSpec((None, s, LANES), lambda i: (i, 0, 0)),
            const((1, LANES)),
            const((CS_BLK, CS_BLK)),
            const((LANES, GROUP_W)),
            const((LANES, GROUP_W)),
            const((1, GROUP_W)),
            const((1, GROUP_W)),
        ],
        out_specs=[
            pl.BlockSpec((None, s, LANES), lambda i: (i, 0, 0)),
            pl.BlockSpec((None, s, GROUP_W), lambda i: (i, 0, 0)),
            pl.BlockSpec((None, s, GROUP_W), lambda i: (i, 0, 0)),
        ],
        out_shape=[
            jax.ShapeDtypeStruct((b, s, LANES), F32),
            jax.ShapeDtypeStruct((b, s, GROUP_W), BF16),
            jax.ShapeDtypeStruct((b, s, GROUP_W), BF16),
        ],
        compiler_params=_params(("arbitrary",)),
        name="forget_cs",
    )(f_logit, b_pad, tri_incl, *_aug_constants())


def _head_rmsnorm(o, g):
    return o * lax.rsqrt(jnp.mean(o * o, axis=-1, keepdims=True) + EPS) * g


def _col_reduce(x, pair, full):
    while x.shape[0] > 8:
        half = x.shape[0] // 2
        x = pair(x[:half], x[half:])
    return full(x, axis=0, keepdims=True)


def _head_out(acc_t, g_col):
    o = acc_t * lax.rsqrt(jnp.mean(acc_t * acc_t, axis=0, keepdims=True) + EPS) * g_col
    return jnp.transpose(o)


def _store_head_out(o_ref, acc_t, g_col):
    o_ref[...] = _head_out(acc_t, g_col).astype(o_ref.dtype)


def _cast_rows_spec(w, n_steps, step_of):
    rows, cols = w.shape
    assert rows % n_steps == 0 and (rows // n_steps) % 16 == 0
    return pl.BlockSpec((rows // n_steps, cols), lambda bi, h, qi: (step_of(bi, h, qi), 0))


def _sb_kernel(q_ref, k_ref, vt_ref, g_ref, u2_ref, w_ref, o_ref, wb_ref):
    wb_ref[...] = w_ref[...].astype(BF16)

    qi = pl.program_id(2)
    u2 = u2_ref[...]
    n_sub = TQ_S // TK_S
    key = lax.broadcasted_iota(jnp.int32, (TK_S, TK_S), 0)
    qry = lax.broadcasted_iota(jnp.int32, (TK_S, TK_S), 1)
    causal = key < qry

    def stage_q(q, kb):
        ks = pl.multiple_of(kb * TK_S, TK_S)
        return _dot_nt(k_ref[pl.ds(ks, TK_S), :], q)

    def stage_e(z, diagonal):
        zn = jnp.minimum(z, 0.0)
        zp = zn - z
        sp = jnp.log2(1.0 + jnp.exp2(zn + zp))
        lb = zn - sp
        lk = zp - sp
        if diagonal:
            lk = jnp.where(causal, lk, 0.0)
        hi = lk.astype(BF16)
        lo = (lk - hi.astype(F32)).astype(BF16)
        return lb, jnp.concatenate([hi, lo], axis=0), lk[:1, :]

    def stage_c(hilo, lk_first):
        rest = _dot(u2, hilo)
        return rest, rest[:1, :] + lk_first

    def stage_x(lb, rest, carry, diagonal):
        if diagonal:
            a = jnp.where(causal, jnp.exp2(lb + rest), 0.0)
        else:
            a = jnp.exp2(lb + (rest + carry))
        return a.astype(BF16)

    def stage_v(kb, a, gate=None):
        vt = vt_ref[:, pl.ds(pl.multiple_of(kb * TK_S, TK_S), TK_S)]
        if gate is not None:
            vt = vt * gate
        return _dot(vt, a)

    items = []
    for sub in range(n_sub):
        kd = qi * n_sub + sub
        q = q_ref[sub * TK_S:(sub + 1) * TK_S, :]
        items.append(dict(q=q, kb=kd, diagonal=True, gate=None))
        items.append(dict(q=q, kb=jnp.maximum(kd - 1, 0), diagonal=False,
                          gate=jnp.where(kd > 0, 1.0, 0.0).astype(BF16)))
    n_items = len(items)
    for t in range(n_items + 4):
        if 4 <= t:
            it = items[t - 4]
            it["pv"] = stage_v(it["kb"], it["a"], it["gate"])
        if 2 <= t < n_items + 2:
            it = items[t - 2]
            it["rest"], it["colsum"] = stage_c(it["hilo"], it["lk0"])
        if t < n_items:
            it = items[t]
            it["z"] = stage_q(it["q"], it["kb"])
        if 3 <= t < n_items + 3:
            it = items[t - 3]
            carry_in = None if it["diagonal"] else items[t - 4]["colsum"]
            it["a"] = stage_x(it["lb"], it["rest"], carry_in, it["diagonal"])
        if 1 <= t < n_items + 1:
            it = items[t - 1]
            it["lb"], it["hilo"], it["lk0"] = stage_e(it["z"], it["diagonal"])

    qs = [items[2 * r]["q"] for r in range(n_sub)]
    state = []
    for r in range(n_sub):
        d, p = items[2 * r], items[2 * r + 1]
        state += [d["colsum"] + p["colsum"], d["pv"] + p["pv"]]

    kd_last = qi * n_sub + n_sub - 1

    def cond(st):
        worst = st[1]
        for r in range(1, n_sub):
            worst = jnp.maximum(worst, st[1 + 2 * r])
        return jnp.logical_and(kd_last - 2 - st[0] >= 0, jnp.max(worst) > SB_DEAD_LOG2)

    def body(st):
        j, out = st[0], []
        for r in range(n_sub):
            kb = qi * n_sub + r - 2 - j
            gate = jnp.where(kb >= 0, 1.0, 0.0).astype(BF16)
            kb = jnp.maximum(kb, 0)
            lb, hilo, lk0 = stage_e(stage_q(qs[r], kb), False)
            rest, colsum = stage_c(hilo, lk0)
            a = stage_x(lb, rest, st[1 + 2 * r], False)
            out += [st[1 + 2 * r] + colsum, st[2 + 2 * r] + stage_v(kb, a, gate)]
        return (j + 1, *out)

    st = lax.while_loop(cond, body, (0, *state))
    _store_head_out(o_ref, jnp.concatenate([st[2 + 2 * r] for r in range(n_sub)], axis=1), g_ref[...])


def _sb_attn(qk, v_t, g_col, u2, w_cast):
    b, s, _ = qk.shape
    assert TQ_S % TK_S == 0
    nq = s // TQ_S
    w_spec = _cast_rows_spec(w_cast, b * N_HEADS * nq, lambda bi, h, qi: (bi * N_HEADS + h) * nq + qi)
    return pl.pallas_call(
        _sb_kernel,
        grid=(b, N_HEADS, nq),
        in_specs=[
            pl.BlockSpec((None, TQ_S, HEAD_DIM), lambda bi, h, qi: (bi, qi, h)),
            pl.BlockSpec((None, s, HEAD_DIM), lambda bi, h, qi: (bi, 0, N_HEADS + h)),
            pl.BlockSpec((None, HEAD_DIM, s), lambda bi, h, qi: (h, 0, bi)),
            pl.BlockSpec((HEAD_DIM, 1), lambda bi, h, qi: (h, 0)),
            pl.BlockSpec((TK_S, 2 * TK_S), lambda bi, h, qi: (0, 0)),
            w_spec,
        ],
        out_specs=[pl.BlockSpec((None, TQ_S, HEAD_DIM), lambda bi, h, qi: (bi, qi, h)), w_spec],
        out_shape=[jax.ShapeDtypeStruct((b, s, GROUP_W), BF16),
                   jax.ShapeDtypeStruct(w_cast.shape, BF16)],
        compiler_params=_params(("arbitrary", "arbitrary", "arbitrary")),
        name="sb_attn",
    )(qk, qk, v_t, g_col, u2, w_cast)


def _fox_kernel(q_ref, k_ref, extq_ref, extk_ref, vt_ref, cq_ref, clast_ref, g_ref, wu_ref, wd_ref,
                o_ref, wub_ref, wdb_ref,
                kn_ref, z_ref, p_ref, m_ref, l_ref, a_ref, acc_ref):
    wub_ref[...] = wu_ref[...].astype(BF16)
    wdb_ref[...] = wd_ref[...].astype(BF16)

    bi, h, qs = pl.program_id(0), pl.program_id(1), pl.program_id(2)
    last_tile = vt_ref.shape[1] // TK_F - 1

    def sq_norms_ub(rows):
        rf = rows.astype(F32)
        r2_up = (rf * rf * (1.0 + 2.0 ** -7)).astype(BF16)
        return _dot_nt(jnp.ones((8, HEAD_DIM), BF16), r2_up)[:1]

    @pl.when(qs == 0)
    def _():
        best = sq_norms_ub(k_ref[:KN_BLK, :])
        for r0 in range(KN_BLK, k_ref.shape[0], KN_BLK):
            best = jnp.maximum(best, sq_norms_ub(k_ref[r0:r0 + KN_BLK, :]))
        kn_ref[...] = jnp.broadcast_to(jnp.sqrt(jnp.max(best, axis=1, keepdims=True)), kn_ref.shape)

    def make_block(sub):
        qi = qs * N_SUB_F + sub
        rows = slice(sub * TQ_F, (sub + 1) * TQ_F)
        n_full = (qi * TQ_F) // TK_F
        q = q_ref[rows, :]
        q_aug = jnp.concatenate([q, extq_ref[rows, :]], axis=1)
        ub = jnp.sqrt(sq_norms_ub(q)) * kn_ref[:, :1] * NORM_SLACK + cq_ref[:, rows]

        def kb_of(i):
            return jnp.clip(n_full - i, 0, last_tile)

        def qk_t(i):
            ks = pl.multiple_of(kb_of(i) * TK_F, TK_F)
            k_aug = jnp.concatenate([k_ref[pl.ds(ks, TK_F), :], extk_ref[pl.ds(ks, TK_F), :]], axis=1)
            return _dot_nt(k_aug, q_aug)

        def pv_t(i, slot):
            ks = pl.multiple_of(kb_of(i) * TK_F, TK_F)
            return _dot(vt_ref[:, pl.ds(ks, TK_F)], p_ref[sub, slot])

        def prologue():
            st = qk_t(0)
            z_ref[sub, 1] = qk_t(1)
            key = n_full * TK_F + lax.broadcasted_iota(jnp.int32, (TK_F, TQ_F), 0)
            qry = qi * TQ_F + lax.broadcasted_iota(jnp.int32, (TK_F, TQ_F), 1)
            st = jnp.where(key <= qry, st, -jnp.inf)
            m0 = _col_reduce(st, jnp.maximum, jnp.max)
            p0 = jnp.exp2(st - m0)
            p_ref[sub, 0] = p0.astype(BF16)
            m_ref[sub] = m0
            l_ref[sub] = _col_reduce(p0, jnp.add, jnp.sum)
            a_ref[sub] = jnp.ones_like(m0)
            acc_ref[sub] = jnp.zeros(acc_ref.shape[1:], F32)
            gap = jnp.max(ub - m0)
            n_tiles = 1
            for i in range(clast_ref.shape[2]):
                alive = jnp.logical_and(i < n_full, gap - clast_ref[bi, h, i] > FOX_DEAD_LOG2)
                n_tiles = n_tiles + alive.astype(jnp.int32)
            return n_tiles

        def step(i, slot, n_tiles=None):
            z_ref[sub, 1 - slot] = qk_t(i + 1)
            acc_ref[sub] = a_ref[sub] * acc_ref[sub] + pv_t(i - 1, 1 - slot)
            z = z_ref[sub, slot]
            if n_tiles is not None:
                z = jnp.where(i < n_tiles, z, -jnp.inf)
            m = m_ref[sub]
            m_new = jnp.maximum(m, _col_reduce(z, jnp.maximum, jnp.max))
            alpha = jnp.exp2(m - m_new)
            p = jnp.exp2(z_ref[sub, slot] - m_new) if n_tiles is None else jnp.exp2(z - m_new)
            p_ref[sub, slot] = p.astype(BF16)
            l_ref[sub] = alpha * l_ref[sub] + _col_reduce(p, jnp.add, jnp.sum)
            m_ref[sub] = m_new
            a_ref[sub] = alpha

        def walk(n_tiles):
            def body(j, _):
                i = 1 + 2 * j
                step(i, 1)
                step(i + 1, 0, n_tiles)
                return 0

            lax.fori_loop(0, n_tiles // 2, body, 0)

        def epilogue(n_tiles):
            acc = a_ref[sub] * acc_ref[sub] + pv_t(2 * (n_tiles // 2), 0)
            o_ref[rows, :] = _head_out(acc / l_ref[sub], g_ref[...]).astype(o_ref.dtype)

        return prologue, walk, epilogue

    blocks = [make_block(sub) for sub in range(N_SUB_F)]
    n_tiles = [prologue() for prologue, _, _ in blocks]
    for (_, walk, _), n in zip(blocks, n_tiles):
        walk(n)
    for (_, _, epilogue), n in zip(blocks, n_tiles):
        epilogue(n)


def _fox_attn(qk, v_t, c, ext_q, ext_k, g_col, w_cast_a, w_cast_b):
    b, s, _ = qk.shape
    base = 2 * N_HEADS
    n_kt = s // TK_F
    tq_step = N_SUB_F * TQ_F
    nq = s // tq_step
    assert TK_F % TQ_F == 0 and s % tq_step == 0
    step_of = lambda bi, h, qi: (bi * N_HEADS + h) * nq + qi
    wa_spec = _cast_rows_spec(w_cast_a, b * N_HEADS * nq, step_of)
    wb_spec = _cast_rows_spec(w_cast_b, b * N_HEADS * nq, step_of)
    c_hs = jnp.transpose(c[:, :, :N_HEADS], (0, 2, 1))
    c_q = c_hs.reshape(b, N_HEADS, nq, 1, tq_step)
    c_last = c_hs.reshape(b, N_HEADS, n_kt, TK_F)[..., TK_F - 1]
    return pl.pallas_call(
        _fox_kernel,
        grid=(b, N_HEADS, nq),
        in_specs=[
            pl.BlockSpec((None, tq_step, HEAD_DIM), lambda bi, h, qi: (bi, qi, base + h)),
            pl.BlockSpec((None, s, HEAD_DIM), lambda bi, h, qi: (bi, 0, base + N_HEADS + h)),
            pl.BlockSpec((None, tq_step, HEAD_DIM), lambda bi, h, qi: (bi, qi, h)),
            pl.BlockSpec((None, s, HEAD_DIM), lambda bi, h, qi: (bi, 0, h)),
            pl.BlockSpec((None, HEAD_DIM, s), lambda bi, h, qi: (N_HEADS + h, 0, bi)),
            pl.BlockSpec((None, None, None, 1, tq_step), lambda bi, h, qi: (bi, h, qi, 0, 0)),
            pl.BlockSpec(memory_space=pltpu.SMEM),
            pl.BlockSpec((HEAD_DIM, 1), lambda bi, h, qi: (h, 0)),
            wa_spec,
            wb_spec,
        ],
        out_specs=[pl.BlockSpec((None, tq_step, HEAD_DIM), lambda bi, h, qi: (bi, qi, h)), wa_spec, wb_spec],
        scratch_shapes=[
            pltpu.VMEM((1, LANES), F32),
            pltpu.VMEM((N_SUB_F, 2, TK_F, TQ_F), F32),
            pltpu.VMEM((N_SUB_F, 2, TK_F, TQ_F), BF16),
            pltpu.VMEM((N_SUB_F, 1, TQ_F), F32),
            pltpu.VMEM((N_SUB_F, 1, TQ_F), F32),
            pltpu.VMEM((N_SUB_F, 1, TQ_F), F32),
            pltpu.VMEM((N_SUB_F, HEAD_DIM, TQ_F), F32),
        ],
        out_shape=[jax.ShapeDtypeStruct((b, s, GROUP_W), BF16),
                   jax.ShapeDtypeStruct(w_cast_a.shape, BF16),
                   jax.ShapeDtypeStruct(w_cast_b.shape, BF16)],
        compiler_params=_params(("arbitrary", "arbitrary", "arbitrary")),
        name="fox_attn",
    )(qk, qk, ext_q, ext_k, v_t, c_q, c_last, g_col, w_cast_a, w_cast_b)


def _out_proj_kernel(ms_ref, mf_ref, w_ref, x_ref, g_ref, x1_ref, h2_ref):
    acc = _dot(ms_ref[...], w_ref[:GROUP_W, :]) + _dot(mf_ref[...], w_ref[GROUP_W:, :])
    x1 = x_ref[...] + acc
    x1_ref[...] = x1
    var = jnp.mean(x1 * x1, axis=-1, keepdims=True)
    h2_ref[...] = (x1 * lax.rsqrt(var + EPS) * g_ref[...]).astype(BF16)


def _out_proj(mixed_sb, mixed_fox, w_out, x2, g_mlp):
    m, d = x2.shape
    return pl.pallas_call(
        _out_proj_kernel,
        grid=(m // TM_OUT,),
        in_specs=[
            pl.BlockSpec((TM_OUT, GROUP_W), lambda i: (i, 0)),
            pl.BlockSpec((TM_OUT, GROUP_W), lambda i: (i, 0)),
            pl.BlockSpec((2 * GROUP_W, d), lambda i: (0, 0)),
            pl.BlockSpec((TM_OUT, d), lambda i: (i, 0)),
            pl.BlockSpec((1, d), lambda i: (0, 0)),
        ],
        out_specs=[
            pl.BlockSpec((TM_OUT, d), lambda i: (i, 0)),
            pl.BlockSpec((TM_OUT, d), lambda i: (i, 0)),
        ],
        out_shape=[
            jax.ShapeDtypeStruct((m, d), F32),
            jax.ShapeDtypeStruct((m, d), BF16),
        ],
        compiler_params=_params(("arbitrary",)),
        name="out_proj",
    )(mixed_sb, mixed_fox, w_out, x2, g_mlp)


def _mlp_kernel(h_ref, x1_ref, wu_ref, wd_ref, g_ref, o_ref):
    f = pl.program_id(1)

    @pl.when(f == 0)
    def _():
        o_ref[...] = x1_ref[...]

    for c0 in range(0, wu_ref.shape[1], MLP_CHUNK):
        c1 = c0 + MLP_CHUNK
        u = jnp.maximum(_dot(h_ref[...], wu_ref[:, c0:c1]), 0.0)
        o_ref[...] += _dot((u * u).astype(BF16), wd_ref[c0:c1, :])

    @pl.when(f == pl.num_programs(1) - 1)
    def _():
        x2 = o_ref[...]
        var = jnp.mean(x2 * x2, axis=-1, keepdims=True)
        o_ref[...] = x2 * lax.rsqrt(var + EPS) * g_ref[...]


def _mlp(h2, x1, w_up, w_down, g_final):
    m, d = x1.shape
    dff = w_up.shape[1]
    return pl.pallas_call(
        _mlp_kernel,
        grid=(m // TM_MLP, dff // TF_MLP),
        in_specs=[
            pl.BlockSpec((TM_MLP, d), lambda i, f: (i, 0)),
            pl.BlockSpec((TM_MLP, d), lambda i, f: (i, 0)),
            pl.BlockSpec((d, TF_MLP), lambda i, f: (0, f)),
            pl.BlockSpec((TF_MLP, d), lambda i, f: (f, 0)),
            pl.BlockSpec((1, d), lambda i, f: (0, 0)),
        ],
        out_specs=pl.BlockSpec((TM_MLP, d), lambda i, f: (i, 0)),
        out_shape=jax.ShapeDtypeStruct((m, d), F32),
        compiler_params=_params(("arbitrary", "arbitrary"), VMEM_LIMIT_MLP),
        name="mlp",
    )(h2, x1, w_up, w_down, g_final)


def kernel(x, g_attn, w_in, b_f, g_out_sb, g_out_fox, w_out, g_mlp, w_up, w_down, g_final):
    b, s, d = x.shape
    n_qkv = 6 * GROUP_W
    assert s % TQ_S == 0 and s % TK_F == 0 and (b * s) % TM_IN == 0
    x2 = x.reshape(b * s, d)

    idx = jnp.arange(TK_S)
    u_excl = (idx[None, :] > idx[:, None]).astype(BF16)
    u2 = jnp.concatenate([u_excl, u_excl], axis=1)
    cidx = jnp.arange(CS_BLK)
    tri_incl = (cidx[None, :] <= cidx[:, None]).astype(BF16)

    gw = GROUP_W
    for l in range(g_attn.shape[0]):
        w_t = jnp.swapaxes(w_in[l], 0, 1)
        w_f_t = jnp.pad(w_t[n_qkv:], ((0, LANES - N_HEADS), (0, 0))).astype(BF16)
        b_pad = jnp.pad(b_f[l], (0, LANES - N_HEADS)).reshape(1, LANES)

        qk, v_t, f_logit = _in_proj(x2, g_attn[l].reshape(1, d), w_t, w_f_t)
        qk = qk.reshape(b, s, N_QK_GROUPS * gw)
        v_t = v_t.reshape(2 * N_HEADS, HEAD_DIM, b * s)
        c, ext_q, ext_k = _forget_cs(f_logit.reshape(b, s, LANES), b_pad, tri_incl)

        mixed_sb, w_out_b = _sb_attn(qk, v_t, g_out_sb[l].reshape(gw, 1), u2, w_out[l])
        mixed_fox, w_up_b, w_down_b = _fox_attn(qk, v_t, c, ext_q, ext_k, g_out_fox[l].reshape(gw, 1),
                                                w_up[l], w_down[l])

        x1, h2 = _out_proj(mixed_sb.reshape(b * s, GROUP_W), mixed_fox.reshape(b * s, GROUP_W),
                           w_out_b, x2, g_mlp[l].reshape(1, d))
        assert g_attn.shape[0] == 1
        x2 = _mlp(h2, x1, w_up_b, w_down_b, g_final.reshape(1, d))
    return x2.reshape(b, s, d)
```

```python
import functools

import jax
import jax.numpy as jnp
from jax import lax
from jax.experimental import pallas as pl
from jax.experimental.pallas import tpu as pltpu

F32 = jnp.float32
BF16 = jnp.bfloat16

HEAD_DIM = 128
N_HEADS = 8
GROUP_W = N_HEADS * HEAD_DIM
EPS = 1e-6
SCALE = HEAD_DIM ** -0.5
LOG2E = 1.4426950408889634
SB_DEAD_LOG2 = -152.0
FOX_DEAD_LOG2 = -152.0
NORM_SLACK = 1.01
M_INIT = -1e30
KN_BLK = 512
LANES = 128

VMEM_LIMIT = 56 * 1024 * 1024
VMEM_LIMIT_MLP = 58 * 1024 * 1024

TM_IN = 1024
TN_IN = 1024
IN_CHUNK = 256
TQ_S = 2048
TK_S = 256
TQ_F = 512
TK_F = 512
N_SUB_F = 4
CS_BLK = 256
CS_CHUNK = 1024
TM_OUT = 512
TM_MLP = 512
TF_MLP = 2048
MLP_CHUNK = 1024


def _params(sem, vmem_limit=VMEM_LIMIT):
    return pltpu.CompilerParams(dimension_semantics=sem, vmem_limit_bytes=vmem_limit)


def _dot(a, b):
    return jnp.dot(a, b, preferred_element_type=F32)


def _dot_nt(a, b):
    return lax.dot_general(a, b, (((1,), (1,)), ((), ())), preferred_element_type=F32)


def _log_sigmoid(x):
    return jnp.minimum(x, 0.0) - jnp.log(1.0 + jnp.exp(-jnp.abs(x)))


N_QK_GROUPS = 4


def _in_proj_kernel(x_ref, g_ref, w_ref, wf_ref, qk_ref, vt_ref, f_ref, h_ref):
    j = pl.program_id(1)

    @pl.when(j == 0)
    def _():
        x = x_ref[...]
        var = jnp.mean(x * x, axis=-1, keepdims=True)
        h = (x * lax.rsqrt(var + EPS) * g_ref[...]).astype(BF16)
        h_ref[...] = h
        f_ref[...] = _dot_nt(h, wf_ref[...])

    @pl.when(j < N_QK_GROUPS)
    def _():
        scale = jnp.where((j == 0) | (j == 2), SCALE * LOG2E, 1.0).astype(F32)
        for c0 in range(0, TN_IN, IN_CHUNK):
            c1 = c0 + IN_CHUNK
            w = w_ref[c0:c1, :].astype(BF16)
            qk_ref[:, c0:c1] = (_dot_nt(h_ref[...], w) * scale).astype(BF16)

    @pl.when(j >= N_QK_GROUPS)
    def _():
        for c0 in range(0, TN_IN, IN_CHUNK):
            c1 = c0 + IN_CHUNK
            w = w_ref[c0:c1, :].astype(BF16)
            vt_ref[c0:c1, :] = _dot_nt(w, h_ref[...]).astype(BF16)


def _w_group(j):
    return jnp.where(j < 2, j, jnp.where(j < 4, j + 1, jnp.where(j == 4, 2, 5)))


def _in_proj(x2, g, w_qkv_t, w_f_t):
    m, d = x2.shape
    n = 6 * GROUP_W
    assert TN_IN == GROUP_W and w_qkv_t.shape[0] >= n
    return pl.pallas_call(
        _in_proj_kernel,
        grid=(m // TM_IN, n // TN_IN),
        in_specs=[
            pl.BlockSpec((TM_IN, d), lambda i, j: (i, 0)),
            pl.BlockSpec((1, d), lambda i, j: (0, 0)),
            pl.BlockSpec((TN_IN, d), lambda i, j: (_w_group(j), 0)),
            pl.BlockSpec((LANES, d), lambda i, j: (0, 0)),
        ],
        out_specs=[
            pl.BlockSpec((TM_IN, TN_IN), lambda i, j: (i, jnp.minimum(j, N_QK_GROUPS - 1))),
            pl.BlockSpec((TN_IN, TM_IN), lambda i, j: (jnp.maximum(j - N_QK_GROUPS, 0), i)),
            pl.BlockSpec((TM_IN, LANES), lambda i, j: (i, 0)),
        ],
        out_shape=[
            jax.ShapeDtypeStruct((m, N_QK_GROUPS * GROUP_W), BF16),
            jax.ShapeDtypeStruct((2 * GROUP_W, m), BF16),
            jax.ShapeDtypeStruct((m, LANES), F32),
        ],
        scratch_shapes=[pltpu.VMEM((TM_IN, d), BF16)],
        compiler_params=_params(("arbitrary", "arbitrary")),
        name="in_proj",
    )(x2, g, w_qkv_t, w_f_t)


def _split3(x):
    p1 = x.astype(BF16)
    r1 = x - p1.astype(F32)
    p2 = r1.astype(BF16)
    p3 = (r1 - p2.astype(F32)).astype(BF16)
    return p1, p2, p3


def _forget_cs_kernel(f_ref, b_ref, tri_ref, selq_ref, selk_ref, oneq_ref, onek_ref,
                      c_ref, extq_ref, extk_ref, carry_ref):
    n_blk = f_ref.shape[0] // CS_BLK
    tri = tri_ref[...]

    @pl.when(pl.program_id(1) == 0)
    def _():
        carry_ref[...] = jnp.zeros_like(carry_ref)

    def body(i, carry):
        r0 = pl.multiple_of(i * CS_BLK, CS_BLK)
        lf = _log_sigmoid(f_ref[pl.ds(r0, CS_BLK), :] + b_ref[...])
        p1, p2, p3 = _split3(lf)
        c = _dot(tri, p1) + _dot(tri, p2) + _dot(tri, p3) + carry
        c2 = c * LOG2E
        c_ref[pl.ds(r0, CS_BLK), :] = c2
        lane = lax.broadcasted_iota(jnp.int32, c2.shape, 1)
        packed = jnp.zeros_like(c2)
        for k, piece in enumerate(_split3(c2)):
            piece = jnp.where(lane < N_HEADS, piece.astype(F32), 0.0)
            packed = packed + (piece if k == 0 else pltpu.roll(piece, N_HEADS * k, axis=1))
        packed = packed.astype(BF16)
        extq_ref[pl.ds(r0, CS_BLK), :] = (_dot(packed, selq_ref[...]) + oneq_ref[...]).astype(BF16)
        extk_ref[pl.ds(r0, CS_BLK), :] = (_dot(packed, selk_ref[...]) + onek_ref[...]).astype(BF16)
        return c[CS_BLK - 1:CS_BLK, :]

    carry_ref[...] = lax.fori_loop(0, n_blk, body, carry_ref[...])


def _aug_constants():
    row = jnp.arange(LANES)
    col = jnp.arange(GROUP_W)
    piece, head = row // N_HEADS, row % N_HEADS
    blk, lane = col // HEAD_DIM, col % HEAD_DIM
    same = (head[:, None] == blk[None, :]) & (piece[:, None] < 3)
    selq = jnp.where(same & (lane[None, :] == 3 + piece[:, None]), 1.0, 0.0).astype(BF16)
    selk = jnp.where(same & (lane[None, :] == piece[:, None]), -1.0, 0.0).astype(BF16)
    oneq = jnp.where(lane < 3, 1.0, 0.0).astype(F32).reshape(1, GROUP_W)
    onek = jnp.where((lane >= 3) & (lane < 6), 1.0, 0.0).astype(F32).reshape(1, GROUP_W)
    return selq, selk, oneq, onek


def _forget_cs(f_logit, b_pad, tri_incl):
    b, s, _ = f_logit.shape
    assert s % CS_CHUNK == 0 and CS_CHUNK % CS_BLK == 0
    const = lambda shape: pl.BlockSpec(shape, lambda i, j: (0, 0))
    rows = lambda width: pl.BlockSpec((None, CS_CHUNK, width), lambda i, j: (i, j, 0))
    return pl.pallas_call(
        _forget_cs_kernel,
        grid=(b, s // CS_CHUNK),
        in_specs=[
            rows(LANES),
            const((1, LANES)),
            const((CS_BLK, CS_BLK)),
            const((LANES, GROUP_W)),
            const((LANES, GROUP_W)),
            const((1, GROUP_W)),
            const((1, GROUP_W)),
        ],
        out_specs=[rows(LANES), rows(GROUP_W), rows(GROUP_W)],
        out_shape=[
            jax.ShapeDtypeStruct((b, s, LANES), F32),
            jax.ShapeDtypeStruct((b, s, GROUP_W), BF16),
            jax.ShapeDtypeStruct((b, s, GROUP_W), BF16),
        ],
        scratch_shapes=[pltpu.VMEM((1, LANES), F32)],
        compiler_params=_params(("arbitrary", "arbitrary")),
        name="forget_cs",
    )(f_logit, b_pad, tri_incl, *_aug_constants())


def _head_rmsnorm(o, g):
    return o * lax.rsqrt(jnp.mean(o * o, axis=-1, keepdims=True) + EPS) * g


def _col_reduce(x, pair, full):
    while x.shape[0] > 8:
        half = x.shape[0] // 2
        x = pair(x[:half], x[half:])
    return full(x, axis=0, keepdims=True)


def _head_out(acc_t, g_col):
    o = acc_t * lax.rsqrt(jnp.mean(acc_t * acc_t, axis=0, keepdims=True) + EPS) * g_col
    return jnp.transpose(o)


def _store_head_out(o_ref, acc_t, g_col):
    o_ref[...] = _head_out(acc_t, g_col).astype(o_ref.dtype)


def _cast_rows_spec(w, n_steps, step_of):
    rows, cols = w.shape
    assert rows % n_steps == 0 and (rows // n_steps) % 16 == 0
    return pl.BlockSpec((rows // n_steps, cols), lambda bi, h, qi: (step_of(bi, h, qi), 0))


def _sb_kernel(q_ref, k_ref, vt_ref, g_ref, u2_ref, w_ref, o_ref, wb_ref):
    wb_ref[...] = w_ref[...].astype(BF16)

    qi = pl.program_id(2)
    u2 = u2_ref[...]
    n_sub = TQ_S // TK_S
    key = lax.broadcasted_iota(jnp.int32, (TK_S, TK_S), 0)
    qry = lax.broadcasted_iota(jnp.int32, (TK_S, TK_S), 1)
    causal = key < qry

    def stage_q(q, kb):
        ks = pl.multiple_of(kb * TK_S, TK_S)
        return _dot_nt(k_ref[pl.ds(ks, TK_S), :], q)

    def stage_e(z, diagonal):
        zn = jnp.minimum(z, 0.0)
        zp = zn - z
        sp = jnp.log2(1.0 + jnp.exp2(zn + zp))
        lb = zn - sp
        lk = zp - sp
        if diagonal:
            lk = jnp.where(causal, lk, 0.0)
        hi = lk.astype(BF16)
        lo = (lk - hi.astype(F32)).astype(BF16)
        return lb, jnp.concatenate([hi, lo], axis=0), lk[:1, :]

    def stage_c(hilo, lk_first):
        rest = _dot(u2, hilo)
        return rest, rest[:1, :] + lk_first

    def stage_x(lb, rest, carry, diagonal):
        if diagonal:
            a = jnp.where(causal, jnp.exp2(lb + rest), 0.0)
        else:
            a = jnp.exp2(lb + (rest + carry))
        return a.astype(BF16)

    def stage_v(kb, a, gate=None):
        vt = vt_ref[:, pl.ds(pl.multiple_of(kb * TK_S, TK_S), TK_S)]
        if gate is not None:
            vt = vt * gate
        return _dot(vt, a)

    items = []
    for sub in range(n_sub):
        kd = qi * n_sub + sub
        q = q_ref[sub * TK_S:(sub + 1) * TK_S, :]
        items.append(dict(q=q, kb=kd, diagonal=True, gate=None))
        items.append(dict(q=q, kb=jnp.maximum(kd - 1, 0), diagonal=False,
                          gate=jnp.where(kd > 0, 1.0, 0.0).astype(BF16)))
    n_items = len(items)
    for t in range(n_items + 4):
        if 4 <= t:
            it = items[t - 4]
            it["pv"] = stage_v(it["kb"], it["a"], it["gate"])
        if 2 <= t < n_items + 2:
            it = items[t - 2]
            it["rest"], it["colsum"] = stage_c(it["hilo"], it["lk0"])
        if t < n_items:
            it = items[t]
            it["z"] = stage_q(it["q"], it["kb"])
        if 3 <= t < n_items + 3:
            it = items[t - 3]
            carry_in = None if it["diagonal"] else items[t - 4]["colsum"]
            it["a"] = stage_x(it["lb"], it["rest"], carry_in, it["diagonal"])
        if 1 <= t < n_items + 1:
            it = items[t - 1]
            it["lb"], it["hilo"], it["lk0"] = stage_e(it["z"], it["diagonal"])

    qs = [items[2 * r]["q"] for r in range(n_sub)]
    state = []
    for r in range(n_sub):
        d, p = items[2 * r], items[2 * r + 1]
        state += [d["colsum"] + p["colsum"], d["pv"] + p["pv"]]

    kd_last = qi * n_sub + n_sub - 1

    def cond(st):
        worst = st[1]
        for r in range(1, n_sub):
            worst = jnp.maximum(worst, st[1 + 2 * r])
        return jnp.logical_and(kd_last - 2 - st[0] >= 0, jnp.max(worst) > SB_DEAD_LOG2)

    def body(st):
        j, out = st[0], []
        for r in range(n_sub):
            kb = qi * n_sub + r - 2 - j
            gate = jnp.where(kb >= 0, 1.0, 0.0).astype(BF16)
            kb = jnp.maximum(kb, 0)
            lb, hilo, lk0 = stage_e(stage_q(qs[r], kb), False)
            rest, colsum = stage_c(hilo, lk0)
            a = stage_x(lb, rest, st[1 + 2 * r], False)
            out += [st[1 + 2 * r] + colsum, st[2 + 2 * r] + stage_v(kb, a, gate)]
        return (j + 1, *out)

    st = lax.while_loop(cond, body, (0, *state))
    _store_head_out(o_ref, jnp.concatenate([st[2 + 2 * r] for r in range(n_sub)], axis=1), g_ref[...])


def _sb_attn(qk, v_t, g_col, u2, w_cast):
    b, s, _ = qk.shape
    assert TQ_S % TK_S == 0
    nq = s // TQ_S
    w_spec = _cast_rows_spec(w_cast, b * N_HEADS * nq, lambda bi, h, qi: (bi * N_HEADS + h) * nq + qi)
    return pl.pallas_call(
        _sb_kernel,
        grid=(b, N_HEADS, nq),
        in_specs=[
            pl.BlockSpec((None, TQ_S, HEAD_DIM), lambda bi, h, qi: (bi, qi, h)),
            pl.BlockSpec((None, s, HEAD_DIM), lambda bi, h, qi: (bi, 0, N_HEADS + h)),
            pl.BlockSpec((None, HEAD_DIM, s), lambda bi, h, qi: (h, 0, bi)),
            pl.BlockSpec((HEAD_DIM, 1), lambda bi, h, qi: (h, 0)),
            pl.BlockSpec((TK_S, 2 * TK_S), lambda bi, h, qi: (0, 0)),
            w_spec,
        ],
        out_specs=[pl.BlockSpec((None, TQ_S, HEAD_DIM), lambda bi, h, qi: (bi, qi, h)), w_spec],
        out_shape=[jax.ShapeDtypeStruct((b, s, GROUP_W), BF16),
                   jax.ShapeDtypeStruct(w_cast.shape, BF16)],
        compiler_params=_params(("arbitrary", "arbitrary", "arbitrary")),
        name="sb_attn",
    )(qk, qk, v_t, g_col, u2, w_cast)


def _fox_kernel(q_ref, k_ref, extq_ref, extk_ref, vt_ref, cq_ref, clast_ref, g_ref, wu_ref, wd_ref,
                o_ref, wub_ref, wdb_ref,
                kn_ref, z_ref, p_ref, m_ref, l_ref, a_ref, acc_ref):
    wub_ref[...] = wu_ref[...].astype(BF16)
    wdb_ref[...] = wd_ref[...].astype(BF16)

    bi, h, qs = pl.program_id(0), pl.program_id(1), pl.program_id(2)
    last_tile = vt_ref.shape[1] // TK_F - 1

    def sq_norms_ub(rows):
        rf = rows.astype(F32)
        r2_up = (rf * rf * (1.0 + 2.0 ** -7)).astype(BF16)
        return _dot_nt(jnp.ones((8, HEAD_DIM), BF16), r2_up)[:1]

    @pl.when(qs == 0)
    def _():
        best = sq_norms_ub(k_ref[:KN_BLK, :])
        for r0 in range(KN_BLK, k_ref.shape[0], KN_BLK):
            best = jnp.maximum(best, sq_norms_ub(k_ref[r0:r0 + KN_BLK, :]))
        kn_ref[...] = jnp.broadcast_to(jnp.sqrt(jnp.max(best, axis=1, keepdims=True)), kn_ref.shape)

    def make_block(sub):
        qi = qs * N_SUB_F + sub
        rows = slice(sub * TQ_F, (sub + 1) * TQ_F)
        n_full = (qi * TQ_F) // TK_F
        q = q_ref[rows, :]
        q_aug = jnp.concatenate([q, extq_ref[rows, :]], axis=1)
        ub = jnp.sqrt(sq_norms_ub(q)) * kn_ref[:, :1] * NORM_SLACK + cq_ref[:, rows]

        def kb_of(i):
            return jnp.clip(n_full - i, 0, last_tile)

        def qk_t(i):
            ks = pl.multiple_of(kb_of(i) * TK_F, TK_F)
            k_aug = jnp.concatenate([k_ref[pl.ds(ks, TK_F), :], extk_ref[pl.ds(ks, TK_F), :]], axis=1)
            return _dot_nt(k_aug, q_aug)

        def pv_t(i, slot):
            ks = pl.multiple_of(kb_of(i) * TK_F, TK_F)
            return _dot(vt_ref[:, pl.ds(ks, TK_F)], p_ref[sub, slot])

        def prologue():
            st = qk_t(0)
            z_ref[sub, 1] = qk_t(1)
            key = n_full * TK_F + lax.broadcasted_iota(jnp.int32, (TK_F, TQ_F), 0)
            qry = qi * TQ_F + lax.broadcasted_iota(jnp.int32, (TK_F, TQ_F), 1)
            st = jnp.where(key <= qry, st, -jnp.inf)
            m0 = _col_reduce(st, jnp.maximum, jnp.max)
            p0 = jnp.exp2(st - m0)
            p_ref[sub, 0] = p0.astype(BF16)
            m_ref[sub] = m0
            l_ref[sub] = _col_reduce(p0, jnp.add, jnp.sum)
            a_ref[sub] = jnp.ones_like(m0)
            acc_ref[sub] = jnp.zeros(acc_ref.shape[1:], F32)
            gap = jnp.max(ub - m0)
            n_tiles = 1
            for i in range(clast_ref.shape[2]):
                alive = jnp.logical_and(i < n_full, gap - clast_ref[bi, h, i] > FOX_DEAD_LOG2)
                n_tiles = n_tiles + alive.astype(jnp.int32)
            return n_tiles

        def step(i, slot, last=False):
            if not last:
                z_ref[sub, 1 - slot] = qk_t(i + 1)
            acc_ref[sub] = a_ref[sub] * acc_ref[sub] + pv_t(i - 1, 1 - slot)
            m = m_ref[sub]
            m_new = jnp.maximum(m, _col_reduce(z_ref[sub, slot], jnp.maximum, jnp.max))
            alpha = jnp.exp2(m - m_new)
            p = jnp.exp2(z_ref[sub, slot] - m_new)
            p_ref[sub, slot] = p.astype(BF16)
            l_ref[sub] = alpha * l_ref[sub] + _col_reduce(p, jnp.add, jnp.sum)
            m_ref[sub] = m_new
            a_ref[sub] = alpha

        def walk(n_tiles):
            n_in = n_tiles - 1

            def body(j, _):
                i = 1 + 2 * j
                step(i, 1)
                step(i + 1, 0)
                return 0

            lax.fori_loop(0, n_in // 2, body, 0)

            @pl.when(n_in % 2 == 1)
            def _():
                step(n_in, 1, last=True)

        def epilogue(n_tiles):
            acc = a_ref[sub] * acc_ref[sub] + pv_t(n_tiles - 1, (n_tiles - 1) % 2)
            o_ref[rows, :] = _head_out(acc / l_ref[sub], g_ref[...]).astype(o_ref.dtype)

        return prologue, walk, epilogue

    blocks = [make_block(sub) for sub in range(N_SUB_F)]
    n_tiles = [prologue() for prologue, _, _ in blocks]
    for (_, walk, _), n in zip(blocks, n_tiles):
        walk(n)
    for (_, _, epilogue), n in zip(blocks, n_tiles):
        epilogue(n)


def _fox_attn(qk, v_t, c, ext_q, ext_k, g_col, w_cast_a, w_cast_b):
    b, s, _ = qk.shape
    base = 2 * N_HEADS
    n_kt = s // TK_F
    tq_step = N_SUB_F * TQ_F
    nq = s // tq_step
    assert TK_F % TQ_F == 0 and s % tq_step == 0
    step_of = lambda bi, h, qi: (bi * N_HEADS + h) * nq + qi
    wa_spec = _cast_rows_spec(w_cast_a, b * N_HEADS * nq, step_of)
    wb_spec = _cast_rows_spec(w_cast_b, b * N_HEADS * nq, step_of)
    c_hs = jnp.transpose(c[:, :, :N_HEADS], (0, 2, 1))
    c_q = c_hs.reshape(b, N_HEADS, nq, 1, tq_step)
    c_last = c_hs.reshape(b, N_HEADS, n_kt, TK_F)[..., TK_F - 1]
    return pl.pallas_call(
        _fox_kernel,
        grid=(b, N_HEADS, nq),
        in_specs=[
            pl.BlockSpec((None, tq_step, HEAD_DIM), lambda bi, h, qi: (bi, qi, base + h)),
            pl.BlockSpec((None, s, HEAD_DIM), lambda bi, h, qi: (bi, 0, base + N_HEADS + h)),
            pl.BlockSpec((None, tq_step, HEAD_DIM), lambda bi, h, qi: (bi, qi, h)),
            pl.BlockSpec((None, s, HEAD_DIM), lambda bi, h, qi: (bi, 0, h)),
            pl.BlockSpec((None, HEAD_DIM, s), lambda bi, h, qi: (N_HEADS + h, 0, bi)),
            pl.BlockSpec((None, None, None, 1, tq_step), lambda bi, h, qi: (bi, h, qi, 0, 0)),
            pl.BlockSpec(memory_space=pltpu.SMEM),
            pl.BlockSpec((HEAD_DIM, 1), lambda bi, h, qi: (h, 0)),
            wa_spec,
            wb_spec,
        ],
        out_specs=[pl.BlockSpec((None, tq_step, HEAD_DIM), lambda bi, h, qi: (bi, qi, h)), wa_spec, wb_spec],
        scratch_shapes=[
            pltpu.VMEM((1, LANES), F32),
            pltpu.VMEM((N_SUB_F, 2, TK_F, TQ_F), F32),
            pltpu.VMEM((N_SUB_F, 2, TK_F, TQ_F), BF16),
            pltpu.VMEM((N_SUB_F, 1, TQ_F), F32),
            pltpu.VMEM((N_SUB_F, 1, TQ_F), F32),
            pltpu.VMEM((N_SUB_F, 1, TQ_F), F32),
            pltpu.VMEM((N_SUB_F, HEAD_DIM, TQ_F), F32),
        ],
        out_shape=[jax.ShapeDtypeStruct((b, s, GROUP_W), BF16),
                   jax.ShapeDtypeStruct(w_cast_a.shape, BF16),
                   jax.ShapeDtypeStruct(w_cast_b.shape, BF16)],
        compiler_params=_params(("arbitrary", "arbitrary", "arbitrary")),
        name="fox_attn",
    )(qk, qk, ext_q, ext_k, v_t, c_q, c_last, g_col, w_cast_a, w_cast_b)


def _out_proj_kernel(ms_ref, mf_ref, w_ref, x_ref, g_ref, x1_ref, h2_ref):
    acc = _dot(ms_ref[...], w_ref[:GROUP_W, :]) + _dot(mf_ref[...], w_ref[GROUP_W:, :])
    x1 = x_ref[...] + acc
    x1_ref[...] = x1
    var = jnp.mean(x1 * x1, axis=-1, keepdims=True)
    h2_ref[...] = (x1 * lax.rsqrt(var + EPS) * g_ref[...]).astype(BF16)


def _out_proj(mixed_sb, mixed_fox, w_out, x2, g_mlp):
    m, d = x2.shape
    return pl.pallas_call(
        _out_proj_kernel,
        grid=(m // TM_OUT,),
        in_specs=[
            pl.BlockSpec((TM_OUT, GROUP_W), lambda i: (i, 0)),
            pl.BlockSpec((TM_OUT, GROUP_W), lambda i: (i, 0)),
            pl.BlockSpec((2 * GROUP_W, d), lambda i: (0, 0)),
            pl.BlockSpec((TM_OUT, d), lambda i: (i, 0)),
            pl.BlockSpec((1, d), lambda i: (0, 0)),
        ],
        out_specs=[
            pl.BlockSpec((TM_OUT, d), lambda i: (i, 0)),
            pl.BlockSpec((TM_OUT, d), lambda i: (i, 0)),
        ],
        out_shape=[
            jax.ShapeDtypeStruct((m, d), F32),
            jax.ShapeDtypeStruct((m, d), BF16),
        ],
        compiler_params=_params(("arbitrary",)),
        name="out_proj",
    )(mixed_sb, mixed_fox, w_out, x2, g_mlp)


def _mlp_kernel(h_ref, x1_ref, wu_ref, wd_ref, g_ref, o_ref):
    f = pl.program_id(1)

    @pl.when(f == 0)
    def _():
        o_ref[...] = x1_ref[...]

    for c0 in range(0, wu_ref.shape[1], MLP_CHUNK):
        c1 = c0 + MLP_CHUNK
        u = jnp.maximum(_dot(h_ref[...], wu_ref[:, c0:c1]), 0.0)
        o_ref[...] += _dot((u * u).astype(BF16), wd_ref[c0:c1, :])

    @pl.when(f == pl.num_programs(1) - 1)
    def _():
        x2 = o_ref[...]
        var = jnp.mean(x2 * x2, axis=-1, keepdims=True)
        o_ref[...] = x2 * lax.rsqrt(var + EPS) * g_ref[...]


def _mlp(h2, x1, w_up, w_down, g_final):
    m, d = x1.shape
    dff = w_up.shape[1]
    return pl.pallas_call(
        _mlp_kernel,
        grid=(m // TM_MLP, dff // TF_MLP),
        in_specs=[
            pl.BlockSpec((TM_MLP, d), lambda i, f: (i, 0)),
            pl.BlockSpec((TM_MLP, d), lambda i, f: (i, 0)),
            pl.BlockSpec((d, TF_MLP), lambda i, f: (0, f)),
            pl.BlockSpec((TF_MLP, d), lambda i, f: (f, 0)),
            pl.BlockSpec((1, d), lambda i, f: (0, 0)),
        ],
        out_specs=pl.BlockSpec((TM_MLP, d), lambda i, f: (i, 0)),
        out_shape=jax.ShapeDtypeStruct((m, d), F32),
        compiler_params=_params(("arbitrary", "arbitrary"), VMEM_LIMIT_MLP),
        name="mlp",
    )(h2, x1, w_up, w_down, g_final)


def kernel(x, g_attn, w_in, b_f, g_out_sb, g_out_fox, w_out, g_mlp, w_up, w_down, g_final):
    b, s, d = x.shape
    n_qkv = 6 * GROUP_W
    assert s % TQ_S == 0 and s % TK_F == 0 and (b * s) % TM_IN == 0
    x2 = x.reshape(b * s, d)

    idx = jnp.arange(TK_S)
    u_excl = (idx[None, :] > idx[:, None]).astype(BF16)
    u2 = jnp.concatenate([u_excl, u_excl], axis=1)
    cidx = jnp.arange(CS_BLK)
    tri_incl = (cidx[None, :] <= cidx[:, None]).astype(BF16)

    gw = GROUP_W
    for l in range(g_attn.shape[0]):
        w_t = jnp.swapaxes(w_in[l], 0, 1)
        w_f_t = jnp.pad(w_t[n_qkv:], ((0, LANES - N_HEADS), (0, 0))).astype(BF16)
        b_pad = jnp.pad(b_f[l], (0, LANES - N_HEADS)).reshape(1, LANES)

        qk, v_t, f_logit = _in_proj(x2, g_attn[l].reshape(1, d), w_t, w_f_t)
        qk = qk.reshape(b, s, N_QK_GROUPS * gw)
        v_t = v_t.reshape(2 * N_HEADS, HEAD_DIM, b * s)
        c, ext_q, ext_k = _forget_cs(f_logit.reshape(b, s, LANES), b_pad, tri_incl)

        mixed_sb, w_out_b = _sb_attn(qk, v_t, g_out_sb[l].reshape(gw, 1), u2, w_out[l])
        mixed_fox, w_up_b, w_down_b = _fox_attn(qk, v_t, c, ext_q, ext_k, g_out_fox[l].reshape(gw, 1),
                                                w_up[l], w_down[l])

        x1, h2 = _out_proj(mixed_sb.reshape(b * s, GROUP_W), mixed_fox.reshape(b * s, GROUP_W),
                           w_out_b, x2, g_mlp[l].reshape(1, d))
        assert g_attn.shape[0] == 1
        x2 = _mlp(h2, x1, w_up_b, w_down_b, g_final.reshape(1, d))
    return x2.reshape(b, s, d)
```

```python
import functools

import jax
import jax.numpy as jnp
from jax import lax
from jax.experimental import pallas as pl
from jax.experimental.pallas import tpu as pltpu

F32 = jnp.float32
BF16 = jnp.bfloat16

HEAD_DIM = 128
N_HEADS = 8
GROUP_W = N_HEADS * HEAD_DIM
EPS = 1e-6
SCALE = HEAD_DIM ** -0.5
LOG2E = 1.4426950408889634
SB_DEAD_LOG2 = -152.0
FOX_DEAD_LOG2 = -152.0
NORM_SLACK = 1.01
M_INIT = -1e30
KN_BLK = 512
LANES = 128

VMEM_LIMIT = 56 * 1024 * 1024
VMEM_LIMIT_MLP = 58 * 1024 * 1024

TM_IN = 1024
TN_IN = 1024
IN_CHUNK = 256
TQ_S = 2048
TK_S = 256
TQ_F = 512
TK_F = 512
N_SUB_F = 8
CS_BLK = 256
CS_CHUNK = 1024
TM_OUT = 512
TM_MLP = 512
TF_MLP = 2048
MLP_CHUNK = 1024


def _params(sem, vmem_limit=VMEM_LIMIT):
    return pltpu.CompilerParams(dimension_semantics=sem, vmem_limit_bytes=vmem_limit)


def _dot(a, b):
    return jnp.dot(a, b, preferred_element_type=F32)


def _dot_nt(a, b):
    return lax.dot_general(a, b, (((1,), (1,)), ((), ())), preferred_element_type=F32)


def _log_sigmoid(x):
    return jnp.minimum(x, 0.0) - jnp.log(1.0 + jnp.exp(-jnp.abs(x)))


N_QK_GROUPS = 4


def _in_proj_kernel(x_ref, g_ref, w_ref, wf_ref, qk_ref, vt_ref, f_ref, h_ref):
    j = pl.program_id(1)

    @pl.when(j == 0)
    def _():
        x = x_ref[...]
        var = jnp.mean(x * x, axis=-1, keepdims=True)
        h = (x * lax.rsqrt(var + EPS) * g_ref[...]).astype(BF16)
        h_ref[...] = h
        f_ref[...] = _dot_nt(h, wf_ref[...])

    @pl.when(j < N_QK_GROUPS)
    def _():
        scale = jnp.where((j == 0) | (j == 2), SCALE * LOG2E, 1.0).astype(F32)
        for c0 in range(0, TN_IN, IN_CHUNK):
            c1 = c0 + IN_CHUNK
            w = w_ref[c0:c1, :].astype(BF16)
            qk_ref[:, c0:c1] = (_dot_nt(h_ref[...], w) * scale).astype(BF16)

    @pl.when(j >= N_QK_GROUPS)
    def _():
        for c0 in range(0, TN_IN, IN_CHUNK):
            c1 = c0 + IN_CHUNK
            w = w_ref[c0:c1, :].astype(BF16)
            vt_ref[c0:c1, :] = _dot_nt(w, h_ref[...]).astype(BF16)


def _w_group(j):
    return jnp.where(j < 2, j, jnp.where(j < 4, j + 1, jnp.where(j == 4, 2, 5)))


def _in_proj(x2, g, w_qkv_t, w_f_t):
    m, d = x2.shape
    n = 6 * GROUP_W
    assert TN_IN == GROUP_W and w_qkv_t.shape[0] >= n
    return pl.pallas_call(
        _in_proj_kernel,
        grid=(m // TM_IN, n // TN_IN),
        in_specs=[
            pl.BlockSpec((TM_IN, d), lambda i, j: (i, 0)),
            pl.BlockSpec((1, d), lambda i, j: (0, 0)),
            pl.BlockSpec((TN_IN, d), lambda i, j: (_w_group(j), 0)),
            pl.BlockSpec((LANES, d), lambda i, j: (0, 0)),
        ],
        out_specs=[
            pl.BlockSpec((TM_IN, TN_IN), lambda i, j: (i, jnp.minimum(j, N_QK_GROUPS - 1))),
            pl.BlockSpec((TN_IN, TM_IN), lambda i, j: (jnp.maximum(j - N_QK_GROUPS, 0), i)),
            pl.BlockSpec((TM_IN, LANES), lambda i, j: (i, 0)),
        ],
        out_shape=[
            jax.ShapeDtypeStruct((m, N_QK_GROUPS * GROUP_W), BF16),
            jax.ShapeDtypeStruct((2 * GROUP_W, m), BF16),
            jax.ShapeDtypeStruct((m, LANES), F32),
        ],
        scratch_shapes=[pltpu.VMEM((TM_IN, d), BF16)],
        compiler_params=_params(("arbitrary", "arbitrary")),
        name="in_proj",
    )(x2, g, w_qkv_t, w_f_t)


def _split3(x):
    p1 = x.astype(BF16)
    r1 = x - p1.astype(F32)
    p2 = r1.astype(BF16)
    p3 = (r1 - p2.astype(F32)).astype(BF16)
    return p1, p2, p3


def _forget_cs_kernel(f_ref, b_ref, tri_ref, selq_ref, selk_ref, oneq_ref, onek_ref,
                      c_ref, extq_ref, extk_ref, carry_ref):
    n_blk = f_ref.shape[0] // CS_BLK
    tri = tri_ref[...]

    @pl.when(pl.program_id(1) == 0)
    def _():
        carry_ref[...] = jnp.zeros_like(carry_ref)

    def body(i, carry):
        r0 = pl.multiple_of(i * CS_BLK, CS_BLK)
        lf = _log_sigmoid(f_ref[pl.ds(r0, CS_BLK), :] + b_ref[...])
        p1, p2, p3 = _split3(lf)
        c = _dot(tri, p1) + _dot(tri, p2) + _dot(tri, p3) + carry
        c2 = c * LOG2E
        c_ref[pl.ds(r0, CS_BLK), :] = c2
        lane = lax.broadcasted_iota(jnp.int32, c2.shape, 1)
        packed = jnp.zeros_like(c2)
        for k, piece in enumerate(_split3(c2)):
            piece = jnp.where(lane < N_HEADS, piece.astype(F32), 0.0)
            packed = packed + (piece if k == 0 else pltpu.roll(piece, N_HEADS * k, axis=1))
        packed = packed.astype(BF16)
        extq_ref[pl.ds(r0, CS_BLK), :] = (_dot(packed, selq_ref[...]) + oneq_ref[...]).astype(BF16)
        extk_ref[pl.ds(r0, CS_BLK), :] = (_dot(packed, selk_ref[...]) + onek_ref[...]).astype(BF16)
        return c[CS_BLK - 1:CS_BLK, :]

    carry_ref[...] = lax.fori_loop(0, n_blk, body, carry_ref[...])


def _aug_constants():
    row = jnp.arange(LANES)
    col = jnp.arange(GROUP_W)
    piece, head = row // N_HEADS, row % N_HEADS
    blk, lane = col // HEAD_DIM, col % HEAD_DIM
    same = (head[:, None] == blk[None, :]) & (piece[:, None] < 3)
    selq = jnp.where(same & (lane[None, :] == 3 + piece[:, None]), 1.0, 0.0).astype(BF16)
    selk = jnp.where(same & (lane[None, :] == piece[:, None]), -1.0, 0.0).astype(BF16)
    oneq = jnp.where(lane < 3, 1.0, 0.0).astype(F32).reshape(1, GROUP_W)
    onek = jnp.where((lane >= 3) & (lane < 6), 1.0, 0.0).astype(F32).reshape(1, GROUP_W)
    return selq, selk, oneq, onek


def _forget_cs(f_logit, b_pad, tri_incl):
    b, s, _ = f_logit.shape
    assert s % CS_CHUNK == 0 and CS_CHUNK % CS_BLK == 0
    const = lambda shape: pl.BlockSpec(shape, lambda i, j: (0, 0))
    rows = lambda width: pl.BlockSpec((None, CS_CHUNK, width), lambda i, j: (i, j, 0))
    return pl.pallas_call(
        _forget_cs_kernel,
        grid=(b, s // CS_CHUNK),
        in_specs=[
            rows(LANES),
            const((1, LANES)),
            const((CS_BLK, CS_BLK)),
            const((LANES, GROUP_W)),
            const((LANES, GROUP_W)),
            const((1, GROUP_W)),
            const((1, GROUP_W)),
        ],
        out_specs=[rows(LANES), rows(GROUP_W), rows(GROUP_W)],
        out_shape=[
            jax.ShapeDtypeStruct((b, s, LANES), F32),
            jax.ShapeDtypeStruct((b, s, GROUP_W), BF16),
            jax.ShapeDtypeStruct((b, s, GROUP_W), BF16),
        ],
        scratch_shapes=[pltpu.VMEM((1, LANES), F32)],
        compiler_params=_params(("arbitrary", "arbitrary")),
        name="forget_cs",
    )(f_logit, b_pad, tri_incl, *_aug_constants())


def _head_rmsnorm(o, g):
    return o * lax.rsqrt(jnp.mean(o * o, axis=-1, keepdims=True) + EPS) * g


def _col_reduce(x, pair, full):
    while x.shape[0] > 8:
        half = x.shape[0] // 2
        x = pair(x[:half], x[half:])
    return full(x, axis=0, keepdims=True)


def _head_out(acc_t, g_col):
    o = acc_t * lax.rsqrt(jnp.mean(acc_t * acc_t, axis=0, keepdims=True) + EPS) * g_col
    return jnp.transpose(o)


def _store_head_out(o_ref, acc_t, g_col):
    o_ref[...] = _head_out(acc_t, g_col).astype(o_ref.dtype)


def _cast_rows_spec(w, n_steps, step_of):
    rows, cols = w.shape
    assert rows % n_steps == 0 and (rows // n_steps) % 16 == 0
    return pl.BlockSpec((rows // n_steps, cols), lambda bi, h, qi: (step_of(bi, h, qi), 0))


def _cast_side_job(w_refs, wb_refs):
    for w_ref, wb_ref in zip(w_refs, wb_refs):
        wb_ref[...] = w_ref[...].astype(BF16)


def _sb_kernel(n_cast, q_ref, k_ref, vt_ref, g_ref, u2_ref, *rest):
    o_ref = rest[n_cast]
    _cast_side_job(rest[:n_cast], rest[n_cast + 1:])

    qi = pl.program_id(2)
    u2 = u2_ref[...]
    n_sub = TQ_S // TK_S
    key = lax.broadcasted_iota(jnp.int32, (TK_S, TK_S), 0)
    qry = lax.broadcasted_iota(jnp.int32, (TK_S, TK_S), 1)
    causal = key < qry

    def stage_q(q, kb):
        ks = pl.multiple_of(kb * TK_S, TK_S)
        return _dot_nt(k_ref[pl.ds(ks, TK_S), :], q)

    def stage_e(z, diagonal):
        zn = jnp.minimum(z, 0.0)
        zp = zn - z
        sp = jnp.log2(1.0 + jnp.exp2(zn + zp))
        lb = zn - sp
        lk = zp - sp
        if diagonal:
            lk = jnp.where(causal, lk, 0.0)
        hi = lk.astype(BF16)
        lo = (lk - hi.astype(F32)).astype(BF16)
        return lb, jnp.concatenate([hi, lo], axis=0), lk[:1, :]

    def stage_c(hilo, lk_first):
        rest = _dot(u2, hilo)
        return rest, rest[:1, :] + lk_first

    def stage_x(lb, rest, carry, diagonal):
        if diagonal:
            a = jnp.where(causal, jnp.exp2(lb + rest), 0.0)
        else:
            a = jnp.exp2(lb + (rest + carry))
        return a.astype(BF16)

    def stage_v(kb, a, gate=None):
        vt = vt_ref[:, pl.ds(pl.multiple_of(kb * TK_S, TK_S), TK_S)]
        if gate is not None:
            vt = vt * gate
        return _dot(vt, a)

    items = []
    for sub in range(n_sub):
        kd = qi * n_sub + sub
        q = q_ref[sub * TK_S:(sub + 1) * TK_S, :]
        items.append(dict(q=q, kb=kd, diagonal=True, gate=None))
        items.append(dict(q=q, kb=jnp.maximum(kd - 1, 0), diagonal=False,
                          gate=jnp.where(kd > 0, 1.0, 0.0).astype(BF16)))
    n_items = len(items)
    for t in range(n_items + 4):
        if 4 <= t:
            it = items[t - 4]
            it["pv"] = stage_v(it["kb"], it["a"], it["gate"])
        if 2 <= t < n_items + 2:
            it = items[t - 2]
            it["rest"], it["colsum"] = stage_c(it["hilo"], it["lk0"])
        if t < n_items:
            it = items[t]
            it["z"] = stage_q(it["q"], it["kb"])
        if 3 <= t < n_items + 3:
            it = items[t - 3]
            carry_in = None if it["diagonal"] else items[t - 4]["colsum"]
            it["a"] = stage_x(it["lb"], it["rest"], carry_in, it["diagonal"])
        if 1 <= t < n_items + 1:
            it = items[t - 1]
            it["lb"], it["hilo"], it["lk0"] = stage_e(it["z"], it["diagonal"])

    qs = [items[2 * r]["q"] for r in range(n_sub)]
    state = []
    for r in range(n_sub):
        d, p = items[2 * r], items[2 * r + 1]
        state += [d["colsum"] + p["colsum"], d["pv"] + p["pv"]]

    kd_last = qi * n_sub + n_sub - 1

    def cond(st):
        worst = st[1]
        for r in range(1, n_sub):
            worst = jnp.maximum(worst, st[1 + 2 * r])
        return jnp.logical_and(kd_last - 2 - st[0] >= 0, jnp.max(worst) > SB_DEAD_LOG2)

    def body(st):
        j, out = st[0], []
        for r in range(n_sub):
            kb = qi * n_sub + r - 2 - j
            gate = jnp.where(kb >= 0, 1.0, 0.0).astype(BF16)
            kb = jnp.maximum(kb, 0)
            lb, hilo, lk0 = stage_e(stage_q(qs[r], kb), False)
            rest, colsum = stage_c(hilo, lk0)
            a = stage_x(lb, rest, st[1 + 2 * r], False)
            out += [st[1 + 2 * r] + colsum, st[2 + 2 * r] + stage_v(kb, a, gate)]
        return (j + 1, *out)

    st = lax.while_loop(cond, body, (0, *state))
    _store_head_out(o_ref, jnp.concatenate([st[2 + 2 * r] for r in range(n_sub)], axis=1), g_ref[...])


def _sb_attn(qk, v_t, g_col, u2, w_cast):
    b, s, _ = qk.shape
    assert TQ_S % TK_S == 0
    nq = s // TQ_S
    w_specs = [_cast_rows_spec(w, b * N_HEADS * nq, lambda bi, h, qi: (bi * N_HEADS + h) * nq + qi)
               for w in w_cast]
    return pl.pallas_call(
        functools.partial(_sb_kernel, len(w_cast)),
        grid=(b, N_HEADS, nq),
        in_specs=[
            pl.BlockSpec((None, TQ_S, HEAD_DIM), lambda bi, h, qi: (bi, qi, h)),
            pl.BlockSpec((None, s, HEAD_DIM), lambda bi, h, qi: (bi, 0, N_HEADS + h)),
            pl.BlockSpec((None, HEAD_DIM, s), lambda bi, h, qi: (h, 0, bi)),
            pl.BlockSpec((HEAD_DIM, 1), lambda bi, h, qi: (h, 0)),
            pl.BlockSpec((TK_S, 2 * TK_S), lambda bi, h, qi: (0, 0)),
            *w_specs,
        ],
        out_specs=[pl.BlockSpec((None, TQ_S, HEAD_DIM), lambda bi, h, qi: (bi, qi, h)), *w_specs],
        out_shape=[jax.ShapeDtypeStruct((b, s, GROUP_W), BF16),
                   *[jax.ShapeDtypeStruct(w.shape, BF16) for w in w_cast]],
        compiler_params=_params(("arbitrary", "arbitrary", "arbitrary")),
        name="sb_attn",
    )(qk, qk, v_t, g_col, u2, *w_cast)


def _fox_kernel(n_cast, q_ref, k_ref, extq_ref, extk_ref, vt_ref, cq_ref, clast_ref, g_ref, *rest):
    o_ref = rest[n_cast]
    _cast_side_job(rest[:n_cast], rest[n_cast + 1:2 * n_cast + 1])
    kn_ref, z_ref, p_ref, m_ref, l_ref, a_ref, acc_ref = rest[2 * n_cast + 1:]

    bi, h, qs = pl.program_id(0), pl.program_id(1), pl.program_id(2)
    last_tile = vt_ref.shape[1] // TK_F - 1

    def sq_norms_ub(rows):
        rf = rows.astype(F32)
        r2_up = (rf * rf * (1.0 + 2.0 ** -7)).astype(BF16)
        return _dot_nt(jnp.ones((8, HEAD_DIM), BF16), r2_up)[:1]

    @pl.when(qs == 0)
    def _():
        best = sq_norms_ub(k_ref[:KN_BLK, :])
        for r0 in range(KN_BLK, k_ref.shape[0], KN_BLK):
            best = jnp.maximum(best, sq_norms_ub(k_ref[r0:r0 + KN_BLK, :]))
        kn_ref[...] = jnp.broadcast_to(jnp.sqrt(jnp.max(best, axis=1, keepdims=True)), kn_ref.shape)

    def make_block(sub):
        qi = qs * N_SUB_F + sub
        rows = slice(sub * TQ_F, (sub + 1) * TQ_F)
        n_full = (qi * TQ_F) // TK_F
        q = q_ref[rows, :]
        q_aug = jnp.concatenate([q, extq_ref[rows, :]], axis=1)
        ub = jnp.sqrt(sq_norms_ub(q)) * kn_ref[:, :1] * NORM_SLACK + cq_ref[:, rows]

        def kb_of(i):
            return jnp.clip(n_full - i, 0, last_tile)

        def qk_t(i):
            ks = pl.multiple_of(kb_of(i) * TK_F, TK_F)
            k_aug = jnp.concatenate([k_ref[pl.ds(ks, TK_F), :], extk_ref[pl.ds(ks, TK_F), :]], axis=1)
            return _dot_nt(k_aug, q_aug)

        def pv_t(i, slot):
            ks = pl.multiple_of(kb_of(i) * TK_F, TK_F)
            return _dot(vt_ref[:, pl.ds(ks, TK_F)], p_ref[sub, slot])

        def prologue():
            st = qk_t(0)
            z_ref[sub, 1] = qk_t(1)
            key = n_full * TK_F + lax.broadcasted_iota(jnp.int32, (TK_F, TQ_F), 0)
            qry = qi * TQ_F + lax.broadcasted_iota(jnp.int32, (TK_F, TQ_F), 1)
            st = jnp.where(key <= qry, st, -jnp.inf)
            m0 = _col_reduce(st, jnp.maximum, jnp.max)
            p0 = jnp.exp2(st - m0)
            p_ref[sub, 0] = p0.astype(BF16)
            m_ref[sub] = m0
            l_ref[sub] = _col_reduce(p0, jnp.add, jnp.sum)
            a_ref[sub] = jnp.ones_like(m0)
            acc_ref[sub] = jnp.zeros(acc_ref.shape[1:], F32)
            gap = jnp.max(ub - m0)
            n_tiles = 1
            for i in range(clast_ref.shape[2]):
                alive = jnp.logical_and(i < n_full, gap - clast_ref[bi, h, i] > FOX_DEAD_LOG2)
                n_tiles = n_tiles + alive.astype(jnp.int32)
            return n_tiles

        def step(i, slot, last=False):
            if not last:
                z_ref[sub, 1 - slot] = qk_t(i + 1)
            acc_ref[sub] = a_ref[sub] * acc_ref[sub] + pv_t(i - 1, 1 - slot)
            m = m_ref[sub]
            m_new = jnp.maximum(m, _col_reduce(z_ref[sub, slot], jnp.maximum, jnp.max))
            alpha = jnp.exp2(m - m_new)
            p = jnp.exp2(z_ref[sub, slot] - m_new)
            p_ref[sub, slot] = p.astype(BF16)
            l_ref[sub] = alpha * l_ref[sub] + _col_reduce(p, jnp.add, jnp.sum)
            m_ref[sub] = m_new
            a_ref[sub] = alpha

        def walk(n_tiles):
            n_in = n_tiles - 1

            def body(j, _):
                i = 1 + 2 * j
                step(i, 1)
                step(i + 1, 0)
                return 0

            lax.fori_loop(0, n_in // 2, body, 0)

            @pl.when(n_in % 2 == 1)
            def _():
                step(n_in, 1, last=True)

        def epilogue(n_tiles):
            acc = a_ref[sub] * acc_ref[sub] + pv_t(n_tiles - 1, (n_tiles - 1) % 2)
            o_ref[rows, :] = _head_out(acc / l_ref[sub], g_ref[...]).astype(o_ref.dtype)

        return prologue, walk, epilogue

    blocks = [make_block(sub) for sub in range(N_SUB_F)]
    n_tiles = [prologue() for prologue, _, _ in blocks]
    for (_, walk, _), n in zip(blocks, n_tiles):
        walk(n)
    for (_, _, epilogue), n in zip(blocks, n_tiles):
        epilogue(n)


def _fox_attn(qk, v_t, c, ext_q, ext_k, g_col, w_cast):
    b, s, _ = qk.shape
    base = 2 * N_HEADS
    n_kt = s // TK_F
    tq_step = N_SUB_F * TQ_F
    nq = s // tq_step
    assert TK_F % TQ_F == 0 and s % tq_step == 0
    step_of = lambda bi, h, qi: (bi * N_HEADS + h) * nq + qi
    w_specs = [_cast_rows_spec(w, b * N_HEADS * nq, step_of) for w in w_cast]
    c_hs = jnp.transpose(c[:, :, :N_HEADS], (0, 2, 1))
    c_q = c_hs.reshape(b, N_HEADS, nq, 1, tq_step)
    c_last = c_hs.reshape(b, N_HEADS, n_kt, TK_F)[..., TK_F - 1]
    return pl.pallas_call(
        functools.partial(_fox_kernel, len(w_cast)),
        grid=(b, N_HEADS, nq),
        in_specs=[
            pl.BlockSpec((None, tq_step, HEAD_DIM), lambda bi, h, qi: (bi, qi, base + h)),
            pl.BlockSpec((None, s, HEAD_DIM), lambda bi, h, qi: (bi, 0, base + N_HEADS + h)),
            pl.BlockSpec((None, tq_step, HEAD_DIM), lambda bi, h, qi: (bi, qi, h)),
            pl.BlockSpec((None, s, HEAD_DIM), lambda bi, h, qi: (bi, 0, h)),
            pl.BlockSpec((None, HEAD_DIM, s), lambda bi, h, qi: (N_HEADS + h, 0, bi)),
            pl.BlockSpec((None, None, None, 1, tq_step), lambda bi, h, qi: (bi, h, qi, 0, 0)),
            pl.BlockSpec(memory_space=pltpu.SMEM),
            pl.BlockSpec((HEAD_DIM, 1), lambda bi, h, qi: (h, 0)),
            *w_specs,
        ],
        out_specs=[pl.BlockSpec((None, tq_step, HEAD_DIM), lambda bi, h, qi: (bi, qi, h)), *w_specs],
        scratch_shapes=[
            pltpu.VMEM((1, LANES), F32),
            pltpu.VMEM((N_SUB_F, 2, TK_F, TQ_F), F32),
            pltpu.VMEM((N_SUB_F, 2, TK_F, TQ_F), BF16),
            pltpu.VMEM((N_SUB_F, 1, TQ_F), F32),
            pltpu.VMEM((N_SUB_F, 1, TQ_F), F32),
            pltpu.VMEM((N_SUB_F, 1, TQ_F), F32),
            pltpu.VMEM((N_SUB_F, HEAD_DIM, TQ_F), F32),
        ],
        out_shape=[jax.ShapeDtypeStruct((b, s, GROUP_W), BF16),
                   *[jax.ShapeDtypeStruct(w.shape, BF16) for w in w_cast]],
        compiler_params=_params(("arbitrary", "arbitrary", "arbitrary")),
        name="fox_attn",
    )(qk, qk, ext_q, ext_k, v_t, c_q, c_last, g_col, *w_cast)


def _out_proj_kernel(ms_ref, mf_ref, w_ref, x_ref, g_ref, x1_ref, h2_ref):
    acc = _dot(ms_ref[...], w_ref[:GROUP_W, :]) + _dot(mf_ref[...], w_ref[GROUP_W:, :])
    x1 = x_ref[...] + acc
    x1_ref[...] = x1
    var = jnp.mean(x1 * x1, axis=-1, keepdims=True)
    h2_ref[...] = (x1 * lax.rsqrt(var + EPS) * g_ref[...]).astype(BF16)


def _out_proj(mixed_sb, mixed_fox, w_out, x2, g_mlp):
    m, d = x2.shape
    return pl.pallas_call(
        _out_proj_kernel,
        grid=(m // TM_OUT,),
        in_specs=[
            pl.BlockSpec((TM_OUT, GROUP_W), lambda i: (i, 0)),
            pl.BlockSpec((TM_OUT, GROUP_W), lambda i: (i, 0)),
            pl.BlockSpec((2 * GROUP_W, d), lambda i: (0, 0)),
            pl.BlockSpec((TM_OUT, d), lambda i: (i, 0)),
            pl.BlockSpec((1, d), lambda i: (0, 0)),
        ],
        out_specs=[
            pl.BlockSpec((TM_OUT, d), lambda i: (i, 0)),
            pl.BlockSpec((TM_OUT, d), lambda i: (i, 0)),
        ],
        out_shape=[
            jax.ShapeDtypeStruct((m, d), F32),
            jax.ShapeDtypeStruct((m, d), BF16),
        ],
        compiler_params=_params(("arbitrary",)),
        name="out_proj",
    )(mixed_sb, mixed_fox, w_out, x2, g_mlp)


def _mlp_kernel(h_ref, x1_ref, wu_ref, wd_ref, g_ref, o_ref):
    f = pl.program_id(1)

    @pl.when(f == 0)
    def _():
        o_ref[...] = x1_ref[...]

    for c0 in range(0, wu_ref.shape[1], MLP_CHUNK):
        c1 = c0 + MLP_CHUNK
        u = jnp.maximum(_dot(h_ref[...], wu_ref[:, c0:c1]), 0.0)
        o_ref[...] += _dot((u * u).astype(BF16), wd_ref[c0:c1, :])

    @pl.when(f == pl.num_programs(1) - 1)
    def _():
        x2 = o_ref[...]
        var = jnp.mean(x2 * x2, axis=-1, keepdims=True)
        o_ref[...] = x2 * lax.rsqrt(var + EPS) * g_ref[...]


def _mlp(h2, x1, w_up, w_down, g_final):
    m, d = x1.shape
    dff = w_up.shape[1]
    return pl.pallas_call(
        _mlp_kernel,
        grid=(m // TM_MLP, dff // TF_MLP),
        in_specs=[
            pl.BlockSpec((TM_MLP, d), lambda i, f: (i, 0)),
            pl.BlockSpec((TM_MLP, d), lambda i, f: (i, 0)),
            pl.BlockSpec((d, TF_MLP), lambda i, f: (0, f)),
            pl.BlockSpec((TF_MLP, d), lambda i, f: (f, 0)),
            pl.BlockSpec((1, d), lambda i, f: (0, 0)),
        ],
        out_specs=pl.BlockSpec((TM_MLP, d), lambda i, f: (i, 0)),
        out_shape=jax.ShapeDtypeStruct((m, d), F32),
        compiler_params=_params(("arbitrary", "arbitrary"), VMEM_LIMIT_MLP),
        name="mlp",
    )(h2, x1, w_up, w_down, g_final)


def kernel(x, g_attn, w_in, b_f, g_out_sb, g_out_fox, w_out, g_mlp, w_up, w_down, g_final):
    b, s, d = x.shape
    n_qkv = 6 * GROUP_W
    assert s % TQ_S == 0 and s % TK_F == 0 and (b * s) % TM_IN == 0
    x2 = x.reshape(b * s, d)

    idx = jnp.arange(TK_S)
    u_excl = (idx[None, :] > idx[:, None]).astype(BF16)
    u2 = jnp.concatenate([u_excl, u_excl], axis=1)
    cidx = jnp.arange(CS_BLK)
    tri_incl = (cidx[None, :] <= cidx[:, None]).astype(BF16)

    gw = GROUP_W
    for l in range(g_attn.shape[0]):
        w_t = jnp.swapaxes(w_in[l], 0, 1)
        w_f_t = jnp.pad(w_t[n_qkv:], ((0, LANES - N_HEADS), (0, 0))).astype(BF16)
        b_pad = jnp.pad(b_f[l], (0, LANES - N_HEADS)).reshape(1, LANES)

        qk, v_t, f_logit = _in_proj(x2, g_attn[l].reshape(1, d), w_t, w_f_t)
        qk = qk.reshape(b, s, N_QK_GROUPS * gw)
        v_t = v_t.reshape(2 * N_HEADS, HEAD_DIM, b * s)
        c, ext_q, ext_k = _forget_cs(f_logit.reshape(b, s, LANES), b_pad, tri_incl)

        mixed_sb, w_up_b, w_down_b = _sb_attn(qk, v_t, g_out_sb[l].reshape(gw, 1), u2,
                                              (w_up[l], w_down[l]))
        mixed_fox, w_out_b = _fox_attn(qk, v_t, c, ext_q, ext_k, g_out_fox[l].reshape(gw, 1), (w_out[l],))

        x1, h2 = _out_proj(mixed_sb.reshape(b * s, GROUP_W), mixed_fox.reshape(b * s, GROUP_W),
                           w_out_b, x2, g_mlp[l].reshape(1, d))
        assert g_attn.shape[0] == 1
        x2 = _mlp(h2, x1, w_up_b, w_down_b, g_final.reshape(1, d))
    return x2.reshape(b, s, d)
```

```python
import functools

import jax
import jax.numpy as jnp
from jax import lax
from jax.experimental import pallas as pl
from jax.experimental.pallas import tpu as pltpu

F32 = jnp.float32
BF16 = jnp.bfloat16

HEAD_DIM = 128
N_HEADS = 8
GROUP_W = N_HEADS * HEAD_DIM
EPS = 1e-6
SCALE = HEAD_DIM ** -0.5
LOG2E = 1.4426950408889634
SB_DEAD_LOG2 = -152.0
FOX_DEAD_LOG2 = -152.0
NORM_SLACK = 1.01
M_INIT = -1e30
KN_BLK = 512
LANES = 128

VMEM_LIMIT = 56 * 1024 * 1024
VMEM_LIMIT_MLP = 58 * 1024 * 1024

TM_IN = 1024
TN_IN = 1024
IN_CHUNK = 256
TQ_S = 2048
TK_S = 256
TQ_F = 512
TK_F = 512
N_SUB_F = 8
CS_BLK = 256
CS_CHUNK = 1024
TM_OUT = 512
TM_MLP = 512
TF_MLP = 2048
MLP_CHUNK = 1024


def _params(sem, vmem_limit=VMEM_LIMIT):
    return pltpu.CompilerParams(dimension_semantics=sem, vmem_limit_bytes=vmem_limit)


def _dot(a, b):
    return jnp.dot(a, b, preferred_element_type=F32)


def _dot_nt(a, b):
    return lax.dot_general(a, b, (((1,), (1,)), ((), ())), preferred_element_type=F32)


def _log_sigmoid(x):
    return jnp.minimum(x, 0.0) - jnp.log(1.0 + jnp.exp(-jnp.abs(x)))


N_QK_GROUPS = 4


def _in_proj_kernel(x_ref, g_ref, w_ref, wf_ref, qk_ref, vt_ref, f_ref, h_ref):
    j = pl.program_id(1)

    @pl.when(j == 0)
    def _():
        x = x_ref[...]
        var = jnp.mean(x * x, axis=-1, keepdims=True)
        h = (x * lax.rsqrt(var + EPS) * g_ref[...]).astype(BF16)
        h_ref[...] = h
        f_ref[...] = _dot_nt(h, wf_ref[...])

    @pl.when(j < N_QK_GROUPS)
    def _():
        scale = jnp.where((j == 0) | (j == 2), SCALE * LOG2E, 1.0).astype(F32)
        for c0 in range(0, TN_IN, IN_CHUNK):
            c1 = c0 + IN_CHUNK
            w = w_ref[c0:c1, :].astype(BF16)
            qk_ref[:, c0:c1] = (_dot_nt(h_ref[...], w) * scale).astype(BF16)

    @pl.when(j >= N_QK_GROUPS)
    def _():
        for c0 in range(0, TN_IN, IN_CHUNK):
            c1 = c0 + IN_CHUNK
            w = w_ref[c0:c1, :].astype(BF16)
            vt_ref[c0:c1, :] = _dot_nt(w, h_ref[...]).astype(BF16)


def _w_group(j):
    return jnp.where(j < 2, j, jnp.where(j < 4, j + 1, jnp.where(j == 4, 2, 5)))


def _in_proj(x2, g, w_qkv_t, w_f_t):
    m, d = x2.shape
    n = 6 * GROUP_W
    assert TN_IN == GROUP_W and w_qkv_t.shape[0] >= n
    return pl.pallas_call(
        _in_proj_kernel,
        grid=(m // TM_IN, n // TN_IN),
        in_specs=[
            pl.BlockSpec((TM_IN, d), lambda i, j: (i, 0)),
            pl.BlockSpec((1, d), lambda i, j: (0, 0)),
            pl.BlockSpec((TN_IN, d), lambda i, j: (_w_group(j), 0)),
            pl.BlockSpec((LANES, d), lambda i, j: (0, 0)),
        ],
        out_specs=[
            pl.BlockSpec((TM_IN, TN_IN), lambda i, j: (i, jnp.minimum(j, N_QK_GROUPS - 1))),
            pl.BlockSpec((TN_IN, TM_IN), lambda i, j: (jnp.maximum(j - N_QK_GROUPS, 0), i)),
            pl.BlockSpec((TM_IN, LANES), lambda i, j: (i, 0)),
        ],
        out_shape=[
            jax.ShapeDtypeStruct((m, N_QK_GROUPS * GROUP_W), BF16),
            jax.ShapeDtypeStruct((2 * GROUP_W, m), BF16),
            jax.ShapeDtypeStruct((m, LANES), F32),
        ],
        scratch_shapes=[pltpu.VMEM((TM_IN, d), BF16)],
        compiler_params=_params(("arbitrary", "arbitrary")),
        name="in_proj",
    )(x2, g, w_qkv_t, w_f_t)


def _split3(x):
    p1 = x.astype(BF16)
    r1 = x - p1.astype(F32)
    p2 = r1.astype(BF16)
    p3 = (r1 - p2.astype(F32)).astype(BF16)
    return p1, p2, p3


def _forget_cs_kernel(f_ref, b_ref, tri_ref, selq_ref, selk_ref, oneq_ref, onek_ref,
                      c_ref, extq_ref, extk_ref, carry_ref):
    n_blk = f_ref.shape[0] // CS_BLK
    tri = tri_ref[...]

    @pl.when(pl.program_id(1) == 0)
    def _():
        carry_ref[...] = jnp.zeros_like(carry_ref)

    def body(i, carry):
        r0 = pl.multiple_of(i * CS_BLK, CS_BLK)
        lf = _log_sigmoid(f_ref[pl.ds(r0, CS_BLK), :] + b_ref[...])
        p1, p2, p3 = _split3(lf)
        c = _dot(tri, p1) + _dot(tri, p2) + _dot(tri, p3) + carry
        c2 = c * LOG2E
        c_ref[pl.ds(r0, CS_BLK), :] = c2
        lane = lax.broadcasted_iota(jnp.int32, c2.shape, 1)
        packed = jnp.zeros_like(c2)
        for k, piece in enumerate(_split3(c2)):
            piece = jnp.where(lane < N_HEADS, piece.astype(F32), 0.0)
            packed = packed + (piece if k == 0 else pltpu.roll(piece, N_HEADS * k, axis=1))
        packed = packed.astype(BF16)
        extq_ref[pl.ds(r0, CS_BLK), :] = (_dot(packed, selq_ref[...]) + oneq_ref[...]).astype(BF16)
        extk_ref[pl.ds(r0, CS_BLK), :] = (_dot(packed, selk_ref[...]) + onek_ref[...]).astype(BF16)
        return c[CS_BLK - 1:CS_BLK, :]

    carry_ref[...] = lax.fori_loop(0, n_blk, body, carry_ref[...])


def _aug_constants():
    row = jnp.arange(LANES)
    col = jnp.arange(GROUP_W)
    piece, head = row // N_HEADS, row % N_HEADS
    blk, lane = col // HEAD_DIM, col % HEAD_DIM
    same = (head[:, None] == blk[None, :]) & (piece[:, None] < 3)
    selq = jnp.where(same & (lane[None, :] == 3 + piece[:, None]), 1.0, 0.0).astype(BF16)
    selk = jnp.where(same & (lane[None, :] == piece[:, None]), -1.0, 0.0).astype(BF16)
    oneq = jnp.where(lane < 3, 1.0, 0.0).astype(F32).reshape(1, GROUP_W)
    onek = jnp.where((lane >= 3) & (lane < 6), 1.0, 0.0).astype(F32).reshape(1, GROUP_W)
    return selq, selk, oneq, onek


def _forget_cs(f_logit, b_pad, tri_incl):
    b, s, _ = f_logit.shape
    assert s % CS_CHUNK == 0 and CS_CHUNK % CS_BLK == 0
    const = lambda shape: pl.BlockSpec(shape, lambda i, j: (0, 0))
    rows = lambda width: pl.BlockSpec((None, CS_CHUNK, width), lambda i, j: (i, j, 0))
    return pl.pallas_call(
        _forget_cs_kernel,
        grid=(b, s // CS_CHUNK),
        in_specs=[
            rows(LANES),
            const((1, LANES)),
            const((CS_BLK, CS_BLK)),
            const((LANES, GROUP_W)),
            const((LANES, GROUP_W)),
            const((1, GROUP_W)),
            const((1, GROUP_W)),
        ],
        out_specs=[rows(LANES), rows(GROUP_W), rows(GROUP_W)],
        out_shape=[
            jax.ShapeDtypeStruct((b, s, LANES), F32),
            jax.ShapeDtypeStruct((b, s, GROUP_W), BF16),
            jax.ShapeDtypeStruct((b, s, GROUP_W), BF16),
        ],
        scratch_shapes=[pltpu.VMEM((1, LANES), F32)],
        compiler_params=_params(("arbitrary", "arbitrary")),
        name="forget_cs",
    )(f_logit, b_pad, tri_incl, *_aug_constants())


def _col_reduce(x, pair, full):
    while x.shape[0] > 8:
        half = x.shape[0] // 2
        x = pair(x[:half], x[half:])
    return full(x, axis=0, keepdims=True)


def _head_out(acc_t, g_col):
    o = acc_t * lax.rsqrt(jnp.mean(acc_t * acc_t, axis=0, keepdims=True) + EPS) * g_col
    return jnp.transpose(o)


def _store_head_out(o_ref, acc_t, g_col):
    o_ref[...] = _head_out(acc_t, g_col).astype(o_ref.dtype)


def _cast_rows_spec(w, n_steps, step_of):
    rows, cols = w.shape
    assert rows % n_steps == 0 and (rows // n_steps) % 16 == 0
    return pl.BlockSpec((rows // n_steps, cols), lambda bi, h, qi: (step_of(bi, h, qi), 0))


def _cast_side_job(w_refs, wb_refs):
    for w_ref, wb_ref in zip(w_refs, wb_refs):
        wb_ref[...] = w_ref[...].astype(BF16)


def _sb_kernel(n_cast, q_ref, k_ref, vt_ref, g_ref, u2_ref, *rest):
    o_ref = rest[n_cast]
    _cast_side_job(rest[:n_cast], rest[n_cast + 1:])

    qi = pl.program_id(2)
    u2 = u2_ref[...]
    n_sub = TQ_S // TK_S
    key = lax.broadcasted_iota(jnp.int32, (TK_S, TK_S), 0)
    qry = lax.broadcasted_iota(jnp.int32, (TK_S, TK_S), 1)
    causal = key < qry

    def stage_q(q, kb):
        ks = pl.multiple_of(kb * TK_S, TK_S)
        return _dot_nt(k_ref[pl.ds(ks, TK_S), :], q)

    def stage_e(z, diagonal):
        zn = jnp.minimum(z, 0.0)
        zp = zn - z
        sp = jnp.log2(1.0 + jnp.exp2(zn + zp))
        lb = zn - sp
        lk = zp - sp
        if diagonal:
            lk = jnp.where(causal, lk, 0.0)
        hi = lk.astype(BF16)
        lo = (lk - hi.astype(F32)).astype(BF16)
        return lb, jnp.concatenate([hi, lo], axis=0), lk[:1, :]

    def stage_c(hilo, lk_first):
        rest = _dot(u2, hilo)
        return rest, rest[:1, :] + lk_first

    def stage_x(lb, rest, carry, diagonal):
        if diagonal:
            a = jnp.where(causal, jnp.exp2(lb + rest), 0.0)
        else:
            a = jnp.exp2(lb + (rest + carry))
        return a.astype(BF16)

    def stage_v(kb, a, gate=None):
        vt = vt_ref[:, pl.ds(pl.multiple_of(kb * TK_S, TK_S), TK_S)]
        if gate is not None:
            vt = vt * gate
        return _dot(vt, a)

    items = []
    for sub in range(n_sub):
        kd = qi * n_sub + sub
        q = q_ref[sub * TK_S:(sub + 1) * TK_S, :]
        items.append(dict(q=q, kb=kd, diagonal=True, gate=None))
        items.append(dict(q=q, kb=jnp.maximum(kd - 1, 0), diagonal=False,
                          gate=jnp.where(kd > 0, 1.0, 0.0).astype(BF16)))
    n_items = len(items)
    for t in range(n_items + 4):
        if 4 <= t:
            it = items[t - 4]
            it["pv"] = stage_v(it["kb"], it["a"], it["gate"])
        if 2 <= t < n_items + 2:
            it = items[t - 2]
            it["rest"], it["colsum"] = stage_c(it["hilo"], it["lk0"])
        if t < n_items:
            it = items[t]
            it["z"] = stage_q(it["q"], it["kb"])
        if 3 <= t < n_items + 3:
            it = items[t - 3]
            carry_in = None if it["diagonal"] else items[t - 4]["colsum"]
            it["a"] = stage_x(it["lb"], it["rest"], carry_in, it["diagonal"])
        if 1 <= t < n_items + 1:
            it = items[t - 1]
            it["lb"], it["hilo"], it["lk0"] = stage_e(it["z"], it["diagonal"])

    qs = [items[2 * r]["q"] for r in range(n_sub)]
    state = []
    for r in range(n_sub):
        d, p = items[2 * r], items[2 * r + 1]
        state += [d["colsum"] + p["colsum"], d["pv"] + p["pv"]]

    kd_last = qi * n_sub + n_sub - 1

    def cond(st):
        worst = st[1]
        for r in range(1, n_sub):
            worst = jnp.maximum(worst, st[1 + 2 * r])
        return jnp.logical_and(kd_last - 2 - st[0] >= 0, jnp.max(worst) > SB_DEAD_LOG2)

    def body(st):
        j, out = st[0], []
        for r in range(n_sub):
            kb = qi * n_sub + r - 2 - j
            gate = jnp.where(kb >= 0, 1.0, 0.0).astype(BF16)
            kb = jnp.maximum(kb, 0)
            lb, hilo, lk0 = stage_e(stage_q(qs[r], kb), False)
            rest, colsum = stage_c(hilo, lk0)
            a = stage_x(lb, rest, st[1 + 2 * r], False)
            out += [st[1 + 2 * r] + colsum, st[2 + 2 * r] + stage_v(kb, a, gate)]
        return (j + 1, *out)

    st = lax.while_loop(cond, body, (0, *state))
    _store_head_out(o_ref, jnp.concatenate([st[2 + 2 * r] for r in range(n_sub)], axis=1), g_ref[...])


def _sb_attn(qk, v_t, g_col, u2, w_cast):
    b, s, _ = qk.shape
    assert TQ_S % TK_S == 0
    nq = s // TQ_S
    w_specs = [_cast_rows_spec(w, b * N_HEADS * nq, lambda bi, h, qi: (bi * N_HEADS + h) * nq + qi)
               for w in w_cast]
    return pl.pallas_call(
        functools.partial(_sb_kernel, len(w_cast)),
        grid=(b, N_HEADS, nq),
        in_specs=[
            pl.BlockSpec((None, TQ_S, HEAD_DIM), lambda bi, h, qi: (bi, qi, h)),
            pl.BlockSpec((None, s, HEAD_DIM), lambda bi, h, qi: (bi, 0, N_HEADS + h)),
            pl.BlockSpec((None, HEAD_DIM, s), lambda bi, h, qi: (h, 0, bi)),
            pl.BlockSpec((HEAD_DIM, 1), lambda bi, h, qi: (h, 0)),
            pl.BlockSpec((TK_S, 2 * TK_S), lambda bi, h, qi: (0, 0)),
            *w_specs,
        ],
        out_specs=[pl.BlockSpec((None, TQ_S, HEAD_DIM), lambda bi, h, qi: (bi, qi, h)), *w_specs],
        out_shape=[jax.ShapeDtypeStruct((b, s, GROUP_W), BF16),
                   *[jax.ShapeDtypeStruct(w.shape, BF16) for w in w_cast]],
        compiler_params=_params(("arbitrary", "arbitrary", "arbitrary")),
        name="sb_attn",
    )(qk, qk, v_t, g_col, u2, *w_cast)


def _fox_kernel(n_cast, q_ref, k_ref, extq_ref, extk_ref, vt_ref, cq_ref, clast_ref, g_ref, *rest):
    o_ref = rest[n_cast]
    _cast_side_job(rest[:n_cast], rest[n_cast + 1:2 * n_cast + 1])
    kn_ref, z_ref, p_ref, m_ref, l_ref, a_ref, acc_ref = rest[2 * n_cast + 1:]

    bi, h, qs = pl.program_id(0), pl.program_id(1), pl.program_id(2)
    last_tile = vt_ref.shape[1] // TK_F - 1

    def sq_norms_ub(rows):
        rf = rows.astype(F32)
        r2_up = (rf * rf * (1.0 + 2.0 ** -7)).astype(BF16)
        return _dot_nt(jnp.ones((8, HEAD_DIM), BF16), r2_up)[:1]

    @pl.when(qs == 0)
    def _():
        best = sq_norms_ub(k_ref[:KN_BLK, :])
        for r0 in range(KN_BLK, k_ref.shape[0], KN_BLK):
            best = jnp.maximum(best, sq_norms_ub(k_ref[r0:r0 + KN_BLK, :]))
        kn_ref[...] = jnp.broadcast_to(jnp.sqrt(jnp.max(best, axis=1, keepdims=True)), kn_ref.shape)

    def make_block(sub):
        qi = qs * N_SUB_F + sub
        rows = slice(sub * TQ_F, (sub + 1) * TQ_F)
        n_full = (qi * TQ_F) // TK_F
        q = q_ref[rows, :]
        q_aug = jnp.concatenate([q, extq_ref[rows, :]], axis=1)
        ub = jnp.sqrt(sq_norms_ub(q)) * kn_ref[:, :1] * NORM_SLACK + cq_ref[:, rows]

        def kb_of(i):
            return jnp.clip(n_full - i, 0, last_tile)

        def qk_t(i):
            ks = pl.multiple_of(kb_of(i) * TK_F, TK_F)
            k_aug = jnp.concatenate([k_ref[pl.ds(ks, TK_F), :], extk_ref[pl.ds(ks, TK_F), :]], axis=1)
            return _dot_nt(k_aug, q_aug)

        def pv_t(i, slot):
            ks = pl.multiple_of(kb_of(i) * TK_F, TK_F)
            return _dot(vt_ref[:, pl.ds(ks, TK_F)], p_ref[sub, slot])

        def prologue():
            st = qk_t(0)
            z_ref[sub, 1] = qk_t(1)
            key = n_full * TK_F + lax.broadcasted_iota(jnp.int32, (TK_F, TQ_F), 0)
            qry = qi * TQ_F + lax.broadcasted_iota(jnp.int32, (TK_F, TQ_F), 1)
            st = jnp.where(key <= qry, st, -jnp.inf)
            m0 = _col_reduce(st, jnp.maximum, jnp.max)
            p0 = jnp.exp2(st - m0)
            p_ref[sub, 0] = p0.astype(BF16)
            m_ref[sub] = m0
            l_ref[sub] = _col_reduce(p0, jnp.add, jnp.sum)
            a_ref[sub] = jnp.ones_like(m0)
            acc_ref[sub] = jnp.zeros(acc_ref.shape[1:], F32)
            gap = jnp.max(ub - m0)
            n_tiles = 1
            for i in range(clast_ref.shape[2]):
                alive = jnp.logical_and(i < n_full, gap - clast_ref[bi, h, i] > FOX_DEAD_LOG2)
                n_tiles = n_tiles + alive.astype(jnp.int32)
            return n_tiles

        def step(i, slot, last=False):
            if not last:
                z_ref[sub, 1 - slot] = qk_t(i + 1)
            acc_ref[sub] = a_ref[sub] * acc_ref[sub] + pv_t(i - 1, 1 - slot)
            m = m_ref[sub]
            m_new = jnp.maximum(m, _col_reduce(z_ref[sub, slot], jnp.maximum, jnp.max))
            alpha = jnp.exp2(m - m_new)
            p = jnp.exp2(z_ref[sub, slot] - m_new)
            p_ref[sub, slot] = p.astype(BF16)
            l_ref[sub] = alpha * l_ref[sub] + _col_reduce(p, jnp.add, jnp.sum)
            m_ref[sub] = m_new
            a_ref[sub] = alpha

        def walk(n_tiles):
            n_in = n_tiles - 1

            def body(j, _):
                i = 1 + 2 * j
                step(i, 1)
                step(i + 1, 0)
                return 0

            lax.fori_loop(0, n_in // 2, body, 0)

            @pl.when(n_in % 2 == 1)
            def _():
                step(n_in, 1, last=True)

        def epilogue(n_tiles):
            acc = a_ref[sub] * acc_ref[sub] + pv_t(n_tiles - 1, (n_tiles - 1) % 2)
            o_ref[rows, :] = _head_out(acc / l_ref[sub], g_ref[...]).astype(o_ref.dtype)

        return prologue, walk, epilogue

    blocks = [make_block(sub) for sub in range(N_SUB_F)]
    n_tiles = [prologue() for prologue, _, _ in blocks]
    for (_, walk, _), n in zip(blocks, n_tiles):
        walk(n)
    for (_, _, epilogue), n in zip(blocks, n_tiles):
        epilogue(n)


def _fox_attn(qk, v_t, c, ext_q, ext_k, g_col, w_cast):
    b, s, _ = qk.shape
    base = 2 * N_HEADS
    n_kt = s // TK_F
    tq_step = N_SUB_F * TQ_F
    nq = s // tq_step
    assert TK_F % TQ_F == 0 and s % tq_step == 0
    step_of = lambda bi, h, qi: (bi * N_HEADS + h) * nq + qi
    w_specs = [_cast_rows_spec(w, b * N_HEADS * nq, step_of) for w in w_cast]
    c_hs = jnp.transpose(c[:, :, :N_HEADS], (0, 2, 1))
    c_q = c_hs.reshape(b, N_HEADS, nq, 1, tq_step)
    c_last = c_hs.reshape(b, N_HEADS, n_kt, TK_F)[..., TK_F - 1]
    return pl.pallas_call(
        functools.partial(_fox_kernel, len(w_cast)),
        grid=(b, N_HEADS, nq),
        in_specs=[
            pl.BlockSpec((None, tq_step, HEAD_DIM), lambda bi, h, qi: (bi, qi, base + h)),
            pl.BlockSpec((None, s, HEAD_DIM), lambda bi, h, qi: (bi, 0, base + N_HEADS + h)),
            pl.BlockSpec((None, tq_step, HEAD_DIM), lambda bi, h, qi: (bi, qi, h)),
            pl.BlockSpec((None, s, HEAD_DIM), lambda bi, h, qi: (bi, 0, h)),
            pl.BlockSpec((None, HEAD_DIM, s), lambda bi, h, qi: (N_HEADS + h, 0, bi)),
            pl.BlockSpec((None, None, None, 1, tq_step), lambda bi, h, qi: (bi, h, qi, 0, 0)),
            pl.BlockSpec(memory_space=pltpu.SMEM),
            pl.BlockSpec((HEAD_DIM, 1), lambda bi, h, qi: (h, 0)),
            *w_specs,
        ],
        out_specs=[pl.BlockSpec((None, tq_step, HEAD_DIM), lambda bi, h, qi: (bi, qi, h)), *w_specs],
        scratch_shapes=[
            pltpu.VMEM((1, LANES), F32),
            pltpu.VMEM((N_SUB_F, 2, TK_F, TQ_F), F32),
            pltpu.VMEM((N_SUB_F, 2, TK_F, TQ_F), BF16),
            pltpu.VMEM((N_SUB_F, 1, TQ_F), F32),
            pltpu.VMEM((N_SUB_F, 1, TQ_F), F32),
            pltpu.VMEM((N_SUB_F, 1, TQ_F), F32),
            pltpu.VMEM((N_SUB_F, HEAD_DIM, TQ_F), F32),
        ],
        out_shape=[jax.ShapeDtypeStruct((b, s, GROUP_W), BF16),
                   *[jax.ShapeDtypeStruct(w.shape, BF16) for w in w_cast]],
        compiler_params=_params(("arbitrary", "arbitrary", "arbitrary")),
        name="fox_attn",
    )(qk, qk, ext_q, ext_k, v_t, c_q, c_last, g_col, *w_cast)


def _out_proj_kernel(ms_ref, mf_ref, w_ref, x_ref, g_ref, x1_ref, h2_ref):
    acc = _dot(jnp.concatenate([ms_ref[...], mf_ref[...]], axis=1), w_ref[...])
    x1 = x_ref[...] + acc
    x1_ref[...] = x1
    var = jnp.mean(x1 * x1, axis=-1, keepdims=True)
    h2_ref[...] = (x1 * lax.rsqrt(var + EPS) * g_ref[...]).astype(BF16)


def _out_proj(mixed_sb, mixed_fox, w_out, x2, g_mlp):
    m, d = x2.shape
    return pl.pallas_call(
        _out_proj_kernel,
        grid=(m // TM_OUT,),
        in_specs=[
            pl.BlockSpec((TM_OUT, GROUP_W), lambda i: (i, 0)),
            pl.BlockSpec((TM_OUT, GROUP_W), lambda i: (i, 0)),
            pl.BlockSpec((2 * GROUP_W, d), lambda i: (0, 0)),
            pl.BlockSpec((TM_OUT, d), lambda i: (i, 0)),
            pl.BlockSpec((1, d), lambda i: (0, 0)),
        ],
        out_specs=[
            pl.BlockSpec((TM_OUT, d), lambda i: (i, 0)),
            pl.BlockSpec((TM_OUT, d), lambda i: (i, 0)),
        ],
        out_shape=[
            jax.ShapeDtypeStruct((m, d), F32),
            jax.ShapeDtypeStruct((m, d), BF16),
        ],
        compiler_params=_params(("arbitrary",)),
        name="out_proj",
    )(mixed_sb, mixed_fox, w_out, x2, g_mlp)


def _mlp_kernel(h_ref, x1_ref, wu_ref, wd_ref, g_ref, o_ref):
    f = pl.program_id(1)

    @pl.when(f == 0)
    def _():
        o_ref[...] = x1_ref[...]

    for c0 in range(0, wu_ref.shape[1], MLP_CHUNK):
        c1 = c0 + MLP_CHUNK
        u = jnp.maximum(_dot(h_ref[...], wu_ref[:, c0:c1]), 0.0)
        o_ref[...] += _dot((u * u).astype(BF16), wd_ref[c0:c1, :])

    @pl.when(f == pl.num_programs(1) - 1)
    def _():
        x2 = o_ref[...]
        var = jnp.mean(x2 * x2, axis=-1, keepdims=True)
        o_ref[...] = x2 * lax.rsqrt(var + EPS) * g_ref[...]


def _mlp(h2, x1, w_up, w_down, g_final):
    m, d = x1.shape
    dff = w_up.shape[1]
    return pl.pallas_call(
        _mlp_kernel,
        grid=(m // TM_MLP, dff // TF_MLP),
        in_specs=[
            pl.BlockSpec((TM_MLP, d), lambda i, f: (i, 0)),
            pl.BlockSpec((TM_MLP, d), lambda i, f: (i, 0)),
            pl.BlockSpec((d, TF_MLP), lambda i, f: (0, f)),
            pl.BlockSpec((TF_MLP, d), lambda i, f: (f, 0)),
            pl.BlockSpec((1, d), lambda i, f: (0, 0)),
        ],
        out_specs=pl.BlockSpec((TM_MLP, d), lambda i, f: (i, 0)),
        out_shape=jax.ShapeDtypeStruct((m, d), F32),
        compiler_params=_params(("arbitrary", "arbitrary"), VMEM_LIMIT_MLP),
        name="mlp",
    )(h2, x1, w_up, w_down, g_final)


def kernel(x, g_attn, w_in, b_f, g_out_sb, g_out_fox, w_out, g_mlp, w_up, w_down, g_final):
    b, s, d = x.shape
    n_qkv = 6 * GROUP_W
    assert s % TQ_S == 0 and s % TK_F == 0 and (b * s) % TM_IN == 0
    x2 = x.reshape(b * s, d)

    idx = jnp.arange(TK_S)
    u_excl = (idx[None, :] > idx[:, None]).astype(BF16)
    u2 = jnp.concatenate([u_excl, u_excl], axis=1)
    cidx = jnp.arange(CS_BLK)
    tri_incl = (cidx[None, :] <= cidx[:, None]).astype(BF16)

    gw = GROUP_W
    for l in range(g_attn.shape[0]):
        w_t = jnp.swapaxes(w_in[l], 0, 1)
        w_f_t = jnp.pad(w_t[n_qkv:], ((0, LANES - N_HEADS), (0, 0))).astype(BF16)
        b_pad = jnp.pad(b_f[l], (0, LANES - N_HEADS)).reshape(1, LANES)

        qk, v_t, f_logit = _in_proj(x2, g_attn[l].reshape(1, d), w_t, w_f_t)
        qk = qk.reshape(b, s, N_QK_GROUPS * gw)
        v_t = v_t.reshape(2 * N_HEADS, HEAD_DIM, b * s)
        c, ext_q, ext_k = _forget_cs(f_logit.reshape(b, s, LANES), b_pad, tri_incl)

        mixed_sb, w_up_b, w_down_b = _sb_attn(qk, v_t, g_out_sb[l].reshape(gw, 1), u2,
                                              (w_up[l], w_down[l]))
        mixed_fox, w_out_b = _fox_attn(qk, v_t, c, ext_q, ext_k, g_out_fox[l].reshape(gw, 1), (w_out[l],))

        x1, h2 = _out_proj(mixed_sb.reshape(b * s, GROUP_W), mixed_fox.reshape(b * s, GROUP_W),
                           w_out_b, x2, g_mlp[l].reshape(1, d))
        assert g_attn.shape[0] == 1
        x2 = _mlp(h2, x1, w_up_b, w_down_b, g_final.reshape(1, d))
    return x2.reshape(b, s, d)
```

```python
import functools

import jax
import jax.numpy as jnp
from jax import lax
from jax.experimental import pallas as pl
from jax.experimental.pallas import tpu as pltpu

F32 = jnp.float32
BF16 = jnp.bfloat16

HEAD_DIM = 128
N_HEADS = 8
GROUP_W = N_HEADS * HEAD_DIM
EPS = 1e-6
SCALE = HEAD_DIM ** -0.5
LOG2E = 1.4426950408889634
SB_DEAD_LOG2 = -152.0
FOX_DEAD_LOG2 = -152.0
NORM_SLACK = 1.01
M_INIT = -1e30
KN_BLK = 512
LANES = 128

VMEM_LIMIT = 56 * 1024 * 1024
VMEM_LIMIT_MLP = 58 * 1024 * 1024

TM_IN = 1024
TN_IN = 1024
IN_CHUNK = 256
W_SLOTS = 3
TQ_S = 2048
TK_S = 256
TQ_F = 512
TK_F = 512
N_SUB_F = 8
CS_BLK = 256
CS_CHUNK = 1024
TM_OUT = 512
TM_MLP = 512
TF_MLP = 2048
MLP_CHUNK = 1024


def _params(sem, vmem_limit=VMEM_LIMIT):
    return pltpu.CompilerParams(dimension_semantics=sem, vmem_limit_bytes=vmem_limit)


def _dot(a, b):
    return jnp.dot(a, b, preferred_element_type=F32)


def _dot_nt(a, b):
    return lax.dot_general(a, b, (((1,), (1,)), ((), ())), preferred_element_type=F32)


def _log_sigmoid(x):
    return jnp.minimum(x, 0.0) - jnp.log(1.0 + jnp.exp(-jnp.abs(x)))


N_QK_GROUPS = 4


def _w_group(j):
    return jnp.where(j < 2, j, jnp.where(j < 4, j + 1, jnp.where(j == 4, 2, 5)))


def _w_copy(w_hbm, wbuf, sem, t, n_j):
    slot = t % W_SLOTS
    row0 = pl.multiple_of(_w_group(t % n_j) * TN_IN, TN_IN)
    return pltpu.make_async_copy(w_hbm.at[pl.ds(row0, TN_IN), :], wbuf.at[slot], sem.at[slot])


def _in_proj_kernel(x_ref, g_ref, w_hbm, wf_ref, qk_ref, vt_ref, f_ref, h_ref, wbuf, sem):
    j = pl.program_id(1)
    n_j = pl.num_programs(1)
    t = pl.program_id(0) * n_j + j
    total = pl.num_programs(0) * n_j

    @pl.when(t == 0)
    def _():
        for ahead in range(W_SLOTS - 1):
            _w_copy(w_hbm, wbuf, sem, ahead, n_j).start()

    @pl.when(t + (W_SLOTS - 1) < total)
    def _():
        _w_copy(w_hbm, wbuf, sem, t + (W_SLOTS - 1), n_j).start()

    _w_copy(w_hbm, wbuf, sem, t, n_j).wait()
    w_ref = wbuf.at[t % W_SLOTS]

    @pl.when(j == 0)
    def _():
        x = x_ref[...]
        var = jnp.mean(x * x, axis=-1, keepdims=True)
        h = (x * lax.rsqrt(var + EPS) * g_ref[...]).astype(BF16)
        h_ref[...] = h
        f_ref[...] = _dot_nt(h, wf_ref[...])

    @pl.when(j < N_QK_GROUPS)
    def _():
        scale = jnp.where((j == 0) | (j == 2), SCALE * LOG2E, 1.0).astype(F32)
        for c0 in range(0, TN_IN, IN_CHUNK):
            c1 = c0 + IN_CHUNK
            w = w_ref[c0:c1, :].astype(BF16)
            qk_ref[:, c0:c1] = (_dot_nt(h_ref[...], w) * scale).astype(BF16)

    @pl.when(j >= N_QK_GROUPS)
    def _():
        for c0 in range(0, TN_IN, IN_CHUNK):
            c1 = c0 + IN_CHUNK
            w = w_ref[c0:c1, :].astype(BF16)
            vt_ref[c0:c1, :] = _dot_nt(w, h_ref[...]).astype(BF16)


def _in_proj(x2, g, w_qkv_t, w_f_t):
    m, d = x2.shape
    n = 6 * GROUP_W
    assert TN_IN == GROUP_W and w_qkv_t.shape[0] >= n
    return pl.pallas_call(
        _in_proj_kernel,
        grid=(m // TM_IN, n // TN_IN),
        in_specs=[
            pl.BlockSpec((TM_IN, d), lambda i, j: (i, 0)),
            pl.BlockSpec((1, d), lambda i, j: (0, 0)),
            pl.BlockSpec(memory_space=pl.ANY),
            pl.BlockSpec((LANES, d), lambda i, j: (0, 0)),
        ],
        out_specs=[
            pl.BlockSpec((TM_IN, TN_IN), lambda i, j: (i, jnp.minimum(j, N_QK_GROUPS - 1))),
            pl.BlockSpec((TN_IN, TM_IN), lambda i, j: (jnp.maximum(j - N_QK_GROUPS, 0), i)),
            pl.BlockSpec((TM_IN, LANES), lambda i, j: (i, 0)),
        ],
        out_shape=[
            jax.ShapeDtypeStruct((m, N_QK_GROUPS * GROUP_W), BF16),
            jax.ShapeDtypeStruct((2 * GROUP_W, m), BF16),
            jax.ShapeDtypeStruct((m, LANES), F32),
        ],
        scratch_shapes=[
            pltpu.VMEM((TM_IN, d), BF16),
            pltpu.VMEM((W_SLOTS, TN_IN, d), F32),
            pltpu.SemaphoreType.DMA((W_SLOTS,)),
        ],
        compiler_params=_params(("arbitrary", "arbitrary")),
        name="in_proj",
    )(x2, g, w_qkv_t, w_f_t)


def _split3(x):
    p1 = x.astype(BF16)
    r1 = x - p1.astype(F32)
    p2 = r1.astype(BF16)
    p3 = (r1 - p2.astype(F32)).astype(BF16)
    return p1, p2, p3


def _forget_cs_kernel(f_ref, b_ref, tri_ref, selq_ref, selk_ref, oneq_ref, onek_ref,
                      c_ref, extq_ref, extk_ref, carry_ref):
    n_blk = f_ref.shape[0] // CS_BLK
    tri = tri_ref[...]

    @pl.when(pl.program_id(1) == 0)
    def _():
        carry_ref[...] = jnp.zeros_like(carry_ref)

    def body(i, carry):
        r0 = pl.multiple_of(i * CS_BLK, CS_BLK)
        lf = _log_sigmoid(f_ref[pl.ds(r0, CS_BLK), :] + b_ref[...])
        p1, p2, p3 = _split3(lf)
        c = _dot(tri, p1) + _dot(tri, p2) + _dot(tri, p3) + carry
        c2 = c * LOG2E
        c_ref[pl.ds(r0, CS_BLK), :] = c2
        lane = lax.broadcasted_iota(jnp.int32, c2.shape, 1)
        packed = jnp.zeros_like(c2)
        for k, piece in enumerate(_split3(c2)):
            piece = jnp.where(lane < N_HEADS, piece.astype(F32), 0.0)
            packed = packed + (piece if k == 0 else pltpu.roll(piece, N_HEADS * k, axis=1))
        packed = packed.astype(BF16)
        extq_ref[pl.ds(r0, CS_BLK), :] = (_dot(packed, selq_ref[...]) + oneq_ref[...]).astype(BF16)
        extk_ref[pl.ds(r0, CS_BLK), :] = (_dot(packed, selk_ref[...]) + onek_ref[...]).astype(BF16)
        return c[CS_BLK - 1:CS_BLK, :]

    carry_ref[...] = lax.fori_loop(0, n_blk, body, carry_ref[...])


def _aug_constants():
    row = jnp.arange(LANES)
    col = jnp.arange(GROUP_W)
    piece, head = row // N_HEADS, row % N_HEADS
    blk, lane = col // HEAD_DIM, col % HEAD_DIM
    same = (head[:, None] == blk[None, :]) & (piece[:, None] < 3)
    selq = jnp.where(same & (lane[None, :] == 3 + piece[:, None]), 1.0, 0.0).astype(BF16)
    selk = jnp.where(same & (lane[None, :] == piece[:, None]), -1.0, 0.0).astype(BF16)
    oneq = jnp.where(lane < 3, 1.0, 0.0).astype(F32).reshape(1, GROUP_W)
    onek = jnp.where((lane >= 3) & (lane < 6), 1.0, 0.0).astype(F32).reshape(1, GROUP_W)
    return selq, selk, oneq, onek


def _forget_cs(f_logit, b_pad, tri_incl):
    b, s, _ = f_logit.shape
    assert s % CS_CHUNK == 0 and CS_CHUNK % CS_BLK == 0
    const = lambda shape: pl.BlockSpec(shape, lambda i, j: (0, 0))
    rows = lambda width: pl.BlockSpec((None, CS_CHUNK, width), lambda i, j: (i, j, 0))
    return pl.pallas_call(
        _forget_cs_kernel,
        grid=(b, s // CS_CHUNK),
        in_specs=[
            rows(LANES),
            const((1, LANES)),
            const((CS_BLK, CS_BLK)),
            const((LANES, GROUP_W)),
            const((LANES, GROUP_W)),
            const((1, GROUP_W)),
            const((1, GROUP_W)),
        ],
        out_specs=[rows(LANES), rows(GROUP_W), rows(GROUP_W)],
        out_shape=[
            jax.ShapeDtypeStruct((b, s, LANES), F32),
            jax.ShapeDtypeStruct((b, s, GROUP_W), BF16),
            jax.ShapeDtypeStruct((b, s, GROUP_W), BF16),
        ],
        scratch_shapes=[pltpu.VMEM((1, LANES), F32)],
        compiler_params=_params(("arbitrary", "arbitrary")),
        name="forget_cs",
    )(f_logit, b_pad, tri_incl, *_aug_constants())


def _col_reduce(x, pair, full):
    while x.shape[0] > 8:
        half = x.shape[0] // 2
        x = pair(x[:half], x[half:])
    return full(x, axis=0, keepdims=True)


def _head_out(acc_t, g_col):
    o = acc_t * lax.rsqrt(jnp.mean(acc_t * acc_t, axis=0, keepdims=True) + EPS) * g_col
    return jnp.transpose(o)


def _store_head_out(o_ref, acc_t, g_col):
    o_ref[...] = _head_out(acc_t, g_col).astype(o_ref.dtype)


def _cast_rows_spec(w, n_steps, step_of):
    rows, cols = w.shape
    assert rows % n_steps == 0 and (rows // n_steps) % 16 == 0
    return pl.BlockSpec((rows // n_steps, cols), lambda bi, h, qi: (step_of(bi, h, qi), 0))


def _cast_side_job(w_refs, wb_refs):
    for w_ref, wb_ref in zip(w_refs, wb_refs):
        wb_ref[...] = w_ref[...].astype(BF16)


def _sb_kernel(n_cast, q_ref, k_ref, vt_ref, g_ref, u2_ref, *rest):
    o_ref = rest[n_cast]
    _cast_side_job(rest[:n_cast], rest[n_cast + 1:])

    qi = pl.program_id(2)
    u2 = u2_ref[...]
    n_sub = TQ_S // TK_S
    key = lax.broadcasted_iota(jnp.int32, (TK_S, TK_S), 0)
    qry = lax.broadcasted_iota(jnp.int32, (TK_S, TK_S), 1)
    causal = key < qry

    def stage_q(q, kb):
        ks = pl.multiple_of(kb * TK_S, TK_S)
        return _dot_nt(k_ref[pl.ds(ks, TK_S), :], q)

    def stage_e(z, diagonal):
        zn = jnp.minimum(z, 0.0)
        zp = zn - z
        sp = jnp.log2(1.0 + jnp.exp2(zn + zp))
        lb = zn - sp
        lk = zp - sp
        if diagonal:
            lk = jnp.where(causal, lk, 0.0)
        hi = lk.astype(BF16)
        lo = (lk - hi.astype(F32)).astype(BF16)
        return lb, jnp.concatenate([hi, lo], axis=0), lk[:1, :]

    def stage_c(hilo, lk_first):
        rest = _dot(u2, hilo)
        return rest, rest[:1, :] + lk_first

    def stage_x(lb, rest, carry, diagonal):
        if diagonal:
            a = jnp.where(causal, jnp.exp2(lb + rest), 0.0)
        else:
            a = jnp.exp2(lb + (rest + carry))
        return a.astype(BF16)

    def stage_v(kb, a, gate=None):
        vt = vt_ref[:, pl.ds(pl.multiple_of(kb * TK_S, TK_S), TK_S)]
        if gate is not None:
            vt = vt * gate
        return _dot(vt, a)

    items = []
    for sub in range(n_sub):
        kd = qi * n_sub + sub
        q = q_ref[sub * TK_S:(sub + 1) * TK_S, :]
        items.append(dict(q=q, kb=kd, diagonal=True, gate=None))
        items.append(dict(q=q, kb=jnp.maximum(kd - 1, 0), diagonal=False,
                          gate=jnp.where(kd > 0, 1.0, 0.0).astype(BF16)))
    n_items = len(items)
    for t in range(n_items + 4):
        if 4 <= t:
            it = items[t - 4]
            it["pv"] = stage_v(it["kb"], it["a"], it["gate"])
        if 2 <= t < n_items + 2:
            it = items[t - 2]
            it["rest"], it["colsum"] = stage_c(it["hilo"], it["lk0"])
        if t < n_items:
            it = items[t]
            it["z"] = stage_q(it["q"], it["kb"])
        if 3 <= t < n_items + 3:
            it = items[t - 3]
            carry_in = None if it["diagonal"] else items[t - 4]["colsum"]
            it["a"] = stage_x(it["lb"], it["rest"], carry_in, it["diagonal"])
        if 1 <= t < n_items + 1:
            it = items[t - 1]
            it["lb"], it["hilo"], it["lk0"] = stage_e(it["z"], it["diagonal"])

    qs = [items[2 * r]["q"] for r in range(n_sub)]
    state = []
    for r in range(n_sub):
        d, p = items[2 * r], items[2 * r + 1]
        state += [d["colsum"] + p["colsum"], d["pv"] + p["pv"]]

    kd_last = qi * n_sub + n_sub - 1

    def cond(st):
        worst = st[1]
        for r in range(1, n_sub):
            worst = jnp.maximum(worst, st[1 + 2 * r])
        return jnp.logical_and(kd_last - 2 - st[0] >= 0, jnp.max(worst) > SB_DEAD_LOG2)

    def body(st):
        j, out = st[0], []
        for r in range(n_sub):
            kb = qi * n_sub + r - 2 - j
            gate = jnp.where(kb >= 0, 1.0, 0.0).astype(BF16)
            kb = jnp.maximum(kb, 0)
            lb, hilo, lk0 = stage_e(stage_q(qs[r], kb), False)
            rest, colsum = stage_c(hilo, lk0)
            a = stage_x(lb, rest, st[1 + 2 * r], False)
            out += [st[1 + 2 * r] + colsum, st[2 + 2 * r] + stage_v(kb, a, gate)]
        return (j + 1, *out)

    st = lax.while_loop(cond, body, (0, *state))
    _store_head_out(o_ref, jnp.concatenate([st[2 + 2 * r] for r in range(n_sub)], axis=1), g_ref[...])


def _sb_attn(qk, v_t, g_col, u2, w_cast):
    b, s, _ = qk.shape
    assert TQ_S % TK_S == 0
    nq = s // TQ_S
    w_specs = [_cast_rows_spec(w, b * N_HEADS * nq, lambda bi, h, qi: (bi * N_HEADS + h) * nq + qi)
               for w in w_cast]
    return pl.pallas_call(
        functools.partial(_sb_kernel, len(w_cast)),
        grid=(b, N_HEADS, nq),
        in_specs=[
            pl.BlockSpec((None, TQ_S, HEAD_DIM), lambda bi, h, qi: (bi, qi, h)),
            pl.BlockSpec((None, s, HEAD_DIM), lambda bi, h, qi: (bi, 0, N_HEADS + h)),
            pl.BlockSpec((None, HEAD_DIM, s), lambda bi, h, qi: (h, 0, bi)),
            pl.BlockSpec((HEAD_DIM, 1), lambda bi, h, qi: (h, 0)),
            pl.BlockSpec((TK_S, 2 * TK_S), lambda bi, h, qi: (0, 0)),
            *w_specs,
        ],
        out_specs=[pl.BlockSpec((None, TQ_S, HEAD_DIM), lambda bi, h, qi: (bi, qi, h)), *w_specs],
        out_shape=[jax.ShapeDtypeStruct((b, s, GROUP_W), BF16),
                   *[jax.ShapeDtypeStruct(w.shape, BF16) for w in w_cast]],
        compiler_params=_params(("arbitrary", "arbitrary", "arbitrary")),
        name="sb_attn",
    )(qk, qk, v_t, g_col, u2, *w_cast)


def _fox_kernel(n_cast, q_ref, k_ref, extq_ref, extk_ref, vt_ref, cq_ref, clast_ref, g_ref, *rest):
    o_ref = rest[n_cast]
    _cast_side_job(rest[:n_cast], rest[n_cast + 1:2 * n_cast + 1])
    kn_ref, z_ref, p_ref, m_ref, l_ref, a_ref, acc_ref = rest[2 * n_cast + 1:]

    bi, h, qs = pl.program_id(0), pl.program_id(1), pl.program_id(2)
    last_tile = vt_ref.shape[1] // TK_F - 1

    def sq_norms_ub(rows):
        rf = rows.astype(F32)
        r2_up = (rf * rf * (1.0 + 2.0 ** -7)).astype(BF16)
        return _dot_nt(jnp.ones((8, HEAD_DIM), BF16), r2_up)[:1]

    @pl.when(qs == 0)
    def _():
        best = sq_norms_ub(k_ref[:KN_BLK, :])
        for r0 in range(KN_BLK, k_ref.shape[0], KN_BLK):
            best = jnp.maximum(best, sq_norms_ub(k_ref[r0:r0 + KN_BLK, :]))
        kn_ref[...] = jnp.broadcast_to(jnp.sqrt(jnp.max(best, axis=1, keepdims=True)), kn_ref.shape)

    def make_block(sub):
        qi = qs * N_SUB_F + sub
        rows = slice(sub * TQ_F, (sub + 1) * TQ_F)
        n_full = (qi * TQ_F) // TK_F
        q = q_ref[rows, :]
        q_aug = jnp.concatenate([q, extq_ref[rows, :]], axis=1)
        ub = jnp.sqrt(sq_norms_ub(q)) * kn_ref[:, :1] * NORM_SLACK + cq_ref[:, rows]

        def kb_of(i):
            return jnp.clip(n_full - i, 0, last_tile)

        def qk_t(i):
            ks = pl.multiple_of(kb_of(i) * TK_F, TK_F)
            k_aug = jnp.concatenate([k_ref[pl.ds(ks, TK_F), :], extk_ref[pl.ds(ks, TK_F), :]], axis=1)
            return _dot_nt(k_aug, q_aug)

        def pv_t(i, slot):
            ks = pl.multiple_of(kb_of(i) * TK_F, TK_F)
            return _dot(vt_ref[:, pl.ds(ks, TK_F)], p_ref[sub, slot])

        def prologue():
            st = qk_t(0)
            z_ref[sub, 1] = qk_t(1)
            key = n_full * TK_F + lax.broadcasted_iota(jnp.int32, (TK_F, TQ_F), 0)
            qry = qi * TQ_F + lax.broadcasted_iota(jnp.int32, (TK_F, TQ_F), 1)
            st = jnp.where(key <= qry, st, -jnp.inf)
            m0 = _col_reduce(st, jnp.maximum, jnp.max)
            p0 = jnp.exp2(st - m0)
            p_ref[sub, 0] = p0.astype(BF16)
            m_ref[sub] = m0
            l_ref[sub] = _col_reduce(p0, jnp.add, jnp.sum)
            a_ref[sub] = jnp.ones_like(m0)
            acc_ref[sub] = jnp.zeros(acc_ref.shape[1:], F32)
            gap = jnp.max(ub - m0)
            n_tiles = 1
            for i in range(clast_ref.shape[2]):
                alive = jnp.logical_and(i < n_full, gap - clast_ref[bi, h, i] > FOX_DEAD_LOG2)
                n_tiles = n_tiles + alive.astype(jnp.int32)
            return n_tiles

        def step(i, slot, last=False):
            if not last:
                z_ref[sub, 1 - slot] = qk_t(i + 1)
            acc_ref[sub] = a_ref[sub] * acc_ref[sub] + pv_t(i - 1, 1 - slot)
            m = m_ref[sub]
            m_new = jnp.maximum(m, _col_reduce(z_ref[sub, slot], jnp.maximum, jnp.max))
            alpha = jnp.exp2(m - m_new)
            p = jnp.exp2(z_ref[sub, slot] - m_new)
            p_ref[sub, slot] = p.astype(BF16)
            l_ref[sub] = alpha * l_ref[sub] + _col_reduce(p, jnp.add, jnp.sum)
            m_ref[sub] = m_new
            a_ref[sub] = alpha

        def walk(n_tiles):
            n_in = n_tiles - 1

            def body(j, _):
                i = 1 + 2 * j
                step(i, 1)
                step(i + 1, 0)
                return 0

            lax.fori_loop(0, n_in // 2, body, 0)

            @pl.when(n_in % 2 == 1)
            def _():
                step(n_in, 1, last=True)

        def epilogue(n_tiles):
            acc = a_ref[sub] * acc_ref[sub] + pv_t(n_tiles - 1, (n_tiles - 1) % 2)
            o_ref[rows, :] = _head_out(acc / l_ref[sub], g_ref[...]).astype(o_ref.dtype)

        return prologue, walk, epilogue

    blocks = [make_block(sub) for sub in range(N_SUB_F)]
    n_tiles = [prologue() for prologue, _, _ in blocks]
    for (_, walk, _), n in zip(blocks, n_tiles):
        walk(n)
    for (_, _, epilogue), n in zip(blocks, n_tiles):
        epilogue(n)


def _fox_attn(qk, v_t, c, ext_q, ext_k, g_col, w_cast):
    b, s, _ = qk.shape
    base = 2 * N_HEADS
    n_kt = s // TK_F
    tq_step = N_SUB_F * TQ_F
    nq = s // tq_step
    assert TK_F % TQ_F == 0 and s % tq_step == 0
    step_of = lambda bi, h, qi: (bi * N_HEADS + h) * nq + qi
    w_specs = [_cast_rows_spec(w, b * N_HEADS * nq, step_of) for w in w_cast]
    c_hs = jnp.transpose(c[:, :, :N_HEADS], (0, 2, 1))
    c_q = c_hs.reshape(b, N_HEADS, nq, 1, tq_step)
    c_last = c_hs.reshape(b, N_HEADS, n_kt, TK_F)[..., TK_F - 1]
    return pl.pallas_call(
        functools.partial(_fox_kernel, len(w_cast)),
        grid=(b, N_HEADS, nq),
        in_specs=[
            pl.BlockSpec((None, tq_step, HEAD_DIM), lambda bi, h, qi: (bi, qi, base + h)),
            pl.BlockSpec((None, s, HEAD_DIM), lambda bi, h, qi: (bi, 0, base + N_HEADS + h)),
            pl.BlockSpec((None, tq_step, HEAD_DIM), lambda bi, h, qi: (bi, qi, h)),
            pl.BlockSpec((None, s, HEAD_DIM), lambda bi, h, qi: (bi, 0, h)),
            pl.BlockSpec((None, HEAD_DIM, s), lambda bi, h, qi: (N_HEADS + h, 0, bi)),
            pl.BlockSpec((None, None, None, 1, tq_step), lambda bi, h, qi: (bi, h, qi, 0, 0)),
            pl.BlockSpec(memory_space=pltpu.SMEM),
            pl.BlockSpec((HEAD_DIM, 1), lambda bi, h, qi: (h, 0)),
            *w_specs,
        ],
        out_specs=[pl.BlockSpec((None, tq_step, HEAD_DIM), lambda bi, h, qi: (bi, qi, h)), *w_specs],
        scratch_shapes=[
            pltpu.VMEM((1, LANES), F32),
            pltpu.VMEM((N_SUB_F, 2, TK_F, TQ_F), F32),
            pltpu.VMEM((N_SUB_F, 2, TK_F, TQ_F), BF16),
            pltpu.VMEM((N_SUB_F, 1, TQ_F), F32),
            pltpu.VMEM((N_SUB_F, 1, TQ_F), F32),
            pltpu.VMEM((N_SUB_F, 1, TQ_F), F32),
            pltpu.VMEM((N_SUB_F, HEAD_DIM, TQ_F), F32),
        ],
        out_shape=[jax.ShapeDtypeStruct((b, s, GROUP_W), BF16),
                   *[jax.ShapeDtypeStruct(w.shape, BF16) for w in w_cast]],
        compiler_params=_params(("arbitrary", "arbitrary", "arbitrary")),
        name="fox_attn",
    )(qk, qk, ext_q, ext_k, v_t, c_q, c_last, g_col, *w_cast)


def _out_proj_kernel(ms_ref, mf_ref, w_ref, x_ref, g_ref, x1_ref, h2_ref):
    acc = _dot(jnp.concatenate([ms_ref[...], mf_ref[...]], axis=1), w_ref[...])
    x1 = x_ref[...] + acc
    x1_ref[...] = x1
    var = jnp.mean(x1 * x1, axis=-1, keepdims=True)
    h2_ref[...] = (x1 * lax.rsqrt(var + EPS) * g_ref[...]).astype(BF16)


def _out_proj(mixed_sb, mixed_fox, w_out, x2, g_mlp):
    m, d = x2.shape
    return pl.pallas_call(
        _out_proj_kernel,
        grid=(m // TM_OUT,),
        in_specs=[
            pl.BlockSpec((TM_OUT, GROUP_W), lambda i: (i, 0)),
            pl.BlockSpec((TM_OUT, GROUP_W), lambda i: (i, 0)),
            pl.BlockSpec((2 * GROUP_W, d), lambda i: (0, 0)),
            pl.BlockSpec((TM_OUT, d), lambda i: (i, 0)),
            pl.BlockSpec((1, d), lambda i: (0, 0)),
        ],
        out_specs=[
            pl.BlockSpec((TM_OUT, d), lambda i: (i, 0)),
            pl.BlockSpec((TM_OUT, d), lambda i: (i, 0)),
        ],
        out_shape=[
            jax.ShapeDtypeStruct((m, d), F32),
            jax.ShapeDtypeStruct((m, d), BF16),
        ],
        compiler_params=_params(("arbitrary",)),
        name="out_proj",
    )(mixed_sb, mixed_fox, w_out, x2, g_mlp)


def _mlp_kernel(h_ref, x1_ref, wu_ref, wd_ref, g_ref, o_ref):
    f = pl.program_id(1)

    @pl.when(f == 0)
    def _():
        o_ref[...] = x1_ref[...]

    for c0 in range(0, wu_ref.shape[1], MLP_CHUNK):
        c1 = c0 + MLP_CHUNK
        u = jnp.maximum(_dot(h_ref[...], wu_ref[:, c0:c1]), 0.0)
        o_ref[...] += _dot((u * u).astype(BF16), wd_ref[c0:c1, :])

    @pl.when(f == pl.num_programs(1) - 1)
    def _():
        x2 = o_ref[...]
        var = jnp.mean(x2 * x2, axis=-1, keepdims=True)
        o_ref[...] = x2 * lax.rsqrt(var + EPS) * g_ref[...]


def _mlp(h2, x1, w_up, w_down, g_final):
    m, d = x1.shape
    dff = w_up.shape[1]
    return pl.pallas_call(
        _mlp_kernel,
        grid=(m // TM_MLP, dff // TF_MLP),
        in_specs=[
            pl.BlockSpec((TM_MLP, d), lambda i, f: (i, 0)),
            pl.BlockSpec((TM_MLP, d), lambda i, f: (i, 0)),
            pl.BlockSpec((d, TF_MLP), lambda i, f: (0, f)),
            pl.BlockSpec((TF_MLP, d), lambda i, f: (f, 0)),
            pl.BlockSpec((1, d), lambda i, f: (0, 0)),
        ],
        out_specs=pl.BlockSpec((TM_MLP, d), lambda i, f: (i, 0)),
        out_shape=jax.ShapeDtypeStruct((m, d), F32),
        compiler_params=_params(("arbitrary", "arbitrary"), VMEM_LIMIT_MLP),
        name="mlp",
    )(h2, x1, w_up, w_down, g_final)


def kernel(x, g_attn, w_in, b_f, g_out_sb, g_out_fox, w_out, g_mlp, w_up, w_down, g_final):
    b, s, d = x.shape
    n_qkv = 6 * GROUP_W
    assert s % TQ_S == 0 and s % TK_F == 0 and (b * s) % TM_IN == 0
    x2 = x.reshape(b * s, d)

    idx = jnp.arange(TK_S)
    u_excl = (idx[None, :] > idx[:, None]).astype(BF16)
    u2 = jnp.concatenate([u_excl, u_excl], axis=1)
    cidx = jnp.arange(CS_BLK)
    tri_incl = (cidx[None, :] <= cidx[:, None]).astype(BF16)

    gw = GROUP_W
    for l in range(g_attn.shape[0]):
        w_t = jnp.swapaxes(w_in[l], 0, 1)
        w_f_t = jnp.pad(w_t[n_qkv:], ((0, LANES - N_HEADS), (0, 0))).astype(BF16)
        b_pad = jnp.pad(b_f[l], (0, LANES - N_HEADS)).reshape(1, LANES)

        qk, v_t, f_logit = _in_proj(x2, g_attn[l].reshape(1, d), w_t, w_f_t)
        qk = qk.reshape(b, s, N_QK_GROUPS * gw)
        v_t = v_t.reshape(2 * N_HEADS, HEAD_DIM, b * s)
        c, ext_q, ext_k = _forget_cs(f_logit.reshape(b, s, LANES), b_pad, tri_incl)

        mixed_sb, w_up_b, w_down_b = _sb_attn(qk, v_t, g_out_sb[l].reshape(gw, 1), u2,
                                              (w_up[l], w_down[l]))
        mixed_fox, w_out_b = _fox_attn(qk, v_t, c, ext_q, ext_k, g_out_fox[l].reshape(gw, 1), (w_out[l],))

        x1, h2 = _out_proj(mixed_sb.reshape(b * s, GROUP_W), mixed_fox.reshape(b * s, GROUP_W),
                           w_out_b, x2, g_mlp[l].reshape(1, d))
        assert g_attn.shape[0] == 1
        x2 = _mlp(h2, x1, w_up_b, w_down_b, g_final.reshape(1, d))
    return x2.reshape(b, s, d)
```

```python
import functools

import jax
import jax.numpy as jnp
from jax import lax
from jax.experimental import pallas as pl
from jax.experimental.pallas import tpu as pltpu

F32 = jnp.float32
BF16 = jnp.bfloat16

HEAD_DIM = 128
N_HEADS = 8
GROUP_W = N_HEADS * HEAD_DIM
EPS = 1e-6
SCALE = HEAD_DIM ** -0.5
LOG2E = 1.4426950408889634
SB_DEAD_LOG2 = -152.0
FOX_DEAD_LOG2 = -152.0
NORM_SLACK = 1.01
KN_BLK = 512
LANES = 128

VMEM_LIMIT = 56 * 1024 * 1024
VMEM_LIMIT_MLP = 58 * 1024 * 1024

TM_IN = 1024
TN_IN = 1024
IN_CHUNK = 256
W_SLOTS = 3
TQ_S = 2048
TK_S = 256
TQ_F = 512
TK_F = 512
N_SUB_F = 8
CS_BLK = 256
CS_CHUNK = 1024
TM_OUT = 512
TM_MLP = 512
TF_MLP = 2048
MLP_CHUNK = 1024


def _params(sem, vmem_limit=VMEM_LIMIT):
    return pltpu.CompilerParams(dimension_semantics=sem, vmem_limit_bytes=vmem_limit)


def _dot(a, b):
    return jnp.dot(a, b, preferred_element_type=F32)


def _dot_nt(a, b):
    return lax.dot_general(a, b, (((1,), (1,)), ((), ())), preferred_element_type=F32)


def _log_sigmoid(x):
    return jnp.minimum(x, 0.0) - jnp.log(1.0 + jnp.exp(-jnp.abs(x)))


N_QK_GROUPS = 4


def _w_group(j):
    return jnp.where(j < 2, j, jnp.where(j < 4, j + 1, jnp.where(j == 4, 2, 5)))


def _w_copy(w_hbm, wbuf, sem, t, n_j):
    slot = t % W_SLOTS
    row0 = pl.multiple_of(_w_group(t % n_j) * TN_IN, TN_IN)
    return pltpu.make_async_copy(w_hbm.at[pl.ds(row0, TN_IN), :], wbuf.at[slot], sem.at[slot])


def _in_proj_kernel(x_ref, g_ref, w_hbm, wf_ref, qk_ref, vt_ref, f_ref, h_ref, wbuf, sem):
    j = pl.program_id(1)
    n_j = pl.num_programs(1)
    t = pl.program_id(0) * n_j + j
    total = pl.num_programs(0) * n_j

    @pl.when(t == 0)
    def _():
        for ahead in range(W_SLOTS - 1):
            _w_copy(w_hbm, wbuf, sem, ahead, n_j).start()

    @pl.when(t + (W_SLOTS - 1) < total)
    def _():
        _w_copy(w_hbm, wbuf, sem, t + (W_SLOTS - 1), n_j).start()

    _w_copy(w_hbm, wbuf, sem, t, n_j).wait()
    w_ref = wbuf.at[t % W_SLOTS]

    @pl.when(j == 0)
    def _():
        x = x_ref[...]
        var = jnp.mean(x * x, axis=-1, keepdims=True)
        h = (x * lax.rsqrt(var + EPS) * g_ref[...]).astype(BF16)
        h_ref[...] = h
        f_ref[...] = _dot_nt(h, wf_ref[...])

    @pl.when(j < N_QK_GROUPS)
    def _():
        scale = jnp.where((j == 0) | (j == 2), SCALE * LOG2E, 1.0).astype(F32)
        for c0 in range(0, TN_IN, IN_CHUNK):
            c1 = c0 + IN_CHUNK
            w = w_ref[c0:c1, :].astype(BF16)
            qk_ref[:, c0:c1] = (_dot_nt(h_ref[...], w) * scale).astype(BF16)

    @pl.when(j >= N_QK_GROUPS)
    def _():
        for c0 in range(0, TN_IN, IN_CHUNK):
            c1 = c0 + IN_CHUNK
            w = w_ref[c0:c1, :].astype(BF16)
            vt_ref[c0:c1, :] = _dot_nt(w, h_ref[...]).astype(BF16)


def _in_proj(x2, g, w_qkv_t, w_f_t):
    m, d = x2.shape
    n = 6 * GROUP_W
    assert TN_IN == GROUP_W and w_qkv_t.shape[0] >= n
    return pl.pallas_call(
        _in_proj_kernel,
        grid=(m // TM_IN, n // TN_IN),
        in_specs=[
            pl.BlockSpec((TM_IN, d), lambda i, j: (i, 0)),
            pl.BlockSpec((1, d), lambda i, j: (0, 0)),
            pl.BlockSpec(memory_space=pl.ANY),
            pl.BlockSpec((LANES, d), lambda i, j: (0, 0)),
        ],
        out_specs=[
            pl.BlockSpec((TM_IN, TN_IN), lambda i, j: (i, jnp.minimum(j, N_QK_GROUPS - 1))),
            pl.BlockSpec((TN_IN, TM_IN), lambda i, j: (jnp.maximum(j - N_QK_GROUPS, 0), i)),
            pl.BlockSpec((TM_IN, LANES), lambda i, j: (i, 0)),
        ],
        out_shape=[
            jax.ShapeDtypeStruct((m, N_QK_GROUPS * GROUP_W), BF16),
            jax.ShapeDtypeStruct((2 * GROUP_W, m), BF16),
            jax.ShapeDtypeStruct((m, LANES), F32),
        ],
        scratch_shapes=[
            pltpu.VMEM((TM_IN, d), BF16),
            pltpu.VMEM((W_SLOTS, TN_IN, d), F32),
            pltpu.SemaphoreType.DMA((W_SLOTS,)),
        ],
        compiler_params=_params(("arbitrary", "arbitrary")),
        name="in_proj",
    )(x2, g, w_qkv_t, w_f_t)


def _split3(x):
    p1 = x.astype(BF16)
    r1 = x - p1.astype(F32)
    p2 = r1.astype(BF16)
    p3 = (r1 - p2.astype(F32)).astype(BF16)
    return p1, p2, p3


def _forget_cs_kernel(f_ref, b_ref, tri_ref, selq_ref, selk_ref, oneq_ref, onek_ref,
                      c_ref, extq_ref, extk_ref, carry_ref):
    n_blk = f_ref.shape[0] // CS_BLK
    tri = tri_ref[...]

    @pl.when(pl.program_id(1) == 0)
    def _():
        carry_ref[...] = jnp.zeros_like(carry_ref)

    def body(i, carry):
        r0 = pl.multiple_of(i * CS_BLK, CS_BLK)
        lf = _log_sigmoid(f_ref[pl.ds(r0, CS_BLK), :] + b_ref[...])
        p1, p2, p3 = _split3(lf)
        c = _dot(tri, p1) + _dot(tri, p2) + _dot(tri, p3) + carry
        c2 = c * LOG2E
        c_ref[pl.ds(r0, CS_BLK), :] = c2
        lane = lax.broadcasted_iota(jnp.int32, c2.shape, 1)
        packed = jnp.zeros_like(c2)
        for k, piece in enumerate(_split3(c2)):
            piece = jnp.where(lane < N_HEADS, piece.astype(F32), 0.0)
            packed = packed + (piece if k == 0 else pltpu.roll(piece, N_HEADS * k, axis=1))
        packed = packed.astype(BF16)
        extq_ref[pl.ds(r0, CS_BLK), :] = (_dot(packed, selq_ref[...]) + oneq_ref[...]).astype(BF16)
        extk_ref[pl.ds(r0, CS_BLK), :] = (_dot(packed, selk_ref[...]) + onek_ref[...]).astype(BF16)
        return c[CS_BLK - 1:CS_BLK, :]

    carry_ref[...] = lax.fori_loop(0, n_blk, body, carry_ref[...])


def _aug_constants():
    row = jnp.arange(LANES)
    col = jnp.arange(GROUP_W)
    piece, head = row // N_HEADS, row % N_HEADS
    blk, lane = col // HEAD_DIM, col % HEAD_DIM
    same = (head[:, None] == blk[None, :]) & (piece[:, None] < 3)
    selq = jnp.where(same & (lane[None, :] == 3 + piece[:, None]), 1.0, 0.0).astype(BF16)
    selk = jnp.where(same & (lane[None, :] == piece[:, None]), -1.0, 0.0).astype(BF16)
    oneq = jnp.where(lane < 3, 1.0, 0.0).astype(F32).reshape(1, GROUP_W)
    onek = jnp.where((lane >= 3) & (lane < 6), 1.0, 0.0).astype(F32).reshape(1, GROUP_W)
    return selq, selk, oneq, onek


def _forget_cs(f_logit, b_pad, tri_incl):
    b, s, _ = f_logit.shape
    assert s % CS_CHUNK == 0 and CS_CHUNK % CS_BLK == 0
    const = lambda shape: pl.BlockSpec(shape, lambda i, j: (0, 0))
    rows = lambda width: pl.BlockSpec((None, CS_CHUNK, width), lambda i, j: (i, j, 0))
    return pl.pallas_call(
        _forget_cs_kernel,
        grid=(b, s // CS_CHUNK),
        in_specs=[
            rows(LANES),
            const((1, LANES)),
            const((CS_BLK, CS_BLK)),
            const((LANES, GROUP_W)),
            const((LANES, GROUP_W)),
            const((1, GROUP_W)),
            const((1, GROUP_W)),
        ],
        out_specs=[rows(LANES), rows(GROUP_W), rows(GROUP_W)],
        out_shape=[
            jax.ShapeDtypeStruct((b, s, LANES), F32),
            jax.ShapeDtypeStruct((b, s, GROUP_W), BF16),
            jax.ShapeDtypeStruct((b, s, GROUP_W), BF16),
        ],
        scratch_shapes=[pltpu.VMEM((1, LANES), F32)],
        compiler_params=_params(("arbitrary", "arbitrary")),
        name="forget_cs",
    )(f_logit, b_pad, tri_incl, *_aug_constants())


def _col_reduce(x, pair, full):
    while x.shape[0] > 8:
        half = x.shape[0] // 2
        x = pair(x[:half], x[half:])
    return full(x, axis=0, keepdims=True)


def _head_out(acc_t, g_col):
    o = acc_t * lax.rsqrt(jnp.mean(acc_t * acc_t, axis=0, keepdims=True) + EPS) * g_col
    return jnp.transpose(o)


def _store_head_out(o_ref, acc_t, g_col):
    o_ref[...] = _head_out(acc_t, g_col).astype(o_ref.dtype)


def _cast_rows_spec(w, n_steps, step_of):
    rows, cols = w.shape
    assert rows % n_steps == 0 and (rows // n_steps) % 16 == 0
    return pl.BlockSpec((rows // n_steps, cols), lambda bi, h, qi: (step_of(bi, h, qi), 0))


def _cast_side_job(w_refs, wb_refs):
    for w_ref, wb_ref in zip(w_refs, wb_refs):
        wb_ref[...] = w_ref[...].astype(BF16)


def _sb_kernel(n_cast, q_ref, k_ref, vt_ref, g_ref, u2_ref, *rest):
    o_ref = rest[n_cast]
    _cast_side_job(rest[:n_cast], rest[n_cast + 1:])

    qi = pl.program_id(2)
    u2 = u2_ref[...]
    n_sub = TQ_S // TK_S
    key = lax.broadcasted_iota(jnp.int32, (TK_S, TK_S), 0)
    qry = lax.broadcasted_iota(jnp.int32, (TK_S, TK_S), 1)
    causal = key < qry

    def stage_q(q, kb):
        ks = pl.multiple_of(kb * TK_S, TK_S)
        return _dot_nt(k_ref[pl.ds(ks, TK_S), :], q)

    def stage_e(z, diagonal):
        zn = jnp.minimum(z, 0.0)
        zp = zn - z
        sp = jnp.log2(1.0 + jnp.exp2(zn + zp))
        lb = zn - sp
        lk = zp - sp
        if diagonal:
            lk = jnp.where(causal, lk, 0.0)
        hi = lk.astype(BF16)
        lo = (lk - hi.astype(F32)).astype(BF16)
        return lb, jnp.concatenate([hi, lo], axis=0), lk[:1, :]

    def stage_c(hilo, lk_first):
        rest = _dot(u2, hilo)
        return rest, rest[:1, :] + lk_first

    def stage_x(lb, rest, carry, diagonal):
        if diagonal:
            a = jnp.where(causal, jnp.exp2(lb + rest), 0.0)
        else:
            a = jnp.exp2(lb + (rest + carry))
        return a.astype(BF16)

    def stage_v(kb, a, gate=None):
        vt = vt_ref[:, pl.ds(pl.multiple_of(kb * TK_S, TK_S), TK_S)]
        if gate is not None:
            vt = vt * gate
        return _dot(vt, a)

    items = []
    for sub in range(n_sub):
        kd = qi * n_sub + sub
        q = q_ref[sub * TK_S:(sub + 1) * TK_S, :]
        items.append(dict(q=q, kb=kd, diagonal=True, gate=None))
        items.append(dict(q=q, kb=jnp.maximum(kd - 1, 0), diagonal=False,
                          gate=jnp.where(kd > 0, 1.0, 0.0).astype(BF16)))
    n_items = len(items)
    for t in range(n_items + 4):
        if 4 <= t:
            it = items[t - 4]
            it["pv"] = stage_v(it["kb"], it["a"], it["gate"])
        if 2 <= t < n_items + 2:
            it = items[t - 2]
            it["rest"], it["colsum"] = stage_c(it["hilo"], it["lk0"])
        if t < n_items:
            it = items[t]
            it["z"] = stage_q(it["q"], it["kb"])
        if 3 <= t < n_items + 3:
            it = items[t - 3]
            carry_in = None if it["diagonal"] else items[t - 4]["colsum"]
            it["a"] = stage_x(it["lb"], it["rest"], carry_in, it["diagonal"])
        if 1 <= t < n_items + 1:
            it = items[t - 1]
            it["lb"], it["hilo"], it["lk0"] = stage_e(it["z"], it["diagonal"])

    qs = [items[2 * r]["q"] for r in range(n_sub)]
    state = []
    for r in range(n_sub):
        d, p = items[2 * r], items[2 * r + 1]
        state += [d["colsum"] + p["colsum"], d["pv"] + p["pv"]]

    kd_last = qi * n_sub + n_sub - 1

    def cond(st):
        worst = st[1]
        for r in range(1, n_sub):
            worst = jnp.maximum(worst, st[1 + 2 * r])
        return jnp.logical_and(kd_last - 2 - st[0] >= 0, jnp.max(worst) > SB_DEAD_LOG2)

    def body(st):
        j, out = st[0], []
        for r in range(n_sub):
            kb = qi * n_sub + r - 2 - j
            gate = jnp.where(kb >= 0, 1.0, 0.0).astype(BF16)
            kb = jnp.maximum(kb, 0)
            lb, hilo, lk0 = stage_e(stage_q(qs[r], kb), False)
            rest, colsum = stage_c(hilo, lk0)
            a = stage_x(lb, rest, st[1 + 2 * r], False)
            out += [st[1 + 2 * r] + colsum, st[2 + 2 * r] + stage_v(kb, a, gate)]
        return (j + 1, *out)

    st = lax.while_loop(cond, body, (0, *state))
    _store_head_out(o_ref, jnp.concatenate([st[2 + 2 * r] for r in range(n_sub)], axis=1), g_ref[...])


def _sb_attn(qk, v_t, g_col, u2, w_cast):
    b, s, _ = qk.shape
    assert TQ_S % TK_S == 0
    nq = s // TQ_S
    w_specs = [_cast_rows_spec(w, b * N_HEADS * nq, lambda bi, h, qi: (bi * N_HEADS + h) * nq + qi)
               for w in w_cast]
    return pl.pallas_call(
        functools.partial(_sb_kernel, len(w_cast)),
        grid=(b, N_HEADS, nq),
        in_specs=[
            pl.BlockSpec((None, TQ_S, HEAD_DIM), lambda bi, h, qi: (bi, qi, h)),
            pl.BlockSpec((None, s, HEAD_DIM), lambda bi, h, qi: (bi, 0, N_HEADS + h)),
            pl.BlockSpec((None, HEAD_DIM, s), lambda bi, h, qi: (h, 0, bi)),
            pl.BlockSpec((HEAD_DIM, 1), lambda bi, h, qi: (h, 0)),
            pl.BlockSpec((TK_S, 2 * TK_S), lambda bi, h, qi: (0, 0)),
            *w_specs,
        ],
        out_specs=[pl.BlockSpec((None, TQ_S, HEAD_DIM), lambda bi, h, qi: (bi, qi, h)), *w_specs],
        out_shape=[jax.ShapeDtypeStruct((b, s, GROUP_W), BF16),
                   *[jax.ShapeDtypeStruct(w.shape, BF16) for w in w_cast]],
        compiler_params=_params(("arbitrary", "arbitrary", "arbitrary")),
        name="sb_attn",
    )(qk, qk, v_t, g_col, u2, *w_cast)


def _fox_kernel(n_cast, q_ref, k_ref, extq_ref, extk_ref, vt_ref, cq_ref, clast_ref, g_ref, *rest):
    o_ref = rest[n_cast]
    _cast_side_job(rest[:n_cast], rest[n_cast + 1:2 * n_cast + 1])
    kn_ref, z_ref, p_ref, m_ref, l_ref, a_ref, acc_ref = rest[2 * n_cast + 1:]

    bi, h, qs = pl.program_id(0), pl.program_id(1), pl.program_id(2)
    last_tile = vt_ref.shape[1] // TK_F - 1

    def sq_norms_ub(rows):
        rf = rows.astype(F32)
        r2_up = (rf * rf * (1.0 + 2.0 ** -7)).astype(BF16)
        return _dot_nt(jnp.ones((8, HEAD_DIM), BF16), r2_up)[:1]

    @pl.when(qs == 0)
    def _():
        best = sq_norms_ub(k_ref[:KN_BLK, :])
        for r0 in range(KN_BLK, k_ref.shape[0], KN_BLK):
            best = jnp.maximum(best, sq_norms_ub(k_ref[r0:r0 + KN_BLK, :]))
        kn_ref[...] = jnp.broadcast_to(jnp.sqrt(jnp.max(best, axis=1, keepdims=True)), kn_ref.shape)

    def make_block(sub):
        qi = qs * N_SUB_F + sub
        rows = slice(sub * TQ_F, (sub + 1) * TQ_F)
        n_full = (qi * TQ_F) // TK_F
        q = q_ref[rows, :]
        q_aug = jnp.concatenate([q, extq_ref[rows, :]], axis=1)
        ub = jnp.sqrt(sq_norms_ub(q)) * kn_ref[:, :1] * NORM_SLACK + cq_ref[:, rows]

        def kb_of(i):
            return jnp.clip(n_full - i, 0, last_tile)

        def qk_t(i):
            ks = pl.multiple_of(kb_of(i) * TK_F, TK_F)
            k_aug = jnp.concatenate([k_ref[pl.ds(ks, TK_F), :], extk_ref[pl.ds(ks, TK_F), :]], axis=1)
            return _dot_nt(k_aug, q_aug)

        def pv_t(i, slot):
            ks = pl.multiple_of(kb_of(i) * TK_F, TK_F)
            return _dot(vt_ref[:, pl.ds(ks, TK_F)], p_ref[sub, slot])

        def prologue():
            st = qk_t(0)
            z_ref[sub, 1] = qk_t(1)
            key = n_full * TK_F + lax.broadcasted_iota(jnp.int32, (TK_F, TQ_F), 0)
            qry = qi * TQ_F + lax.broadcasted_iota(jnp.int32, (TK_F, TQ_F), 1)
            st = jnp.where(key <= qry, st, -jnp.inf)
            m0 = _col_reduce(st, jnp.maximum, jnp.max)
            p0 = jnp.exp2(st - m0)
            p_ref[sub, 0] = p0.astype(BF16)
            m_ref[sub] = m0
            l_ref[sub] = _col_reduce(p0, jnp.add, jnp.sum)
            a_ref[sub] = jnp.ones_like(m0)
            acc_ref[sub] = jnp.zeros(acc_ref.shape[1:], F32)
            gap = jnp.max(ub - m0)
            n_tiles = 1
            for i in range(clast_ref.shape[2]):
                alive = jnp.logical_and(i < n_full, gap - clast_ref[bi, h, i] > FOX_DEAD_LOG2)
                n_tiles = n_tiles + alive.astype(jnp.int32)
            return n_tiles

        def step(i, slot, last=False):
            if not last:
                z_ref[sub, 1 - slot] = qk_t(i + 1)
            acc_ref[sub] = a_ref[sub] * acc_ref[sub] + pv_t(i - 1, 1 - slot)
            m = m_ref[sub]
            m_new = jnp.maximum(m, _col_reduce(z_ref[sub, slot], jnp.maximum, jnp.max))
            alpha = jnp.exp2(m - m_new)
            p = jnp.exp2(z_ref[sub, slot] - m_new)
            p_ref[sub, slot] = p.astype(BF16)
            l_ref[sub] = alpha * l_ref[sub] + _col_reduce(p, jnp.add, jnp.sum)
            m_ref[sub] = m_new
            a_ref[sub] = alpha

        def walk(n_tiles):
            n_in = n_tiles - 1

            def body(j, _):
                i = 1 + 2 * j
                step(i, 1)
                step(i + 1, 0)
                return 0

            lax.fori_loop(0, n_in // 2, body, 0)

            @pl.when(n_in % 2 == 1)
            def _():
                step(n_in, 1, last=True)

        def epilogue(n_tiles):
            acc = a_ref[sub] * acc_ref[sub] + pv_t(n_tiles - 1, (n_tiles - 1) % 2)
            o_ref[rows, :] = _head_out(acc / l_ref[sub], g_ref[...]).astype(o_ref.dtype)

        return prologue, walk, epilogue

    blocks = [make_block(sub) for sub in range(N_SUB_F)]
    n_tiles = [prologue() for prologue, _, _ in blocks]
    for (_, walk, _), n in zip(blocks, n_tiles):
        walk(n)
    for (_, _, epilogue), n in zip(blocks, n_tiles):
        epilogue(n)


def _fox_attn(qk, v_t, c, ext_q, ext_k, g_col, w_cast):
    b, s, _ = qk.shape
    base = 2 * N_HEADS
    n_kt = s // TK_F
    tq_step = N_SUB_F * TQ_F
    nq = s // tq_step
    assert TK_F % TQ_F == 0 and s % tq_step == 0
    step_of = lambda bi, h, qi: (bi * N_HEADS + h) * nq + qi
    w_specs = [_cast_rows_spec(w, b * N_HEADS * nq, step_of) for w in w_cast]
    c_hs = jnp.transpose(c[:, :, :N_HEADS], (0, 2, 1))
    c_q = c_hs.reshape(b, N_HEADS, nq, 1, tq_step)
    c_last = c_hs.reshape(b, N_HEADS, n_kt, TK_F)[..., TK_F - 1]
    return pl.pallas_call(
        functools.partial(_fox_kernel, len(w_cast)),
        grid=(b, N_HEADS, nq),
        in_specs=[
            pl.BlockSpec((None, tq_step, HEAD_DIM), lambda bi, h, qi: (bi, qi, base + h)),
            pl.BlockSpec((None, s, HEAD_DIM), lambda bi, h, qi: (bi, 0, base + N_HEADS + h)),
            pl.BlockSpec((None, tq_step, HEAD_DIM), lambda bi, h, qi: (bi, qi, h)),
            pl.BlockSpec((None, s, HEAD_DIM), lambda bi, h, qi: (bi, 0, h)),
            pl.BlockSpec((None, HEAD_DIM, s), lambda bi, h, qi: (N_HEADS + h, 0, bi)),
            pl.BlockSpec((None, None, None, 1, tq_step), lambda bi, h, qi: (bi, h, qi, 0, 0)),
            pl.BlockSpec(memory_space=pltpu.SMEM),
            pl.BlockSpec((HEAD_DIM, 1), lambda bi, h, qi: (h, 0)),
            *w_specs,
        ],
        out_specs=[pl.BlockSpec((None, tq_step, HEAD_DIM), lambda bi, h, qi: (bi, qi, h)), *w_specs],
        scratch_shapes=[
            pltpu.VMEM((1, LANES), F32),
            pltpu.VMEM((N_SUB_F, 2, TK_F, TQ_F), F32),
            pltpu.VMEM((N_SUB_F, 2, TK_F, TQ_F), BF16),
            pltpu.VMEM((N_SUB_F, 1, TQ_F), F32),
            pltpu.VMEM((N_SUB_F, 1, TQ_F), F32),
            pltpu.VMEM((N_SUB_F, 1, TQ_F), F32),
            pltpu.VMEM((N_SUB_F, HEAD_DIM, TQ_F), F32),
        ],
        out_shape=[jax.ShapeDtypeStruct((b, s, GROUP_W), BF16),
                   *[jax.ShapeDtypeStruct(w.shape, BF16) for w in w_cast]],
        compiler_params=_params(("arbitrary", "arbitrary", "arbitrary")),
        name="fox_attn",
    )(qk, qk, ext_q, ext_k, v_t, c_q, c_last, g_col, *w_cast)


def _out_proj_kernel(ms_ref, mf_ref, w_ref, x_ref, g_ref, x1_ref, h2_ref):
    acc = _dot(jnp.concatenate([ms_ref[...], mf_ref[...]], axis=1), w_ref[...])
    x1 = x_ref[...] + acc
    x1_ref[...] = x1
    var = jnp.mean(x1 * x1, axis=-1, keepdims=True)
    h2_ref[...] = (x1 * lax.rsqrt(var + EPS) * g_ref[...]).astype(BF16)


def _out_proj(mixed_sb, mixed_fox, w_out, x2, g_mlp):
    m, d = x2.shape
    return pl.pallas_call(
        _out_proj_kernel,
        grid=(m // TM_OUT,),
        in_specs=[
            pl.BlockSpec((TM_OUT, GROUP_W), lambda i: (i, 0)),
            pl.BlockSpec((TM_OUT, GROUP_W), lambda i: (i, 0)),
            pl.BlockSpec((2 * GROUP_W, d), lambda i: (0, 0)),
            pl.BlockSpec((TM_OUT, d), lambda i: (i, 0)),
            pl.BlockSpec((1, d), lambda i: (0, 0)),
        ],
        out_specs=[
            pl.BlockSpec((TM_OUT, d), lambda i: (i, 0)),
            pl.BlockSpec((TM_OUT, d), lambda i: (i, 0)),
        ],
        out_shape=[
            jax.ShapeDtypeStruct((m, d), F32),
            jax.ShapeDtypeStruct((m, d), BF16),
        ],
        compiler_params=_params(("arbitrary",)),
        name="out_proj",
    )(mixed_sb, mixed_fox, w_out, x2, g_mlp)


def _mlp_kernel(h_ref, x1_ref, wu_ref, wd_ref, g_ref, o_ref):
    f = pl.program_id(1)

    @pl.when(f == 0)
    def _():
        o_ref[...] = x1_ref[...]

    for c0 in range(0, wu_ref.shape[1], MLP_CHUNK):
        c1 = c0 + MLP_CHUNK
        u = jnp.maximum(_dot(h_ref[...], wu_ref[:, c0:c1]), 0.0)
        o_ref[...] += _dot((u * u).astype(BF16), wd_ref[c0:c1, :])

    @pl.when(f == pl.num_programs(1) - 1)
    def _():
        x2 = o_ref[...]
        var = jnp.mean(x2 * x2, axis=-1, keepdims=True)
        o_ref[...] = x2 * lax.rsqrt(var + EPS) * g_ref[...]


def _mlp(h2, x1, w_up, w_down, g_final):
    m, d = x1.shape
    dff = w_up.shape[1]
    return pl.pallas_call(
        _mlp_kernel,
        grid=(m // TM_MLP, dff // TF_MLP),
        in_specs=[
            pl.BlockSpec((TM_MLP, d), lambda i, f: (i, 0)),
            pl.BlockSpec((TM_MLP, d), lambda i, f: (i, 0)),
            pl.BlockSpec((d, TF_MLP), lambda i, f: (0, f)),
            pl.BlockSpec((TF_MLP, d), lambda i, f: (f, 0)),
            pl.BlockSpec((1, d), lambda i, f: (0, 0)),
        ],
        out_specs=pl.BlockSpec((TM_MLP, d), lambda i, f: (i, 0)),
        out_shape=jax.ShapeDtypeStruct((m, d), F32),
        compiler_params=_params(("arbitrary", "arbitrary"), VMEM_LIMIT_MLP),
        name="mlp",
    )(h2, x1, w_up, w_down, g_final)


def kernel(x, g_attn, w_in, b_f, g_out_sb, g_out_fox, w_out, g_mlp, w_up, w_down, g_final):
    b, s, d = x.shape
    n_qkv = 6 * GROUP_W
    assert s % TQ_S == 0 and s % TK_F == 0 and (b * s) % TM_IN == 0
    x2 = x.reshape(b * s, d)

    idx = jnp.arange(TK_S)
    u_excl = (idx[None, :] > idx[:, None]).astype(BF16)
    u2 = jnp.concatenate([u_excl, u_excl], axis=1)
    cidx = jnp.arange(CS_BLK)
    tri_incl = (cidx[None, :] <= cidx[:, None]).astype(BF16)

    gw = GROUP_W
    for l in range(g_attn.shape[0]):
        w_t = jnp.swapaxes(w_in[l], 0, 1)
        w_f_t = jnp.pad(w_t[n_qkv:], ((0, LANES - N_HEADS), (0, 0))).astype(BF16)
        b_pad = jnp.pad(b_f[l], (0, LANES - N_HEADS)).reshape(1, LANES)

        qk, v_t, f_logit = _in_proj(x2, g_attn[l].reshape(1, d), w_t, w_f_t)
        qk = qk.reshape(b, s, N_QK_GROUPS * gw)
        v_t = v_t.reshape(2 * N_HEADS, HEAD_DIM, b * s)
        c, ext_q, ext_k = _forget_cs(f_logit.reshape(b, s, LANES), b_pad, tri_incl)

        mixed_sb, w_up_b, w_down_b = _sb_attn(qk, v_t, g_out_sb[l].reshape(gw, 1), u2,
                                              (w_up[l], w_down[l]))
        mixed_fox, w_out_b = _fox_attn(qk, v_t, c, ext_q, ext_k, g_out_fox[l].reshape(gw, 1), (w_out[l],))

        x1, h2 = _out_proj(mixed_sb.reshape(b * s, GROUP_W), mixed_fox.reshape(b * s, GROUP_W),
                           w_out_b, x2, g_mlp[l].reshape(1, d))
        assert g_attn.shape[0] == 1
        x2 = _mlp(h2, x1, w_up_b, w_down_b, g_final.reshape(1, d))
    return x2.reshape(b, s, d)
```
